```python
import numpy as np
import jax
import jax.numpy as jnp
from jax import lax

D_MODEL = 1024
BATCH = 2
SEQ = 8192
DEPTH = 1

NSA_HEADS = 8
NSA_KV_GROUPS = 2
NSA_HPG = NSA_HEADS // NSA_KV_GROUPS
NSA_HEAD_DIM = 64
N_NSA_BRANCH = 3
CMP_LEN = 32
CMP_STRIDE = 16
CMP_HIDDEN = 128
SLC_LEN = 64
SLC_TOPK = 16
WINDOW = 512
Q_BLOCK = 128
FORCE_SCORE = 1.0e4

GLA_HEADS = 4
GLA_DK = 64
GLA_DV = 128
GLA_GATE_RANK = 16
GLA_GATE_TAU = 16.0
GLA_CHUNK = 64

ROPE_THETA = 500000.0
ROPE_DIM = NSA_HEAD_DIM // 4

D_FF = (8 * D_MODEL + 3 * 256 - 1) // (3 * 256) * 256

EPS = 1e-6
NEG_INF = -1e30

NSA_Q_COLS = NSA_HEADS * NSA_HEAD_DIM
NSA_KV_COLS = N_NSA_BRANCH * 2 * NSA_KV_GROUPS * NSA_HEAD_DIM
NSA_GATE_COLS = NSA_HEADS * N_NSA_BRANCH
GLA_QK_COLS = GLA_HEADS * GLA_DK
GLA_V_COLS = GLA_HEADS * GLA_DV
MERGE_COLS = 2 * D_MODEL
IN_COLS = NSA_Q_COLS + NSA_KV_COLS + NSA_GATE_COLS + 2 * GLA_QK_COLS + 2 * GLA_V_COLS + GLA_GATE_RANK + MERGE_COLS

kernel_name = 'hybrid_nsa_gla_gated_merge_block'


def rmsnorm(x, gain):
    xf = x.astype(jnp.float32)
    y = xf * lax.rsqrt(jnp.mean(xf * xf, axis=-1, keepdims=True) + EPS)
    return (y * gain.astype(jnp.float32)).astype(x.dtype)


def partial_rope(x, pos):
    half = ROPE_DIM // 2
    inv_freq = ROPE_THETA ** (-jnp.arange(half, dtype=jnp.float32) / half)
    ang = pos.astype(jnp.float32)[:, None] * inv_freq[None, :]
    cos = jnp.cos(ang).astype(x.dtype)
    sin = jnp.sin(ang).astype(x.dtype)
    x1 = x[..., :half]
    x2 = x[..., half:ROPE_DIM]
    return jnp.concatenate([x1 * cos - x2 * sin, x2 * cos + x1 * sin, x[..., ROPE_DIM:]], axis=-1)


def masked_softmax(scores, mask):
    p = jax.nn.softmax(jnp.where(mask, scores, NEG_INF), axis=-1)
    return p * mask


def compress_blocks(kv, pe, w1, w2):
    b, g, t, d = kv.shape
    ratio = CMP_LEN // CMP_STRIDE
    n_sub = t // CMP_STRIDE
    n_cmp = n_sub - ratio + 1
    sub = kv.reshape(b, g, n_sub, CMP_STRIDE, d)
    blocks = jnp.concatenate([sub[:, :, r:r + n_cmp] for r in range(ratio)], axis=3)
    blocks = (blocks + pe).reshape(b, g, n_cmp, CMP_LEN * d)
    return jax.nn.silu(blocks @ w1) @ w2


def selection_overlap(n_cmp, n_slc):
    c_start = np.arange(n_cmp)[:, None] * CMP_STRIDE
    s_start = np.arange(n_slc)[None, :] * SLC_LEN
    ov = (c_start < s_start + SLC_LEN) & (c_start + CMP_LEN > s_start)
    return jnp.asarray(ov, dtype=jnp.float32)


def nsa_attention(q, k_cmp, v_cmp, k_slc, v_slc, k_win, v_win, gates):
    b, g, hg, t_len, d = q.shape
    pos = jnp.arange(t_len)
    scale = NSA_HEAD_DIM ** -0.5
    q_rot = partial_rope(q, pos)
    k_slc = partial_rope(k_slc, pos)
    k_win = partial_rope(k_win, pos)
    n_cmp = k_cmp.shape[2]
    cmp_end = jnp.arange(n_cmp) * CMP_STRIDE + CMP_LEN - 1
    n_slc = t_len // SLC_LEN
    topk = min(SLC_TOPK, n_slc)
    overlap = selection_overlap(n_cmp, n_slc)
    kb = k_slc.reshape(b, g, n_slc, SLC_LEN, d)
    vb = v_slc.reshape(b, g, n_slc, SLC_LEN, d)
    kw = jnp.pad(k_win, ((0, 0), (0, 0), (WINDOW, 0), (0, 0)))
    vw = jnp.pad(v_win, ((0, 0), (0, 0), (WINDOW, 0), (0, 0)))
    bi = jnp.arange(b)[:, None, None, None]
    gi = jnp.arange(g)[None, :, None, None]
    blk_ids = jnp.arange(n_slc)

    def query_block(i):
        s = i * Q_BLOCK
        t = s + jnp.arange(Q_BLOCK)
        q_raw = lax.dynamic_slice_in_dim(q, s, Q_BLOCK, axis=3) * scale
        q_pos = lax.dynamic_slice_in_dim(q_rot, s, Q_BLOCK, axis=3) * scale
        sc = jnp.einsum('bghqd,bgnd->bghqn', q_raw, k_cmp, preferred_element_type=jnp.float32)
        p_c = masked_softmax(sc, cmp_end[None, :] <= t[:, None])
        o_c = jnp.einsum('bghqn,bgnd->bghqd', p_c.astype(v_cmp.dtype), v_cmp)
        imp = jnp.einsum('bgqn,nj->bgqj', p_c.sum(axis=2), overlap)
        cur = t[:, None] // SLC_LEN
        valid = blk_ids[None, :] * SLC_LEN <= t[:, None]
        forced = (blk_ids[None, :] == 0) | (blk_ids[None, :] == cur) | (blk_ids[None, :] == cur - 1)
        score = jnp.where(valid, jnp.where(forced, FORCE_SCORE, imp), -jnp.inf)
        top_v, idx = lax.top_k(score, topk)
        ks = kb[bi, gi, idx]
        vs = vb[bi, gi, idx].reshape(b, g, Q_BLOCK, topk * SLC_LEN, d)
        ss = jnp.einsum('bghqd,bgqnld->bghqnl', q_pos, ks, preferred_element_type=jnp.float32)
        ss = ss.reshape(b, g, hg, Q_BLOCK, topk * SLC_LEN)
        tok = idx[..., None] * SLC_LEN + jnp.arange(SLC_LEN)
        m_s = (jnp.isfinite(top_v)[..., None] & (tok <= t[None, None, :, None, None]))
        m_s = m_s.reshape(b, g, 1, Q_BLOCK, topk * SLC_LEN)
        p_s = masked_softmax(ss, m_s)
        o_s = jnp.einsum('bghqm,bgqmd->bghqd', p_s.astype(vs.dtype), vs)
        kwin = lax.dynamic_slice_in_dim(kw, s, WINDOW + Q_BLOCK, axis=2)
        vwin = lax.dynamic_slice_in_dim(vw, s, WINDOW + Q_BLOCK, axis=2)
        kpos = s - WINDOW + jnp.arange(WINDOW + Q_BLOCK)
        dist = t[:, None] - kpos[None, :]
        m_w = (kpos[None, :] >= 0) & (dist >= 0) & (dist < WINDOW)
        sw = jnp.einsum('bghqd,bgkd->bghqk', q_pos, kwin, preferred_element_type=jnp.float32)
        p_w = masked_softmax(sw, m_w)
        o_w = jnp.einsum('bghqk,bgkd->bghqd', p_w.astype(vwin.dtype), vwin)
        gt = lax.dynamic_slice_in_dim(gates, s, Q_BLOCK, axis=3)
        return gt[..., 0:1] * o_c + gt[..., 1:2] * o_s + gt[..., 2:3] * o_w

    out = lax.map(query_block, jnp.arange(t_len // Q_BLOCK))
    return out.transpose(1, 0, 4, 2, 3, 5).reshape(b, t_len, g * hg * d)


def gla_chunked(q, k, v, log_a):
    b, h, t_len, dk = q.shape
    dv = v.shape[-1]
    n_chunk = t_len // GLA_CHUNK

    def to_chunks(a):
        return jnp.moveaxis(a.astype(jnp.float32).reshape(b, h, n_chunk, GLA_CHUNK, a.shape[-1]), 2, 0)

    causal = jnp.tril(jnp.ones((GLA_CHUNK, GLA_CHUNK), dtype=bool))

    def step(state, inp):
        qc, kc, vc, gc = inp
        cum = jnp.cumsum(gc, axis=2)
        last = cum[:, :, -1:, :]
        inter = jnp.einsum('bhtd,bhdv->bhtv', qc * jnp.exp(cum), state)
        diff = jnp.where(causal[:, :, None], cum[:, :, :, None, :] - cum[:, :, None, :, :], -jnp.inf)
        attn = jnp.einsum('bhtd,bhsd,bhtsd->bhts', qc, kc, jnp.exp(diff))
        out = inter + jnp.einsum('bhts,bhsv->bhtv', attn, vc)
        state = jnp.exp(last)[:, :, 0, :, None] * state + jnp.einsum('bhsd,bhsv->bhdv', kc * jnp.exp(last - cum), vc)
        return state, out

    s0 = jnp.zeros((b, h, dk, dv), jnp.float32)
    _, o = lax.scan(step, s0, (to_chunks(q), to_chunks(k), to_chunks(v), to_chunks(log_a)))
    return jnp.moveaxis(o, 0, 2).reshape(b, h, t_len, dv)


def split_in_proj(proj):
    widths = [NSA_Q_COLS, NSA_KV_COLS, NSA_GATE_COLS, GLA_QK_COLS, GLA_QK_COLS, GLA_V_COLS,
              GLA_GATE_RANK, GLA_V_COLS, MERGE_COLS]
    cuts = [int(c) for c in np.cumsum(widths)[:-1]]
    return jnp.split(proj, cuts, axis=-1)


def setup_inputs(seed: int = 0) -> dict:
    key = jax.random.key(seed)
    ks = jax.random.split(key, 20)
    L = DEPTH

    def nrm(k, shape, scale):
        return jax.random.normal(k, shape, jnp.float32) * scale

    flat_blk = CMP_LEN * NSA_HEAD_DIM
    return {
        'x': nrm(ks[0], (BATCH, SEQ, D_MODEL), 1.0),
        'norm_mix': 1.0 + nrm(ks[1], (L, D_MODEL), 0.02),
        'w_in': nrm(ks[2], (L, D_MODEL, IN_COLS), D_MODEL ** -0.5),
        'cmp_pe_k': nrm(ks[3], (L, CMP_LEN, NSA_HEAD_DIM), 0.1),
        'cmp_pe_v': nrm(ks[4], (L, CMP_LEN, NSA_HEAD_DIM), 0.1),
        'cmp_k_w1': nrm(ks[5], (L, flat_blk, CMP_HIDDEN), flat_blk ** -0.5),
        'cmp_k_w2': nrm(ks[6], (L, CMP_HIDDEN, NSA_HEAD_DIM), CMP_HIDDEN ** -0.5),
        'cmp_v_w1': nrm(ks[7], (L, flat_blk, CMP_HIDDEN), flat_blk ** -0.5),
        'cmp_v_w2': nrm(ks[8], (L, CMP_HIDDEN, NSA_HEAD_DIM), CMP_HIDDEN ** -0.5),
        'gla_gate_w2': nrm(ks[9], (L, GLA_GATE_RANK, GLA_QK_COLS), GLA_GATE_RANK ** -0.5),
        'gla_gate_b': nrm(ks[10], (L, GLA_QK_COLS), 0.1),
        'gla_norm': 1.0 + nrm(ks[11], (L, GLA_DV), 0.02),
        'w_up_nsa': nrm(ks[12], (L, NSA_Q_COLS, D_MODEL), NSA_Q_COLS ** -0.5),
        'w_up_gla': nrm(ks[13], (L, GLA_V_COLS, D_MODEL), GLA_V_COLS ** -0.5),
        'w_out': nrm(ks[14], (L, D_MODEL, D_MODEL), D_MODEL ** -0.5),
        'norm_ffn': 1.0 + nrm(ks[15], (L, D_MODEL), 0.02),
        'w_ffn_gate': nrm(ks[16], (L, D_MODEL, D_FF), D_MODEL ** -0.5),
        'w_ffn_up': nrm(ks[17], (L, D_MODEL, D_FF), D_MODEL ** -0.5),
        'w_ffn_down': nrm(ks[18], (L, D_FF, D_MODEL), D_FF ** -0.5),
        'norm_final': 1.0 + nrm(ks[19], (D_MODEL,), 0.02),
    }


def reference(x, norm_mix, w_in, cmp_pe_k, cmp_pe_v, cmp_k_w1, cmp_k_w2, cmp_v_w1, cmp_v_w2,
              gla_gate_w2, gla_gate_b, gla_norm, w_up_nsa, w_up_gla, w_out, norm_ffn,
              w_ffn_gate, w_ffn_up, w_ffn_down, norm_final):
    b, t_len, _ = x.shape
    for layer in range(DEPTH):
        h = rmsnorm(x, norm_mix[layer])
        proj = h @ w_in[layer]
        q_nsa, kv_nsa, g_nsa, q_gla, k_gla, v_gla, lr_gla, r_gla, g_merge = split_in_proj(proj)
        q_a = q_nsa.reshape(b, t_len, NSA_KV_GROUPS, NSA_HPG, NSA_HEAD_DIM).transpose(0, 2, 3, 1, 4)
        kv = kv_nsa.reshape(b, t_len, 2 * N_NSA_BRANCH, NSA_KV_GROUPS, NSA_HEAD_DIM).transpose(2, 0, 3, 1, 4)
        k_cmp = compress_blocks(kv[0], cmp_pe_k[layer], cmp_k_w1[layer], cmp_k_w2[layer])
        v_cmp = compress_blocks(kv[1], cmp_pe_v[layer], cmp_v_w1[layer], cmp_v_w2[layer])
        gates_a = jax.nn.sigmoid(g_nsa).reshape(b, t_len, NSA_KV_GROUPS, NSA_HPG, N_NSA_BRANCH).transpose(0, 2, 3, 1, 4)
        o_a = nsa_attention(q_a, k_cmp, v_cmp, kv[2], kv[3], kv[4], kv[5], gates_a)
        log_a = jax.nn.log_sigmoid((lr_gla @ gla_gate_w2[layer] + gla_gate_b[layer]).astype(jnp.float32)) / GLA_GATE_TAU
        heads = lambda a, dh: a.reshape(b, t_len, GLA_HEADS, dh).transpose(0, 2, 1, 3)
        o_b = gla_chunked(heads(q_gla * GLA_DK ** -0.5, GLA_DK), heads(k_gla, GLA_DK),
                          heads(v_gla, GLA_DV), heads(log_a, GLA_DK))
        o_b = rmsnorm(o_b.transpose(0, 2, 1, 3).astype(x.dtype), gla_norm[layer])
        o_b = (o_b * jax.nn.silu(r_gla.reshape(b, t_len, GLA_HEADS, GLA_DV))).reshape(b, t_len, GLA_V_COLS)
        gate_a, gate_b = jnp.split(jax.nn.sigmoid(g_merge), 2, axis=-1)
        merged = gate_a * (o_a @ w_up_nsa[layer]) + gate_b * (o_b @ w_up_gla[layer])
        x = x + merged @ w_out[layer]
        h = rmsnorm(x, norm_ffn[layer])
        x = x + (jax.nn.silu(h @ w_ffn_gate[layer]) * (h @ w_ffn_up[layer])) @ w_ffn_down[layer]
    return rmsnorm(x, norm_final)
```

```python
import functools

import numpy as np
import jax
import jax.numpy as jnp
from jax import lax
from jax.experimental import pallas as pl
from jax.experimental.pallas import tpu as pltpu

F32 = jnp.float32
BF16 = jnp.bfloat16

D_MODEL = 1024
NSA_HEADS = 8
NSA_GROUPS = 2
HPG = NSA_HEADS // NSA_GROUPS
DH = 64
CMP_LEN = 32
CMP_STRIDE = 16
CMP_HIDDEN = 128
SLC_LEN = 64
SLC_TOPK = 16
WINDOW = 512
FORCE_SCORE = 1.0e4
GLA_HEADS = 4
GLA_DK = 64
GLA_DV = 128
GLA_RANK = 16
GLA_TAU = 16.0
GLA_CHUNK = 64
GLA_SUB = 16
ROPE_THETA = 500000.0
ROPE_DIM = DH // 4
D_FF = 2816
EPS = 1e-6
NEG = -1e30

LANE = 128
VMEM_LIMIT = 56 * 1024 * 1024
SEL_HALF = 64

SEG_Q = 0
SEG_QSW = SEG_Q + 512
SEG_KV = SEG_QSW + 512
SEG_KSW = SEG_KV + 768
SEG_GATE = SEG_KSW + 256
SEG_GLA = SEG_GATE + 256
SEG_LR = SEG_GLA + 1536
SEG_MERGE = SEG_LR + 128
SEG_END = SEG_MERGE + 2048


def _dot(a, b):
    return jnp.dot(a, b, preferred_element_type=F32)


def _dot_nt(a, b):
    return lax.dot_general(a, b, (((1,), (1,)), ((), ())), preferred_element_type=F32)


def _sigmoid(x):
    return 1.0 / (1.0 + jnp.exp(-x))


def _in_proj_kernel(x_ref, gain_ref, w_ref, cos_ref, sin_ref, w2_ref, b2_ref,
                    qraw_ref, qrot_ref, cmp_ref, kaug_ref, vslc_ref, kwin_ref, vwin_ref,
                    gate_ref, gq_ref, gk_ref, gv_ref, ga_ref, gr_ref, mg_ref, *, tm, seq):
    i = pl.program_id(0)
    x = x_ref[...]
    ms = jnp.mean(x * x, axis=-1, keepdims=True)
    h = (x * lax.rsqrt(ms + EPS) * gain_ref[...]).astype(BF16)

    def proj(a, b):
        return _dot(h, w_ref[:, a:b])

    cos = cos_ref[...]
    sin = sin_ref[...]
    cos4 = jnp.concatenate([cos] * 4, axis=1)
    sin4 = jnp.concatenate([sin] * 4, axis=1)

    scale = DH ** -0.5
    q = proj(SEG_Q, SEG_Q + 512)
    qsw = proj(SEG_QSW, SEG_QSW + 512)
    qraw_ref[...] = (q * scale).astype(BF16)
    qrot_ref[...] = ((q * cos4 + qsw * sin4) * scale).astype(BF16)

    kv = proj(SEG_KV, SEG_KV + 768)
    ksw = proj(SEG_KSW, SEG_KSW + 256)
    for j in range(2):
        for g in range(NSA_GROUPS):
            cmp_ref[2 * j + g] = kv[:, j * 128 + g * DH:j * 128 + (g + 1) * DH].astype(BF16)
    kslc = kv[:, 256:384] * cos + ksw[:, 0:128] * sin
    kwin = kv[:, 512:640] * cos + ksw[:, 128:256] * sin
    vslc = kv[:, 384:512]
    vwin = kv[:, 640:768]
    pos = (i * tm) % seq + lax.broadcasted_iota(jnp.int32, (tm, SEL_HALF), 0)
    blk = (pos // SLC_LEN) % SEL_HALF
    onehot = jnp.where(lax.broadcasted_iota(jnp.int32, (tm, SEL_HALF), 1) == blk, 1.0, 0.0)
    for g in range(NSA_GROUPS):
        sl = slice(g * DH, (g + 1) * DH)
        kaug_ref[g] = jnp.concatenate([kslc[:, sl], onehot], axis=1).astype(BF16)
        vslc_ref[g] = vslc[:, sl].astype(BF16)
        kwin_ref[g] = kwin[:, sl].astype(BF16)
        vwin_ref[g] = vwin[:, sl].astype(BF16)

    gate_ref[...] = _sigmoid(proj(SEG_GATE, SEG_GATE + 256))

    gla = proj(SEG_GLA, SEG_GLA + 1536)
    gq_ref[...] = (gla[:, 0:256] * (GLA_DK ** -0.5)).astype(BF16)
    gk_ref[...] = gla[:, 256:512].astype(BF16)
    gv_ref[...] = gla[:, 512:1024].astype(BF16)
    r = gla[:, 1024:1536]
    gr_ref[...] = (r * _sigmoid(r)).astype(BF16)

    lr = proj(SEG_LR, SEG_LR + 128)
    z = _dot(lr.astype(BF16), w2_ref[...]) + b2_ref[...]
    log_sig = jnp.minimum(z, 0.0) - jnp.log1p(jnp.exp(-jnp.abs(z)))
    ga_ref[...] = log_sig * (1.0 / GLA_TAU)

    mg_ref[...] = _sigmoid(proj(SEG_MERGE, SEG_MERGE + 2048)).astype(BF16)


def _in_proj(x2d, gain, w_all, cos_t, sin_t, w2p, b2, *, seq, tm=256):
    n = x2d.shape[0]
    nt = seq // tm
    row = lambda i: (i, 0)
    grp = lambda i: (0, i, 0)
    const = lambda i: (0, 0)
    tab = lambda i: (i % nt, 0)
    out_shape = (
        jax.ShapeDtypeStruct((n, 512), BF16),
        jax.ShapeDtypeStruct((n, 512), BF16),
        jax.ShapeDtypeStruct((4, n, DH), BF16),
        jax.ShapeDtypeStruct((NSA_GROUPS, n, 128), BF16),
        jax.ShapeDtypeStruct((NSA_GROUPS, n, DH), BF16),
        jax.ShapeDtypeStruct((NSA_GROUPS, n, DH), BF16),
        jax.ShapeDtypeStruct((NSA_GROUPS, n, DH), BF16),
        jax.ShapeDtypeStruct((n, 256), F32),
        jax.ShapeDtypeStruct((n, 256), BF16),
        jax.ShapeDtypeStruct((n, 256), BF16),
        jax.ShapeDtypeStruct((n, 512), BF16),
        jax.ShapeDtypeStruct((n, 256), F32),
        jax.ShapeDtypeStruct((n, 512), BF16),
        jax.ShapeDtypeStruct((n, 2048), BF16),
    )
    out_specs = (
        pl.BlockSpec((tm, 512), row), pl.BlockSpec((tm, 512), row),
        pl.BlockSpec((4, tm, DH), grp),
        pl.BlockSpec((NSA_GROUPS, tm, 128), grp),
        pl.BlockSpec((NSA_GROUPS, tm, DH), grp),
        pl.BlockSpec((NSA_GROUPS, tm, DH), grp),
        pl.BlockSpec((NSA_GROUPS, tm, DH), grp),
        pl.BlockSpec((tm, 256), row),
        pl.BlockSpec((tm, 256), row), pl.BlockSpec((tm, 256), row), pl.BlockSpec((tm, 512), row),
        pl.BlockSpec((tm, 256), row), pl.BlockSpec((tm, 512), row),
        pl.BlockSpec((tm, 2048), row),
    )
    in_specs = [
        pl.BlockSpec((tm, D_MODEL), row),
        pl.BlockSpec((1, D_MODEL), const),
        pl.BlockSpec((D_MODEL, SEG_END), const),
        pl.BlockSpec((tm, LANE), tab),
        pl.BlockSpec((tm, LANE), tab),
        pl.BlockSpec((LANE, 256), const),
        pl.BlockSpec((1, 256), const),
    ]
    return pl.pallas_call(
        functools.partial(_in_proj_kernel, tm=tm, seq=seq),
        grid=(n // tm,),
        in_specs=in_specs, out_specs=out_specs, out_shape=out_shape,
        compiler_params=pltpu.CompilerParams(
            dimension_semantics=("parallel",), vmem_limit_bytes=VMEM_LIMIT),
        name="in_proj",
    )(x2d, gain, w_all, cos_t, sin_t, w2p, b2)


def _compress_kernel(x_ref, w1_ref, w2_ref, pe_ref, o_ref, *, n_sub):
    half = CMP_STRIDE * DH
    x = x_ref[...]
    u = _dot(x, w1_ref[0:half, :])
    v = _dot(x, w1_ref[half:2 * half, :])
    c = _dot(pe_ref[...], w1_ref[...])[0:1, :]
    hid = u + pltpu.roll(v, shift=n_sub - 1, axis=0) + c
    hid = hid * _sigmoid(hid)
    out = _dot(hid.astype(BF16), w2_ref[...])
    rowi = lax.broadcasted_iota(jnp.int32, out.shape, 0)
    o_ref[...] = jnp.where(rowi < n_sub - 1, out, 0.0).astype(BF16)


def _compress(src, w1, w2, pe, *, batch, n_sub):
    return pl.pallas_call(
        functools.partial(_compress_kernel, n_sub=n_sub),
        grid=(4, batch),
        in_specs=[
            pl.BlockSpec((None, None, n_sub, CMP_STRIDE * DH), lambda j, b: (j, b, 0, 0)),
            pl.BlockSpec((None, CMP_LEN * DH, CMP_HIDDEN), lambda j, b: (j // 2, 0, 0)),
            pl.BlockSpec((None, CMP_HIDDEN, DH), lambda j, b: (j // 2, 0, 0)),
            pl.BlockSpec((None, 8, CMP_LEN * DH), lambda j, b: (j // 2, 0, 0)),
        ],
        out_specs=pl.BlockSpec((None, None, n_sub, DH), lambda j, b: (j, b, 0, 0)),
        out_shape=jax.ShapeDtypeStruct((4, batch, n_sub, DH), BF16),
        compiler_params=pltpu.CompilerParams(
            dimension_semantics=("parallel", "parallel"), vmem_limit_bytes=VMEM_LIMIT),
        name="compress",
    )(src, w1, w2, pe)


def _nsa_kernel(qraw_ref, qrot_ref, kc_ref, vc_ref, ovt_ref, kaug_ref, vs_ref, kw_ref, vw_ref,
                gate_ref, o_ref, sc_ref, qa_ref, m_ref, l_ref, acc_ref, oc_ref, os_ref,
                *, tq, tk, tw, seq, topk):
    i = pl.program_id(2)
    s0 = i * tq
    rows = HPG * tq
    n_cmp = kc_ref.shape[0]
    n_sel = ovt_ref.shape[0]

    kc = kc_ref[...]
    vc = vc_ref[...]
    cend = lax.broadcasted_iota(jnp.int32, (tq, n_cmp), 1) * CMP_STRIDE + (CMP_LEN - 1)
    tpos_c = s0 + lax.broadcasted_iota(jnp.int32, (tq, n_cmp), 0)
    cmask = cend <= tpos_c
    psum = jnp.zeros((tq, n_cmp), F32)
    for h in range(HPG):
        s = _dot_nt(qraw_ref[:, h * DH:(h + 1) * DH], kc)
        s = jnp.where(cmask, s, NEG)
        m = jnp.max(s, axis=1, keepdims=True)
        e = jnp.where(cmask, jnp.exp(s - m), 0.0)
        l = jnp.sum(e, axis=1, keepdims=True)
        p = e * jnp.where(l > 0.0, 1.0 / l, 0.0)
        oc_ref[h] = _dot(p.astype(BF16), vc)
        psum = psum + p

    ovt = ovt_ref[...]
    p_hi = psum.astype(BF16)
    p_lo = (psum - p_hi.astype(F32)).astype(BF16)
    imp_t = _dot_nt(ovt, p_hi) + _dot_nt(ovt, p_lo)
    jrow = lax.broadcasted_iota(jnp.int32, (n_sel, tq), 0)
    tlane = s0 + lax.broadcasted_iota(jnp.int32, (n_sel, tq), 1)
    cur = tlane // SLC_LEN
    valid = jrow * SLC_LEN <= tlane
    forced = (jrow == 0) | (jrow == cur) | (jrow == cur - 1)
    score = jnp.where(valid, jnp.where(forced, FORCE_SCORE, imp_t), -jnp.inf)
    sc_ref[...] = score

    def rank_body(k, cnt):
        rowk = sc_ref[pl.ds(k, 1), :]
        ahead = (rowk > score) | ((rowk == score) & (k < jrow))
        return cnt + jnp.where(ahead, 1.0, 0.0)

    cnt = lax.fori_loop(0, n_sel, rank_body, jnp.zeros((n_sel, tq), F32))
    sel_t = (cnt < float(topk)) & valid
    bias = jnp.where(sel_t, 0.0, NEG).T

    n_half = n_sel // SEL_HALF
    for a in range(n_half):
        bh = bias[:, a * SEL_HALF:(a + 1) * SEL_HALF].astype(BF16)
        for h in range(HPG):
            qa_ref[a, h * tq:(h + 1) * tq, :] = jnp.concatenate(
                [qrot_ref[:, h * DH:(h + 1) * DH], bh], axis=1)

    def flash_init():
        m_ref[...] = jnp.full(m_ref.shape, NEG, F32)
        l_ref[...] = jnp.zeros(l_ref.shape, F32)
        acc_ref[...] = jnp.zeros(acc_ref.shape, F32)

    def flash_step(s, v):
        width = s.shape[1]
        m_prev = m_ref[...]
        m_new = jnp.maximum(m_prev, jnp.max(s, axis=1, keepdims=True))
        alpha = jnp.exp(m_prev - m_new)
        m_b = m_new if width == LANE else jnp.concatenate([m_new] * (width // LANE), axis=1)
        p = jnp.exp(s - m_b)
        l_ref[...] = alpha * l_ref[...] + jnp.sum(p, axis=1, keepdims=True)
        acc_ref[...] = alpha[:, 0:DH] * acc_ref[...] + _dot(p.astype(BF16), v)
        m_ref[...] = m_new

    def flash_out(dst_ref):
        dst_ref[...] = acc_ref[...] / l_ref[:, 0:DH]

    tiles_per_half = SEL_HALF * SLC_LEN // tk
    n_tiles = (s0 + tq + tk - 1) // tk
    last = n_tiles - 1

    def sel_tile(tile, qaug, causal):
        start = pl.multiple_of(tile * tk, tk)
        s = _dot_nt(qaug, kaug_ref[pl.ds(start, tk), :])
        if causal:
            kpos = start + lax.broadcasted_iota(jnp.int32, (rows, tk), 1)
            tpos = s0 + (lax.broadcasted_iota(jnp.int32, (rows, tk), 0) & (tq - 1))
            s = jnp.where(kpos <= tpos, s, NEG)
        flash_step(s, vs_ref[pl.ds(start, tk), :])

    flash_init()
    for a in range(n_half):
        lo = a * tiles_per_half
        hi = jnp.minimum(last, (a + 1) * tiles_per_half)

        def body(tile, carry, a=a):
            sel_tile(tile, qa_ref[a], False)
            return carry

        lax.fori_loop(lo, hi, body, 0)
    qa_last = qa_ref[0]
    for a in range(1, n_half):
        qa_last = jnp.where(last >= a * tiles_per_half, qa_ref[a], qa_last)
    sel_tile(last, qa_last, True)
    flash_out(os_ref)

    flash_init()
    q_all = qa_ref[0, :, 0:DH]
    w_lo = jnp.maximum(s0 - WINDOW, 0) // tw
    w_hi = (s0 + tq) // tw

    def win_body(j, carry):
        start = pl.multiple_of(j * tw, tw)
        s = _dot_nt(q_all, kw_ref[pl.ds(start, tw), :])
        kpos = start + lax.broadcasted_iota(jnp.int32, (rows, tw), 1)
        tpos = s0 + (lax.broadcasted_iota(jnp.int32, (rows, tw), 0) & (tq - 1))
        dist = tpos - kpos
        s = jnp.where((dist >= 0) & (dist < WINDOW), s, NEG)
        flash_step(s, vw_ref[pl.ds(start, tw), :])
        return carry

    lax.fori_loop(w_lo, w_hi, win_body, 0)

    gt = gate_ref[...]
    inv_l = 1.0 / l_ref[:, 0:DH]
    outs = []
    for h in range(HPG):
        sl = slice(h * tq, (h + 1) * tq)
        o_w = acc_ref[sl, :] * inv_l[sl, :]
        o_h = (gt[:, h:h + 1] * oc_ref[h] + gt[:, HPG + h:HPG + h + 1] * os_ref[sl, :]
               + gt[:, 2 * HPG + h:2 * HPG + h + 1] * o_w)
        outs.append(o_h)
    o_ref[...] = jnp.concatenate(outs, axis=1).astype(BF16)


def _nsa(qraw, qrot, cmp_kv, ovt, kaug, vslc, kwin, vwin, gates, *, batch, seq, tq=128, tk=256, tw=128):
    n = batch * seq
    nq = seq // tq
    n_cmp = cmp_kv.shape[2]
    n_sel = ovt.shape[0]
    topk = min(SLC_TOPK, seq // SLC_LEN)
    rows = HPG * tq
    qmap = lambda b, g, i: (b * nq + i, g)
    kvmap = lambda b, g, i: (g, b, 0)
    return pl.pallas_call(
        functools.partial(_nsa_kernel, tq=tq, tk=tk, tw=tw, seq=seq, topk=topk),
        grid=(batch, NSA_GROUPS, nq),
        in_specs=[
            pl.BlockSpec((tq, HPG * DH), qmap),
            pl.BlockSpec((tq, HPG * DH), qmap),
            pl.BlockSpec((None, None, n_cmp, DH), lambda b, g, i: (g, b, 0, 0)),
            pl.BlockSpec((None, None, n_cmp, DH), lambda b, g, i: (NSA_GROUPS + g, b, 0, 0)),
            pl.BlockSpec((n_sel, n_cmp), lambda b, g, i: (0, 0)),
            pl.BlockSpec((None, seq, 128), kvmap),
            pl.BlockSpec((None, seq, DH), kvmap),
            pl.BlockSpec((None, seq, DH), kvmap),
            pl.BlockSpec((None, seq, DH), kvmap),
            pl.BlockSpec((tq, LANE), qmap),
        ],
        out_specs=pl.BlockSpec((tq, HPG * DH), qmap),
        out_shape=jax.ShapeDtypeStruct((n, NSA_HEADS * DH), BF16),
        scratch_shapes=[
            pltpu.VMEM((n_sel, tq), F32),
            pltpu.VMEM((n_sel // SEL_HALF, rows, 128), BF16),
            pltpu.VMEM((rows, LANE), F32),
            pltpu.VMEM((rows, LANE), F32),
            pltpu.VMEM((rows, DH), F32),
            pltpu.VMEM((HPG, tq, DH), F32),
            pltpu.VMEM((rows, DH), F32),
        ],
        compiler_params=pltpu.CompilerParams(
            dimension_semantics=("parallel", "parallel", "arbitrary"),
            vmem_limit_bytes=VMEM_LIMIT),
        name="nsa",
    )(qraw, qrot, cmp_kv, cmp_kv, ovt, kaug, vslc, kwin, vwin, gates)


def _gla_kernel(q_ref, k_ref, v_ref, g_ref, r_ref, gn_ref, o_ref, st_ref, *, chunk):
    c = pl.program_id(1)

    @pl.when(c == 0)
    def _():
        st_ref[...] = jnp.zeros(st_ref.shape, F32)

    tril = jnp.where(lax.broadcasted_iota(jnp.int32, (chunk, chunk), 0)
                     >= lax.broadcasted_iota(jnp.int32, (chunk, chunk), 1), 1.0, 0.0).astype(BF16)
    gn = gn_ref[...]
    outs = []
    for h in range(GLA_HEADS):
        q = q_ref[:, h * GLA_DK:(h + 1) * GLA_DK].astype(F32)
        k = k_ref[:, h * GLA_DK:(h + 1) * GLA_DK].astype(F32)
        v = v_ref[:, h * GLA_DV:(h + 1) * GLA_DV]
        g = g_ref[:, h * GLA_DK:(h + 1) * GLA_DK]
        g_hi = g.astype(BF16)
        g_lo = (g - g_hi.astype(F32)).astype(BF16)
        cum = _dot(tril, g_hi) + _dot(tril, g_lo)
        last = cum[chunk - 1:chunk, :]
        st = st_ref[h]
        o = _dot_nt((q * jnp.exp(cum)).astype(BF16), st.astype(BF16))
        pieces = []
        for sb in range(chunk // GLA_SUB):
            r0 = sb * GLA_SUB
            n = r0 + GLA_SUB
            g0 = cum[r0:r0 + 1, :]
            qt = (q[r0:n] * jnp.exp(cum[r0:n] - g0)).astype(BF16)
            kt = (k[0:n] * jnp.exp(g0 - cum[0:n])).astype(BF16)
            a = _dot_nt(qt, kt)
            causal = (lax.broadcasted_iota(jnp.int32, (GLA_SUB, n), 1)
                      <= r0 + lax.broadcasted_iota(jnp.int32, (GLA_SUB, n), 0))
            a = jnp.where(causal, a, 0.0)
            pieces.append(_dot(a.astype(BF16), v[0:n]))
        o = o + jnp.concatenate(pieces, axis=0)
        kd = (k * jnp.exp(last - cum)).astype(BF16)
        v_t = v.astype(F32).T.astype(BF16)
        st_ref[h] = st * jnp.exp(last) + _dot(v_t, kd)
        ms = jnp.mean(o * o, axis=-1, keepdims=True)
        y = o * lax.rsqrt(ms + EPS) * gn
        outs.append(y * r_ref[:, h * GLA_DV:(h + 1) * GLA_DV].astype(F32))
    o_ref[...] = jnp.concatenate(outs, axis=1).astype(BF16)


def _gla(gq, gk, gv, ga, gr, gnorm, *, batch, seq, chunk=GLA_CHUNK):
    n = batch * seq
    nc = seq // chunk
    cmap = lambda b, c: (b * nc + c, 0)
    return pl.pallas_call(
        functools.partial(_gla_kernel, chunk=chunk),
        grid=(batch, nc),
        in_specs=[
            pl.BlockSpec((chunk, GLA_HEADS * GLA_DK), cmap),
            pl.BlockSpec((chunk, GLA_HEADS * GLA_DK), cmap),
            pl.BlockSpec((chunk, GLA_HEADS * GLA_DV), cmap),
            pl.BlockSpec((chunk, GLA_HEADS * GLA_DK), cmap),
            pl.BlockSpec((chunk, GLA_HEADS * GLA_DV), cmap),
            pl.BlockSpec((1, GLA_DV), lambda b, c: (0, 0)),
        ],
        out_specs=pl.BlockSpec((chunk, GLA_HEADS * GLA_DV), cmap),
        out_shape=jax.ShapeDtypeStruct((n, GLA_HEADS * GLA_DV), BF16),
        scratch_shapes=[pltpu.VMEM((GLA_HEADS, GLA_DV, GLA_DK), F32)],
        compiler_params=pltpu.CompilerParams(
            dimension_semantics=("parallel", "arbitrary"), vmem_limit_bytes=VMEM_LIMIT),
        name="gla",
    )(gq, gk, gv, ga, gr, gnorm)


def _merge_kernel(x_ref, oa_ref, ob_ref, mg_ref, wn_ref, wg_ref, wo_ref, o_ref):
    up_a = _dot(oa_ref[...], wn_ref[...])
    up_b = _dot(ob_ref[...], wg_ref[...])
    merged = mg_ref[:, 0:D_MODEL].astype(F32) * up_a + mg_ref[:, D_MODEL:2 * D_MODEL].astype(F32) * up_b
    o_ref[...] = x_ref[...] + _dot(merged.astype(BF16), wo_ref[...])


def _merge(x2d, oa, ob, mg, wn, wg, wo, *, tm=512):
    n = x2d.shape[0]
    row = lambda i: (i, 0)
    const = lambda i: (0, 0)
    return pl.pallas_call(
        _merge_kernel,
        grid=(n // tm,),
        in_specs=[
            pl.BlockSpec((tm, D_MODEL), row),
            pl.BlockSpec((tm, 512), row),
            pl.BlockSpec((tm, 512), row),
            pl.BlockSpec((tm, 2 * D_MODEL), row),
            pl.BlockSpec((512, D_MODEL), const),
            pl.BlockSpec((512, D_MODEL), const),
            pl.BlockSpec((D_MODEL, D_MODEL), const),
        ],
        out_specs=pl.BlockSpec((tm, D_MODEL), row),
        out_shape=jax.ShapeDtypeStruct((n, D_MODEL), F32),
        compiler_params=pltpu.CompilerParams(
            dimension_semantics=("parallel",), vmem_limit_bytes=VMEM_LIMIT),
        name="merge_out",
    )(x2d, oa, ob, mg, wn, wg, wo)


def _ffn_kernel(x_ref, g1_ref, wg_ref, wu_ref, wd_ref, g2_ref, o_ref):
    x = x_ref[...]
    ms = jnp.mean(x * x, axis=-1, keepdims=True)
    h = (x * lax.rsqrt(ms + EPS) * g1_ref[...]).astype(BF16)
    a = _dot(h, wg_ref[...])
    u = _dot(h, wu_ref[...])
    act = (a * _sigmoid(a) * u).astype(BF16)
    y = x + _dot(act, wd_ref[...])
    ms2 = jnp.mean(y * y, axis=-1, keepdims=True)
    o_ref[...] = y * lax.rsqrt(ms2 + EPS) * g2_ref[...]


def _ffn(x1, g1, wg, wu, wd, g2, *, tm=256):
    n = x1.shape[0]
    row = lambda i: (i, 0)
    const = lambda i: (0, 0)
    return pl.pallas_call(
        _ffn_kernel,
        grid=(n // tm,),
        in_specs=[
            pl.BlockSpec((tm, D_MODEL), row),
            pl.BlockSpec((1, D_MODEL), const),
            pl.BlockSpec((D_MODEL, D_FF), const),
            pl.BlockSpec((D_MODEL, D_FF), const),
            pl.BlockSpec((D_FF, D_MODEL), const),
            pl.BlockSpec((1, D_MODEL), const),
        ],
        out_specs=pl.BlockSpec((tm, D_MODEL), row),
        out_shape=jax.ShapeDtypeStruct((n, D_MODEL), F32),
        compiler_params=pltpu.CompilerParams(
            dimension_semantics=("parallel",), vmem_limit_bytes=VMEM_LIMIT),
        name="ffn",
    )(x1, g1, wg, wu, wd, g2)


def _rope_swap(w, n_heads):
    w3 = w.reshape(w.shape[0], n_heads, DH)
    half = ROPE_DIM // 2
    sw = jnp.concatenate([-w3[:, :, half:ROPE_DIM], w3[:, :, 0:half],
                          jnp.zeros_like(w3[:, :, ROPE_DIM:])], axis=2)
    return sw.reshape(w.shape)


def _rope_tables(seq):
    half = ROPE_DIM // 2
    inv_freq = ROPE_THETA ** (-jnp.arange(half, dtype=F32) / half)
    ang = jnp.arange(seq).astype(F32)[:, None] * inv_freq[None, :]
    cos = jnp.cos(ang)
    sin = jnp.sin(ang)
    ones = jnp.ones((seq, DH - ROPE_DIM), F32)
    cos64 = jnp.concatenate([cos, cos, ones], axis=1)
    sin64 = jnp.concatenate([sin, sin, 0.0 * ones], axis=1)
    return jnp.concatenate([cos64, cos64], axis=1), jnp.concatenate([sin64, sin64], axis=1)


def _overlap_t(n_sel, n_cmp_pad, n_slc, n_cmp):
    c_start = np.arange(n_cmp_pad)[None, :] * CMP_STRIDE
    s_start = np.arange(n_sel)[:, None] * SLC_LEN
    ov = (c_start < s_start + SLC_LEN) & (c_start + CMP_LEN > s_start)
    ov &= (np.arange(n_cmp_pad)[None, :] < n_cmp) & (np.arange(n_sel)[:, None] < n_slc)
    return jnp.asarray(ov, dtype=BF16)


def _pack_in_weights(w_in):
    c = 0
    wq = w_in[:, c:c + 512]; c += 512
    wkv = w_in[:, c:c + 768]; c += 768
    wgate = w_in[:, c:c + 24]; c += 24
    wgq = w_in[:, c:c + 256]; c += 256
    wgk = w_in[:, c:c + 256]; c += 256
    wgv = w_in[:, c:c + 512]; c += 512
    wlr = w_in[:, c:c + GLA_RANK]; c += GLA_RANK
    wgr = w_in[:, c:c + 512]; c += 512
    wmg = w_in[:, c:c + 2048]
    d = w_in.shape[0]
    wk_slc = wkv[:, 256:384]
    wk_win = wkv[:, 512:640]
    wg3 = wgate.reshape(d, NSA_GROUPS, HPG, 3).transpose(0, 1, 3, 2).reshape(d, NSA_GROUPS, 3 * HPG)
    wg_pad = jnp.pad(wg3, ((0, 0), (0, 0), (0, LANE - 3 * HPG))).reshape(d, NSA_GROUPS * LANE)
    wlr_pad = jnp.pad(wlr, ((0, 0), (0, LANE - GLA_RANK)))
    w_all = jnp.concatenate([
        wq, _rope_swap(wq, NSA_HEADS), wkv,
        _rope_swap(wk_slc, NSA_GROUPS), _rope_swap(wk_win, NSA_GROUPS),
        wg_pad, wgq, wgk, wgv, wgr, wlr_pad, wmg], axis=1)
    return w_all.astype(BF16)


def kernel(x, norm_mix, w_in, cmp_pe_k, cmp_pe_v, cmp_k_w1, cmp_k_w2, cmp_v_w1, cmp_v_w2,
           gla_gate_w2, gla_gate_b, gla_norm, w_up_nsa, w_up_gla, w_out, norm_ffn,
           w_ffn_gate, w_ffn_up, w_ffn_down, norm_final):
    batch, seq, d = x.shape
    assert d == D_MODEL and w_in.shape[0] == 1
    assert seq % 256 == 0
    n = batch * seq
    x2d = x.reshape(n, d)

    w_all = _pack_in_weights(w_in[0])
    assert w_all.shape[1] == SEG_END
    cos_t, sin_t = _rope_tables(seq)
    w2p = jnp.pad(gla_gate_w2[0], ((0, LANE - GLA_RANK), (0, 0))).astype(BF16)
    b2 = gla_gate_b[0].reshape(1, -1)

    (qraw, qrot, cmp_src, kaug, vslc, kwin, vwin, gates, gq, gk, gv, ga, gr, mg) = _in_proj(
        x2d, norm_mix[0].reshape(1, d), w_all, cos_t, sin_t, w2p, b2, seq=seq)

    n_sub = seq // CMP_STRIDE
    n_cmp = n_sub - CMP_LEN // CMP_STRIDE + 1
    n_slc = seq // SLC_LEN
    cmp_in = cmp_src.reshape(4, batch, n_sub, CMP_STRIDE * DH)
    w1 = jnp.stack([cmp_k_w1[0], cmp_v_w1[0]]).astype(BF16)
    w2 = jnp.stack([cmp_k_w2[0], cmp_v_w2[0]]).astype(BF16)
    pe = jnp.stack([cmp_pe_k[0].reshape(1, -1), cmp_pe_v[0].reshape(1, -1)])
    pe = jnp.broadcast_to(pe, (2, 8, CMP_LEN * DH)).astype(BF16)
    cmp_kv = _compress(cmp_in, w1, w2, pe, batch=batch, n_sub=n_sub)

    n_sel = -(-n_slc // SEL_HALF) * SEL_HALF
    ovt = _overlap_t(n_sel, n_sub, n_slc, n_cmp)
    o_a = _nsa(qraw, qrot, cmp_kv, ovt, kaug, vslc, kwin, vwin, gates, batch=batch, seq=seq)

    o_b = _gla(gq, gk, gv, ga, gr, gla_norm[0].reshape(1, -1), batch=batch, seq=seq)

    x1 = _merge(x2d, o_a, o_b, mg, w_up_nsa[0].astype(BF16), w_up_gla[0].astype(BF16),
                w_out[0].astype(BF16))
    out = _ffn(x1, norm_ffn[0].reshape(1, d), w_ffn_gate[0].astype(BF16), w_ffn_up[0].astype(BF16),
               w_ffn_down[0].astype(BF16), norm_final.reshape(1, d))
    return out.reshape(batch, seq, d)
```

```python
import functools

import numpy as np
import jax
import jax.numpy as jnp
from jax import lax
from jax.experimental import pallas as pl
from jax.experimental.pallas import tpu as pltpu

F32 = jnp.float32
BF16 = jnp.bfloat16

D_MODEL = 1024
NSA_HEADS = 8
NSA_GROUPS = 2
HPG = NSA_HEADS // NSA_GROUPS
DH = 64
CMP_LEN = 32
CMP_STRIDE = 16
CMP_HIDDEN = 128
SLC_LEN = 64
SLC_TOPK = 16
WINDOW = 512
FORCE_SCORE = 1.0e4
GLA_HEADS = 4
GLA_DK = 64
GLA_DV = 128
GLA_RANK = 16
GLA_TAU = 16.0
GLA_CHUNK = 64
GLA_SUB = 16
ROPE_THETA = 500000.0
ROPE_DIM = DH // 4
D_FF = 2816
EPS = 1e-6
NEG = -1e30
LOG2E = 1.4426950408889634
KEY_TILE = 256

LANE = 128
VMEM_LIMIT = 56 * 1024 * 1024
SEL_HALF = 64

SEG_Q = 0
SEG_QSW = SEG_Q + 512
SEG_KV = SEG_QSW + 512
SEG_KSW = SEG_KV + 768
SEG_GATE = SEG_KSW + 256
SEG_GLA = SEG_GATE + 256
SEG_LR = SEG_GLA + 1536
SEG_MERGE = SEG_LR + 128
SEG_END = SEG_MERGE + 2048


def _dot(a, b):
    return jnp.dot(a, b, preferred_element_type=F32)


def _dot_nt(a, b):
    return lax.dot_general(a, b, (((1,), (1,)), ((), ())), preferred_element_type=F32)


def _sigmoid(x):
    return 1.0 / (1.0 + jnp.exp(-x))


def _in_proj_kernel(x_ref, gain_ref, w_ref, cos_ref, sin_ref, w2_ref, b2_ref,
                    qraw_ref, qrot_ref, cmp_ref, kaug_ref, vslc_ref, kwin_ref, vwin_ref,
                    gate_ref, gq_ref, gk_ref, gv_ref, ga_ref, gr_ref, mg_ref, *, tm, seq):
    i = pl.program_id(0)
    x = x_ref[...]
    ms = jnp.mean(x * x, axis=-1, keepdims=True)
    h = (x * lax.rsqrt(ms + EPS) * gain_ref[...]).astype(BF16)

    def proj(a, b):
        return _dot(h, w_ref[:, a:b])

    cos = cos_ref[...]
    sin = sin_ref[...]
    cos4 = jnp.concatenate([cos] * 4, axis=1)
    sin4 = jnp.concatenate([sin] * 4, axis=1)

    scale = DH ** -0.5
    q = proj(SEG_Q, SEG_Q + 512)
    qsw = proj(SEG_QSW, SEG_QSW + 512)
    qraw_t = (q * scale).T
    qrot_t = ((q * cos4 + qsw * sin4) * (scale * LOG2E)).T
    gw = HPG * DH
    for g in range(NSA_GROUPS):
        qraw_ref[g, 0] = qraw_t[g * gw:(g + 1) * gw, :].astype(BF16)
        qrot_ref[g, 0] = qrot_t[g * gw:(g + 1) * gw, :].astype(BF16)

    kv = proj(SEG_KV, SEG_KV + 768)
    ksw = proj(SEG_KSW, SEG_KSW + 256)
    for j in range(2):
        for g in range(NSA_GROUPS):
            cmp_ref[2 * j + g] = kv[:, j * 128 + g * DH:j * 128 + (g + 1) * DH].astype(BF16)
    kslc = kv[:, 256:384] * cos + ksw[:, 0:128] * sin
    kwin = kv[:, 512:640] * cos + ksw[:, 128:256] * sin
    vslc_t = kv[:, 384:512].T
    vwin_t = kv[:, 640:768].T
    pos = (i * tm) % seq + lax.broadcasted_iota(jnp.int32, (tm, SEL_HALF), 0)
    blk = (pos // SLC_LEN) % SEL_HALF
    onehot = jnp.where(lax.broadcasted_iota(jnp.int32, (tm, SEL_HALF), 1) == blk, 1.0, 0.0)
    for g in range(NSA_GROUPS):
        sl = slice(g * DH, (g + 1) * DH)
        kaug_ref[g] = jnp.concatenate([kslc[:, sl], onehot], axis=1).astype(BF16)
        vslc_ref[g, 0] = vslc_t[sl, :].astype(BF16)
        kwin_ref[g] = kwin[:, sl].astype(BF16)
        vwin_ref[g, 0] = vwin_t[sl, :].astype(BF16)

    gate_ref[...] = _sigmoid(proj(SEG_GATE, SEG_GATE + 256))

    gla = proj(SEG_GLA, SEG_GLA + 1536)
    gq_ref[...] = (gla[:, 0:256] * (GLA_DK ** -0.5)).astype(BF16)
    gk_ref[...] = gla[:, 256:512].astype(BF16)
    gv_ref[...] = gla[:, 512:1024].astype(BF16)
    r = gla[:, 1024:1536]
    gr_ref[...] = (r * _sigmoid(r)).astype(BF16)

    lr = proj(SEG_LR, SEG_LR + 128)
    z = _dot(lr.astype(BF16), w2_ref[...]) + b2_ref[...]
    log_sig = jnp.minimum(z, 0.0) - jnp.log1p(jnp.exp(-jnp.abs(z)))
    ga_ref[...] = log_sig * (1.0 / GLA_TAU)

    mg_ref[...] = _sigmoid(proj(SEG_MERGE, SEG_MERGE + 2048)).astype(BF16)


def _in_proj(x2d, gain, w_all, cos_t, sin_t, w2p, b2, *, seq, tm=KEY_TILE):
    n = x2d.shape[0]
    nt = seq // tm
    row = lambda i: (i, 0)
    grp = lambda i: (0, i, 0)
    const = lambda i: (0, 0)
    tab = lambda i: (i % nt, 0)
    out_shape = (
        jax.ShapeDtypeStruct((NSA_GROUPS, n // tm, HPG * DH, tm), BF16),
        jax.ShapeDtypeStruct((NSA_GROUPS, n // tm, HPG * DH, tm), BF16),
        jax.ShapeDtypeStruct((4, n, DH), BF16),
        jax.ShapeDtypeStruct((NSA_GROUPS, n, 128), BF16),
        jax.ShapeDtypeStruct((NSA_GROUPS, n // tm, DH, tm), BF16),
        jax.ShapeDtypeStruct((NSA_GROUPS, n, DH), BF16),
        jax.ShapeDtypeStruct((NSA_GROUPS, n // tm, DH, tm), BF16),
        jax.ShapeDtypeStruct((n, 256), F32),
        jax.ShapeDtypeStruct((n, 256), BF16),
        jax.ShapeDtypeStruct((n, 256), BF16),
        jax.ShapeDtypeStruct((n, 512), BF16),
        jax.ShapeDtypeStruct((n, 256), F32),
        jax.ShapeDtypeStruct((n, 512), BF16),
        jax.ShapeDtypeStruct((n, 2048), BF16),
    )
    out_specs = (
        pl.BlockSpec((NSA_GROUPS, 1, HPG * DH, tm), lambda i: (0, i, 0, 0)),
        pl.BlockSpec((NSA_GROUPS, 1, HPG * DH, tm), lambda i: (0, i, 0, 0)),
        pl.BlockSpec((4, tm, DH), grp),
        pl.BlockSpec((NSA_GROUPS, tm, 128), grp),
        pl.BlockSpec((NSA_GROUPS, 1, DH, tm), lambda i: (0, i, 0, 0)),
        pl.BlockSpec((NSA_GROUPS, tm, DH), grp),
        pl.BlockSpec((NSA_GROUPS, 1, DH, tm), lambda i: (0, i, 0, 0)),
        pl.BlockSpec((tm, 256), row),
        pl.BlockSpec((tm, 256), row), pl.BlockSpec((tm, 256), row), pl.BlockSpec((tm, 512), row),
        pl.BlockSpec((tm, 256), row), pl.BlockSpec((tm, 512), row),
        pl.BlockSpec((tm, 2048), row),
    )
    in_specs = [
        pl.BlockSpec((tm, D_MODEL), row),
        pl.BlockSpec((1, D_MODEL), const),
        pl.BlockSpec((D_MODEL, SEG_END), const),
        pl.BlockSpec((tm, LANE), tab),
        pl.BlockSpec((tm, LANE), tab),
        pl.BlockSpec((LANE, 256), const),
        pl.BlockSpec((1, 256), const),
    ]
    return pl.pallas_call(
        functools.partial(_in_proj_kernel, tm=tm, seq=seq),
        grid=(n // tm,),
        in_specs=in_specs, out_specs=out_specs, out_shape=out_shape,
        compiler_params=pltpu.CompilerParams(
            dimension_semantics=("parallel",), vmem_limit_bytes=VMEM_LIMIT),
        name="in_proj",
    )(x2d, gain, w_all, cos_t, sin_t, w2p, b2)


def _compress_kernel(x_ref, w1_ref, w2_ref, w2t_ref, pe_ref, o_ref, ot_ref, *, n_sub):
    half = CMP_STRIDE * DH
    x = x_ref[...]
    u = _dot(x, w1_ref[0:half, :])
    v = _dot(x, w1_ref[half:2 * half, :])
    c = _dot(pe_ref[...], w1_ref[...])[0:1, :]
    hid = u + pltpu.roll(v, shift=n_sub - 1, axis=0) + c
    hid = (hid * _sigmoid(hid)).astype(BF16)
    out = _dot(hid, w2_ref[...])
    rowi = lax.broadcasted_iota(jnp.int32, out.shape, 0)
    o_ref[...] = jnp.where(rowi < n_sub - 1, out, 0.0).astype(BF16)
    out_t = _dot_nt(w2t_ref[...], hid)
    coli = lax.broadcasted_iota(jnp.int32, out_t.shape, 1)
    ot_ref[...] = jnp.where(coli < n_sub - 1, out_t, 0.0).astype(BF16)


def _compress(src, w1, w2, w2t, pe, *, batch, n_sub):
    return pl.pallas_call(
        functools.partial(_compress_kernel, n_sub=n_sub),
        grid=(4, batch),
        in_specs=[
            pl.BlockSpec((None, None, n_sub, CMP_STRIDE * DH), lambda j, b: (j, b, 0, 0)),
            pl.BlockSpec((None, CMP_LEN * DH, CMP_HIDDEN), lambda j, b: (j // 2, 0, 0)),
            pl.BlockSpec((None, CMP_HIDDEN, DH), lambda j, b: (j // 2, 0, 0)),
            pl.BlockSpec((None, DH, CMP_HIDDEN), lambda j, b: (j // 2, 0, 0)),
            pl.BlockSpec((None, 8, CMP_LEN * DH), lambda j, b: (j // 2, 0, 0)),
        ],
        out_specs=(pl.BlockSpec((None, None, n_sub, DH), lambda j, b: (j, b, 0, 0)),
                   pl.BlockSpec((None, None, DH, n_sub), lambda j, b: (j, b, 0, 0))),
        out_shape=(jax.ShapeDtypeStruct((4, batch, n_sub, DH), BF16),
                   jax.ShapeDtypeStruct((4, batch, DH, n_sub), BF16)),
        compiler_params=pltpu.CompilerParams(
            dimension_semantics=("parallel", "parallel"), vmem_limit_bytes=VMEM_LIMIT),
        name="compress",
    )(src, w1, w2, w2t, pe)


def _softmax_tiles(scores, vt, carry):
    stats = []
    for s, (m, l, _) in zip(scores, carry):
        m_new = jnp.maximum(m, jnp.max(s, axis=0, keepdims=True))
        alpha = jnp.exp2(m - m_new)
        p = jnp.exp2(s - m_new)
        stats.append((m_new, alpha, alpha * l + jnp.sum(p, axis=0, keepdims=True), p.astype(BF16)))
    return tuple((m_new, l_new, alpha * acc + _dot(vt, p))
                 for (m_new, alpha, l_new, p), (_, _, acc) in zip(stats, carry))


def _nsa_kernel(qraw_ref, qrot_ref, kc_ref, vct_ref, ovt_ref, kaug_ref, vst_ref, kw_ref, vwt_ref,
                gate_ref, o_ref, sc_ref, qa_ref, *, tq, topk):
    i = pl.program_id(2)
    s0 = i * tq
    n_cmp = kc_ref.shape[0]
    n_sel = ovt_ref.shape[0]

    kc = kc_ref[...]
    vct = vct_ref[...]
    cend = lax.broadcasted_iota(jnp.int32, (n_cmp, tq), 0) * CMP_STRIDE + (CMP_LEN - 1)
    cmask = cend <= s0 + lax.broadcasted_iota(jnp.int32, (n_cmp, tq), 1)
    psum = jnp.zeros((n_cmp, tq), F32)
    oc_t = []
    for h in range(HPG):
        s = _dot(kc, qraw_ref[h * DH:(h + 1) * DH, :])
        s = jnp.where(cmask, s, NEG)
        m = jnp.max(s, axis=0, keepdims=True)
        e = jnp.where(cmask, jnp.exp(s - m), 0.0)
        l = jnp.sum(e, axis=0, keepdims=True)
        p = e * jnp.where(l > 0.0, 1.0 / l, 0.0)
        oc_t.append(_dot(vct, p.astype(BF16)))
        psum = psum + p

    ovt = ovt_ref[...]
    p_hi = psum.astype(BF16)
    p_lo = (psum - p_hi.astype(F32)).astype(BF16)
    imp_t = _dot(ovt, p_hi) + _dot(ovt, p_lo)
    jrow = lax.broadcasted_iota(jnp.int32, (n_sel, tq), 0)
    tlane = s0 + lax.broadcasted_iota(jnp.int32, (n_sel, tq), 1)
    cur = tlane // SLC_LEN
    valid = jrow * SLC_LEN <= tlane
    forced = (jrow == 0) | (jrow == cur) | (jrow == cur - 1)
    score = jnp.where(valid, jnp.where(forced, FORCE_SCORE, imp_t), -jnp.inf)
    sc_ref[...] = score

    sub = 8
    n_grp = n_sel // sub
    jsub = lax.broadcasted_iota(jnp.int32, (sub, LANE), 0)
    cnt_parts = []
    for c0 in range(0, tq, LANE):
        blocks = [score[v * sub:(v + 1) * sub, c0:c0 + LANE] for v in range(n_grp)]
        cnts = [jnp.zeros((sub, LANE), F32) for _ in range(n_grp)]
        for kg in range(n_grp):
            for r in range(sub):
                rowk = sc_ref[kg * sub + r:kg * sub + r + 1, c0:c0 + LANE]
                for v in range(n_grp):
                    if v > kg:
                        ahead = rowk >= blocks[v]
                    elif v < kg:
                        ahead = rowk > blocks[v]
                    else:
                        ahead = (rowk > blocks[v]) | ((rowk == blocks[v]) & (r < jsub))
                    cnts[v] = cnts[v] + jnp.where(ahead, 1.0, 0.0)
        cnt_parts.append(jnp.concatenate(cnts, axis=0))
    cnt = jnp.concatenate(cnt_parts, axis=1)
    bias_t = jnp.where((cnt < float(topk)) & valid, 0.0, NEG).astype(BF16)

    n_half = n_sel // SEL_HALF
    for a in range(n_half):
        for h in range(HPG):
            qa_ref[a, h, 0:DH, :] = qrot_ref[h * DH:(h + 1) * DH, :]
            qa_ref[a, h, DH:DH + SEL_HALF, :] = bias_t[a * SEL_HALF:(a + 1) * SEL_HALF, :]

    def init_carry():
        one = (jnp.full((1, tq), NEG, F32), jnp.zeros((1, tq), F32), jnp.zeros((DH, tq), F32))
        return tuple(one for _ in range(HPG))

    key_i = lax.broadcasted_iota(jnp.int32, (tq, tq), 0)
    qry_i = lax.broadcasted_iota(jnp.int32, (tq, tq), 1)

    tiles_per_half = SEL_HALF * SLC_LEN // tq

    def sel_tile(tile, carry, half, causal):
        start = pl.multiple_of(tile * tq, tq)
        kt = kaug_ref[pl.ds(start, tq), :]
        scores = []
        for h in range(HPG):
            if isinstance(half, int):
                qa = qa_ref[half, h]
            else:
                qa = qa_ref[0, h]
                for a in range(1, n_half):
                    qa = jnp.where(half >= a, qa_ref[a, h], qa)
            s = _dot(kt, qa)
            scores.append(jnp.where(key_i <= qry_i, s, NEG) if causal else s)
        return _softmax_tiles(scores, vst_ref[tile], carry)

    carry = init_carry()
    for a in range(n_half):
        carry = lax.fori_loop(a * tiles_per_half, jnp.minimum(i, (a + 1) * tiles_per_half),
                              functools.partial(sel_tile, half=a, causal=False), carry)
    carry = sel_tile(i, carry, i // tiles_per_half, True)
    os_t = [acc / l for (_, l, acc) in carry]

    def win_tile(tile, carry):
        start = pl.multiple_of(tile * tq, tq)
        kt = kw_ref[pl.ds(start, tq), :]
        dist = (s0 - start) + qry_i - key_i
        band = (dist >= 0) & (dist < WINDOW)
        scores = [jnp.where(band, _dot(kt, qa_ref[0, h, 0:DH, :]), NEG) for h in range(HPG)]
        return _softmax_tiles(scores, vwt_ref[tile], carry)

    carry = lax.fori_loop(jnp.maximum(s0 - WINDOW, 0) // tq, i + 1, win_tile, init_carry())
    ow_t = [acc / l for (_, l, acc) in carry]

    gt_t = gate_ref[...].T
    outs = [gt_t[h:h + 1, :] * oc_t[h] + gt_t[HPG + h:HPG + h + 1, :] * os_t[h]
            + gt_t[2 * HPG + h:2 * HPG + h + 1, :] * ow_t[h] for h in range(HPG)]
    o_ref[...] = jnp.concatenate(outs, axis=0).T.astype(BF16)


def _nsa(qraw_t, qrot_t, cmp_kv, cmp_kv_t, ovt, kaug, vslc_t, kwin, vwin_t, gates, *, batch, seq,
         tq=KEY_TILE):
    n = batch * seq
    nq = seq // tq
    nk = nq
    tk = tq
    n_cmp = cmp_kv.shape[2]
    n_sel = ovt.shape[0]
    topk = min(SLC_TOPK, seq // SLC_LEN)
    qmap = lambda b, g, i: (b * nq + i, g)
    qtmap = lambda b, g, i: (g, b * nq + i, 0, 0)
    kvmap = lambda b, g, i: (g, b, 0)
    vtmap = lambda b, g, i: (g, b, 0, 0)
    return pl.pallas_call(
        functools.partial(_nsa_kernel, tq=tq, topk=topk),
        grid=(batch, NSA_GROUPS, nq),
        in_specs=[
            pl.BlockSpec((None, None, HPG * DH, tq), qtmap),
            pl.BlockSpec((None, None, HPG * DH, tq), qtmap),
            pl.BlockSpec((None, None, n_cmp, DH), lambda b, g, i: (g, b, 0, 0)),
            pl.BlockSpec((None, None, DH, n_cmp), lambda b, g, i: (NSA_GROUPS + g, b, 0, 0)),
            pl.BlockSpec((n_sel, n_cmp), lambda b, g, i: (0, 0)),
            pl.BlockSpec((None, seq, 128), kvmap),
            pl.BlockSpec((None, nk, DH, tk), vtmap),
            pl.BlockSpec((None, seq, DH), kvmap),
            pl.BlockSpec((None, nk, DH, tk), vtmap),
            pl.BlockSpec((tq, LANE), qmap),
        ],
        out_specs=pl.BlockSpec((tq, HPG * DH), qmap),
        out_shape=jax.ShapeDtypeStruct((n, NSA_HEADS * DH), BF16),
        scratch_shapes=[
            pltpu.VMEM((n_sel, tq), F32),
            pltpu.VMEM((n_sel // SEL_HALF, HPG, DH + SEL_HALF, tq), BF16),
        ],
        compiler_params=pltpu.CompilerParams(
            dimension_semantics=("parallel", "parallel", "arbitrary"),
            vmem_limit_bytes=VMEM_LIMIT),
        name="nsa",
    )(qraw_t, qrot_t, cmp_kv, cmp_kv_t, ovt, kaug, vslc_t, kwin, vwin_t, gates)


def _gla_kernel(q_ref, k_ref, v_ref, g_ref, r_ref, gn_ref, o_ref, st_ref, *, chunk):
    c = pl.program_id(1)

    @pl.when(c == 0)
    def _():
        st_ref[...] = jnp.zeros(st_ref.shape, F32)

    tril = jnp.where(lax.broadcasted_iota(jnp.int32, (chunk, chunk), 0)
                     >= lax.broadcasted_iota(jnp.int32, (chunk, chunk), 1), 1.0, 0.0).astype(BF16)
    gn = gn_ref[...]
    outs = []
    for h in range(GLA_HEADS):
        q = q_ref[:, h * GLA_DK:(h + 1) * GLA_DK].astype(F32)
        k = k_ref[:, h * GLA_DK:(h + 1) * GLA_DK].astype(F32)
        v = v_ref[:, h * GLA_DV:(h + 1) * GLA_DV]
        g = g_ref[:, h * GLA_DK:(h + 1) * GLA_DK]
        g_hi = g.astype(BF16)
        g_lo = (g - g_hi.astype(F32)).astype(BF16)
        cum = _dot(tril, g_hi) + _dot(tril, g_lo)
        last = cum[chunk - 1:chunk, :]
        st = st_ref[h]
        o = _dot_nt((q * jnp.exp(cum)).astype(BF16), st.astype(BF16))
        pieces = []
        for sb in range(chunk // GLA_SUB):
            r0 = sb * GLA_SUB
            n = r0 + GLA_SUB
            g0 = cum[r0:r0 + 1, :]
            qt = (q[r0:n] * jnp.exp(cum[r0:n] - g0)).astype(BF16)
            kt = (k[0:n] * jnp.exp(g0 - cum[0:n])).astype(BF16)
            a = _dot_nt(qt, kt)
            causal = (lax.broadcasted_iota(jnp.int32, (GLA_SUB, n), 1)
                      <= r0 + lax.broadcasted_iota(jnp.int32, (GLA_SUB, n), 0))
            a = jnp.where(causal, a, 0.0)
            pieces.append(_dot(a.astype(BF16), v[0:n]))
        o = o + jnp.concatenate(pieces, axis=0)
        kd = (k * jnp.exp(last - cum)).astype(BF16)
        v_t = v.astype(F32).T.astype(BF16)
        st_ref[h] = st * jnp.exp(last) + _dot(v_t, kd)
        ms = jnp.mean(o * o, axis=-1, keepdims=True)
        y = o * lax.rsqrt(ms + EPS) * gn
        outs.append(y * r_ref[:, h * GLA_DV:(h + 1) * GLA_DV].astype(F32))
    o_ref[...] = jnp.concatenate(outs, axis=1).astype(BF16)


def _gla(gq, gk, gv, ga, gr, gnorm, *, batch, seq, chunk=GLA_CHUNK):
    n = batch * seq
    nc = seq // chunk
    cmap = lambda b, c: (b * nc + c, 0)
    return pl.pallas_call(
        functools.partial(_gla_kernel, chunk=chunk),
        grid=(batch, nc),
        in_specs=[
            pl.BlockSpec((chunk, GLA_HEADS * GLA_DK), cmap),
            pl.BlockSpec((chunk, GLA_HEADS * GLA_DK), cmap),
            pl.BlockSpec((chunk, GLA_HEADS * GLA_DV), cmap),
            pl.BlockSpec((chunk, GLA_HEADS * GLA_DK), cmap),
            pl.BlockSpec((chunk, GLA_HEADS * GLA_DV), cmap),
            pl.BlockSpec((1, GLA_DV), lambda b, c: (0, 0)),
        ],
        out_specs=pl.BlockSpec((chunk, GLA_HEADS * GLA_DV), cmap),
        out_shape=jax.ShapeDtypeStruct((n, GLA_HEADS * GLA_DV), BF16),
        scratch_shapes=[pltpu.VMEM((GLA_HEADS, GLA_DV, GLA_DK), F32)],
        compiler_params=pltpu.CompilerParams(
            dimension_semantics=("parallel", "arbitrary"), vmem_limit_bytes=VMEM_LIMIT),
        name="gla",
    )(gq, gk, gv, ga, gr, gnorm)


def _merge_kernel(x_ref, oa_ref, ob_ref, mg_ref, wn_ref, wg_ref, wo_ref, o_ref):
    up_a = _dot(oa_ref[...], wn_ref[...])
    up_b = _dot(ob_ref[...], wg_ref[...])
    merged = mg_ref[:, 0:D_MODEL].astype(F32) * up_a + mg_ref[:, D_MODEL:2 * D_MODEL].astype(F32) * up_b
    o_ref[...] = x_ref[...] + _dot(merged.astype(BF16), wo_ref[...])


def _merge(x2d, oa, ob, mg, wn, wg, wo, *, tm=512):
    n = x2d.shape[0]
    row = lambda i: (i, 0)
    const = lambda i: (0, 0)
    return pl.pallas_call(
        _merge_kernel,
        grid=(n // tm,),
        in_specs=[
            pl.BlockSpec((tm, D_MODEL), row),
            pl.BlockSpec((tm, 512), row),
            pl.BlockSpec((tm, 512), row),
            pl.BlockSpec((tm, 2 * D_MODEL), row),
            pl.BlockSpec((512, D_MODEL), const),
            pl.BlockSpec((512, D_MODEL), const),
            pl.BlockSpec((D_MODEL, D_MODEL), const),
        ],
        out_specs=pl.BlockSpec((tm, D_MODEL), row),
        out_shape=jax.ShapeDtypeStruct((n, D_MODEL), F32),
        compiler_params=pltpu.CompilerParams(
            dimension_semantics=("parallel",), vmem_limit_bytes=VMEM_LIMIT),
        name="merge_out",
    )(x2d, oa, ob, mg, wn, wg, wo)


def _ffn_kernel(x_ref, g1_ref, wg_ref, wu_ref, wd_ref, g2_ref, o_ref):
    x = x_ref[...]
    ms = jnp.mean(x * x, axis=-1, keepdims=True)
    h = (x * lax.rsqrt(ms + EPS) * g1_ref[...]).astype(BF16)
    a = _dot(h, wg_ref[...])
    u = _dot(h, wu_ref[...])
    act = (a * _sigmoid(a) * u).astype(BF16)
    y = x + _dot(act, wd_ref[...])
    ms2 = jnp.mean(y * y, axis=-1, keepdims=True)
    o_ref[...] = y * lax.rsqrt(ms2 + EPS) * g2_ref[...]


def _ffn(x1, g1, wg, wu, wd, g2, *, tm=256):
    n = x1.shape[0]
    row = lambda i: (i, 0)
    const = lambda i: (0, 0)
    return pl.pallas_call(
        _ffn_kernel,
        grid=(n // tm,),
        in_specs=[
            pl.BlockSpec((tm, D_MODEL), row),
            pl.BlockSpec((1, D_MODEL), const),
            pl.BlockSpec((D_MODEL, D_FF), const),
            pl.BlockSpec((D_MODEL, D_FF), const),
            pl.BlockSpec((D_FF, D_MODEL), const),
            pl.BlockSpec((1, D_MODEL), const),
        ],
        out_specs=pl.BlockSpec((tm, D_MODEL), row),
        out_shape=jax.ShapeDtypeStruct((n, D_MODEL), F32),
        compiler_params=pltpu.CompilerParams(
            dimension_semantics=("parallel",), vmem_limit_bytes=VMEM_LIMIT),
        name="ffn",
    )(x1, g1, wg, wu, wd, g2)


def _rope_swap(w, n_heads):
    w3 = w.reshape(w.shape[0], n_heads, DH)
    half = ROPE_DIM // 2
    sw = jnp.concatenate([-w3[:, :, half:ROPE_DIM], w3[:, :, 0:half],
                          jnp.zeros_like(w3[:, :, ROPE_DIM:])], axis=2)
    return sw.reshape(w.shape)


def _rope_tables(seq):
    half = ROPE_DIM // 2
    inv_freq = ROPE_THETA ** (-jnp.arange(half, dtype=F32) / half)
    ang = jnp.arange(seq).astype(F32)[:, None] * inv_freq[None, :]
    cos = jnp.cos(ang)
    sin = jnp.sin(ang)
    ones = jnp.ones((seq, DH - ROPE_DIM), F32)
    cos64 = jnp.concatenate([cos, cos, ones], axis=1)
    sin64 = jnp.concatenate([sin, sin, 0.0 * ones], axis=1)
    return jnp.concatenate([cos64, cos64], axis=1), jnp.concatenate([sin64, sin64], axis=1)


def _overlap_t(n_sel, n_cmp_pad, n_slc, n_cmp):
    c_start = np.arange(n_cmp_pad)[None, :] * CMP_STRIDE
    s_start = np.arange(n_sel)[:, None] * SLC_LEN
    ov = (c_start < s_start + SLC_LEN) & (c_start + CMP_LEN > s_start)
    ov &= (np.arange(n_cmp_pad)[None, :] < n_cmp) & (np.arange(n_sel)[:, None] < n_slc)
    return jnp.asarray(ov, dtype=BF16)


def _pack_in_weights(w_in):
    c = 0
    wq = w_in[:, c:c + 512]; c += 512
    wkv = w_in[:, c:c + 768]; c += 768
    wgate = w_in[:, c:c + 24]; c += 24
    wgq = w_in[:, c:c + 256]; c += 256
    wgk = w_in[:, c:c + 256]; c += 256
    wgv = w_in[:, c:c + 512]; c += 512
    wlr = w_in[:, c:c + GLA_RANK]; c += GLA_RANK
    wgr = w_in[:, c:c + 512]; c += 512
    wmg = w_in[:, c:c + 2048]
    d = w_in.shape[0]
    wk_slc = wkv[:, 256:384]
    wk_win = wkv[:, 512:640]
    wg3 = wgate.reshape(d, NSA_GROUPS, HPG, 3).transpose(0, 1, 3, 2).reshape(d, NSA_GROUPS, 3 * HPG)
    wg_pad = jnp.pad(wg3, ((0, 0), (0, 0), (0, LANE - 3 * HPG))).reshape(d, NSA_GROUPS * LANE)
    wlr_pad = jnp.pad(wlr, ((0, 0), (0, LANE - GLA_RANK)))
    w_all = jnp.concatenate([
        wq, _rope_swap(wq, NSA_HEADS), wkv,
        _rope_swap(wk_slc, NSA_GROUPS), _rope_swap(wk_win, NSA_GROUPS),
        wg_pad, wgq, wgk, wgv, wgr, wlr_pad, wmg], axis=1)
    return w_all.astype(BF16)


def kernel(x, norm_mix, w_in, cmp_pe_k, cmp_pe_v, cmp_k_w1, cmp_k_w2, cmp_v_w1, cmp_v_w2,
           gla_gate_w2, gla_gate_b, gla_norm, w_up_nsa, w_up_gla, w_out, norm_ffn,
           w_ffn_gate, w_ffn_up, w_ffn_down, norm_final):
    batch, seq, d = x.shape
    assert d == D_MODEL and w_in.shape[0] == 1
    assert seq % KEY_TILE == 0
    n = batch * seq
    x2d = x.reshape(n, d)

    w_all = _pack_in_weights(w_in[0])
    assert w_all.shape[1] == SEG_END
    cos_t, sin_t = _rope_tables(seq)
    w2p = jnp.pad(gla_gate_w2[0], ((0, LANE - GLA_RANK), (0, 0))).astype(BF16)
    b2 = gla_gate_b[0].reshape(1, -1)

    (qraw, qrot, cmp_src, kaug, vslc_t, kwin, vwin_t, gates, gq, gk, gv, ga, gr, mg) = _in_proj(
        x2d, norm_mix[0].reshape(1, d), w_all, cos_t, sin_t, w2p, b2, seq=seq)

    n_sub = seq // CMP_STRIDE
    n_cmp = n_sub - CMP_LEN // CMP_STRIDE + 1
    n_slc = seq // SLC_LEN
    cmp_in = cmp_src.reshape(4, batch, n_sub, CMP_STRIDE * DH)
    w1 = jnp.stack([cmp_k_w1[0], cmp_v_w1[0]]).astype(BF16)
    w2 = jnp.stack([cmp_k_w2[0], cmp_v_w2[0]]).astype(BF16)
    pe = jnp.stack([cmp_pe_k[0].reshape(1, -1), cmp_pe_v[0].reshape(1, -1)])
    pe = jnp.broadcast_to(pe, (2, 8, CMP_LEN * DH)).astype(BF16)
    w2t = jnp.stack([cmp_k_w2[0].T, cmp_v_w2[0].T]).astype(BF16)
    cmp_kv, cmp_kv_t = _compress(cmp_in, w1, w2, w2t, pe, batch=batch, n_sub=n_sub)

    n_sel = -(-n_slc // SEL_HALF) * SEL_HALF
    ovt = _overlap_t(n_sel, n_sub, n_slc, n_cmp)
    o_a = _nsa(qraw, qrot, cmp_kv, cmp_kv_t, ovt, kaug, vslc_t, kwin, vwin_t, gates,
               batch=batch, seq=seq)

    o_b = _gla(gq, gk, gv, ga, gr, gla_norm[0].reshape(1, -1), batch=batch, seq=seq)

    x1 = _merge(x2d, o_a, o_b, mg, w_up_nsa[0].astype(BF16), w_up_gla[0].astype(BF16),
                w_out[0].astype(BF16))
    out = _ffn(x1, norm_ffn[0].reshape(1, d), w_ffn_gate[0].astype(BF16), w_ffn_up[0].astype(BF16),
               w_ffn_down[0].astype(BF16), norm_final.reshape(1, d))
    return out.reshape(batch, seq, d)
```

```python
import functools

import numpy as np
import jax
import jax.numpy as jnp
from jax import lax
from jax.experimental import pallas as pl
from jax.experimental.pallas import tpu as pltpu

F32 = jnp.float32
BF16 = jnp.bfloat16

D_MODEL = 1024
NSA_HEADS = 8
NSA_GROUPS = 2
HPG = NSA_HEADS // NSA_GROUPS
DH = 64
CMP_LEN = 32
CMP_STRIDE = 16
CMP_HIDDEN = 128
SLC_LEN = 64
SLC_TOPK = 16
WINDOW = 512
FORCE_SCORE = 1.0e4
GLA_HEADS = 4
GLA_DK = 64
GLA_DV = 128
GLA_RANK = 16
GLA_TAU = 16.0
GLA_CHUNK = 64
GLA_SUB = 16
ROPE_THETA = 500000.0
ROPE_DIM = DH // 4
D_FF = 2816
EPS = 1e-6
NEG = -1e30
LOG2E = 1.4426950408889634
KEY_TILE = 256

LANE = 128
VMEM_LIMIT = 56 * 1024 * 1024
SEL_HALF = 64
SEL_GROUP = 4

SEG_Q = 0
SEG_QSW = SEG_Q + 512
SEG_KV = SEG_QSW + 512
SEG_KSW = SEG_KV + 768
SEG_GATE = SEG_KSW + 256
SEG_GLA = SEG_GATE + 256
SEG_LR = SEG_GLA + 1536
SEG_MERGE = SEG_LR + 128
SEG_END = SEG_MERGE + 2048


def _dot(a, b):
    return jnp.dot(a, b, preferred_element_type=F32)


def _dot_nt(a, b):
    return lax.dot_general(a, b, (((1,), (1,)), ((), ())), preferred_element_type=F32)


def _sigmoid(x):
    return 1.0 / (1.0 + jnp.exp(-x))


def _in_proj_kernel(x_ref, gain_ref, w_ref, cos_ref, sin_ref, w2_ref, b2_ref,
                    qraw_ref, qrot_ref, cmp_ref, kaug_ref, vslc_ref, kwin_ref, vwin_ref,
                    gate_ref, gq_ref, gk_ref, gv_ref, ga_ref, gr_ref, mg_ref, *, tm, seq):
    i = pl.program_id(0)
    x = x_ref[...]
    ms = jnp.mean(x * x, axis=-1, keepdims=True)
    h = (x * lax.rsqrt(ms + EPS) * gain_ref[...]).astype(BF16)

    def proj(a, b):
        return _dot(h, w_ref[:, a:b])

    cos = cos_ref[...]
    sin = sin_ref[...]
    cos4 = jnp.concatenate([cos] * 4, axis=1)
    sin4 = jnp.concatenate([sin] * 4, axis=1)

    scale = DH ** -0.5
    q = proj(SEG_Q, SEG_Q + 512)
    qsw = proj(SEG_QSW, SEG_QSW + 512)
    qraw_t = (q * scale).T
    qrot_t = ((q * cos4 + qsw * sin4) * (scale * LOG2E)).T
    gw = HPG * DH
    for g in range(NSA_GROUPS):
        qraw_ref[g, 0] = qraw_t[g * gw:(g + 1) * gw, :].astype(BF16)
        qrot_ref[g, 0] = qrot_t[g * gw:(g + 1) * gw, :].astype(BF16)

    kv = proj(SEG_KV, SEG_KV + 768)
    ksw = proj(SEG_KSW, SEG_KSW + 256)
    for j in range(2):
        for g in range(NSA_GROUPS):
            cmp_ref[2 * j + g] = kv[:, j * 128 + g * DH:j * 128 + (g + 1) * DH].astype(BF16)
    kslc = kv[:, 256:384] * cos + ksw[:, 0:128] * sin
    kwin = kv[:, 512:640] * cos + ksw[:, 128:256] * sin
    vslc_t = kv[:, 384:512].T
    vwin_t = kv[:, 640:768].T
    pos = (i * tm) % seq + lax.broadcasted_iota(jnp.int32, (tm, SEL_HALF), 0)
    blk = (pos // SLC_LEN) % SEL_HALF
    onehot = jnp.where(lax.broadcasted_iota(jnp.int32, (tm, SEL_HALF), 1) == blk, 1.0, 0.0)
    for g in range(NSA_GROUPS):
        sl = slice(g * DH, (g + 1) * DH)
        kaug_ref[g] = jnp.concatenate([kslc[:, sl], onehot], axis=1).astype(BF16)
        vslc_ref[g, 0] = vslc_t[sl, :].astype(BF16)
        kwin_ref[g] = kwin[:, sl].astype(BF16)
        vwin_ref[g, 0] = vwin_t[sl, :].astype(BF16)

    gate_ref[...] = _sigmoid(proj(SEG_GATE, SEG_GATE + 256))

    gla = proj(SEG_GLA, SEG_GLA + 1536)
    gq_ref[...] = (gla[:, 0:256] * (GLA_DK ** -0.5)).astype(BF16)
    gk_ref[...] = gla[:, 256:512].astype(BF16)
    gv_ref[...] = gla[:, 512:1024].astype(BF16)
    r = gla[:, 1024:1536]
    gr_ref[...] = (r * _sigmoid(r)).astype(BF16)

    lr = proj(SEG_LR, SEG_LR + 128)
    z = _dot(lr.astype(BF16), w2_ref[...]) + b2_ref[...]
    log_sig = jnp.minimum(z, 0.0) - jnp.log1p(jnp.exp(-jnp.abs(z)))
    ga_ref[...] = log_sig * (1.0 / GLA_TAU)

    mg_ref[...] = _sigmoid(proj(SEG_MERGE, SEG_MERGE + 2048)).astype(BF16)


def _in_proj(x2d, gain, w_all, cos_t, sin_t, w2p, b2, *, seq, tm=KEY_TILE):
    n = x2d.shape[0]
    nt = seq // tm
    row = lambda i: (i, 0)
    grp = lambda i: (0, i, 0)
    const = lambda i: (0, 0)
    tab = lambda i: (i % nt, 0)
    out_shape = (
        jax.ShapeDtypeStruct((NSA_GROUPS, n // tm, HPG * DH, tm), BF16),
        jax.ShapeDtypeStruct((NSA_GROUPS, n // tm, HPG * DH, tm), BF16),
        jax.ShapeDtypeStruct((4, n, DH), BF16),
        jax.ShapeDtypeStruct((NSA_GROUPS, n, 128), BF16),
        jax.ShapeDtypeStruct((NSA_GROUPS, n // tm, DH, tm), BF16),
        jax.ShapeDtypeStruct((NSA_GROUPS, n, DH), BF16),
        jax.ShapeDtypeStruct((NSA_GROUPS, n // tm, DH, tm), BF16),
        jax.ShapeDtypeStruct((n, 256), F32),
        jax.ShapeDtypeStruct((n, 256), BF16),
        jax.ShapeDtypeStruct((n, 256), BF16),
        jax.ShapeDtypeStruct((n, 512), BF16),
        jax.ShapeDtypeStruct((n, 256), F32),
        jax.ShapeDtypeStruct((n, 512), BF16),
        jax.ShapeDtypeStruct((n, 2048), BF16),
    )
    out_specs = (
        pl.BlockSpec((NSA_GROUPS, 1, HPG * DH, tm), lambda i: (0, i, 0, 0)),
        pl.BlockSpec((NSA_GROUPS, 1, HPG * DH, tm), lambda i: (0, i, 0, 0)),
        pl.BlockSpec((4, tm, DH), grp),
        pl.BlockSpec((NSA_GROUPS, tm, 128), grp),
        pl.BlockSpec((NSA_GROUPS, 1, DH, tm), lambda i: (0, i, 0, 0)),
        pl.BlockSpec((NSA_GROUPS, tm, DH), grp),
        pl.BlockSpec((NSA_GROUPS, 1, DH, tm), lambda i: (0, i, 0, 0)),
        pl.BlockSpec((tm, 256), row),
        pl.BlockSpec((tm, 256), row), pl.BlockSpec((tm, 256), row), pl.BlockSpec((tm, 512), row),
        pl.BlockSpec((tm, 256), row), pl.BlockSpec((tm, 512), row),
        pl.BlockSpec((tm, 2048), row),
    )
    in_specs = [
        pl.BlockSpec((tm, D_MODEL), row),
        pl.BlockSpec((1, D_MODEL), const),
        pl.BlockSpec((D_MODEL, SEG_END), const),
        pl.BlockSpec((tm, LANE), tab),
        pl.BlockSpec((tm, LANE), tab),
        pl.BlockSpec((LANE, 256), const),
        pl.BlockSpec((1, 256), const),
    ]
    return pl.pallas_call(
        functools.partial(_in_proj_kernel, tm=tm, seq=seq),
        grid=(n // tm,),
        in_specs=in_specs, out_specs=out_specs, out_shape=out_shape,
        compiler_params=pltpu.CompilerParams(
            dimension_semantics=("parallel",), vmem_limit_bytes=VMEM_LIMIT),
        name="in_proj",
    )(x2d, gain, w_all, cos_t, sin_t, w2p, b2)


def _compress_kernel(x_ref, w1_ref, w2_ref, w2t_ref, pe_ref, o_ref, ot_ref, *, n_sub):
    half = CMP_STRIDE * DH
    x = x_ref[...]
    u = _dot(x, w1_ref[0:half, :])
    v = _dot(x, w1_ref[half:2 * half, :])
    c = _dot(pe_ref[...], w1_ref[...])[0:1, :]
    hid = u + pltpu.roll(v, shift=n_sub - 1, axis=0) + c
    hid = (hid * _sigmoid(hid)).astype(BF16)
    out = _dot(hid, w2_ref[...])
    rowi = lax.broadcasted_iota(jnp.int32, out.shape, 0)
    o_ref[...] = jnp.where(rowi < n_sub - 1, out, 0.0).astype(BF16)
    out_t = _dot_nt(w2t_ref[...], hid)
    coli = lax.broadcasted_iota(jnp.int32, out_t.shape, 1)
    ot_ref[...] = jnp.where(coli < n_sub - 1, out_t, 0.0).astype(BF16)


def _compress(src, w1, w2, w2t, pe, *, batch, n_sub):
    return pl.pallas_call(
        functools.partial(_compress_kernel, n_sub=n_sub),
        grid=(4, batch),
        in_specs=[
            pl.BlockSpec((None, None, n_sub, CMP_STRIDE * DH), lambda j, b: (j, b, 0, 0)),
            pl.BlockSpec((None, CMP_LEN * DH, CMP_HIDDEN), lambda j, b: (j // 2, 0, 0)),
            pl.BlockSpec((None, CMP_HIDDEN, DH), lambda j, b: (j // 2, 0, 0)),
            pl.BlockSpec((None, DH, CMP_HIDDEN), lambda j, b: (j // 2, 0, 0)),
            pl.BlockSpec((None, 8, CMP_LEN * DH), lambda j, b: (j // 2, 0, 0)),
        ],
        out_specs=(pl.BlockSpec((None, None, n_sub, DH), lambda j, b: (j, b, 0, 0)),
                   pl.BlockSpec((None, None, DH, n_sub), lambda j, b: (j, b, 0, 0))),
        out_shape=(jax.ShapeDtypeStruct((4, batch, n_sub, DH), BF16),
                   jax.ShapeDtypeStruct((4, batch, DH, n_sub), BF16)),
        compiler_params=pltpu.CompilerParams(
            dimension_semantics=("parallel", "parallel"), vmem_limit_bytes=VMEM_LIMIT),
        name="compress",
    )(src, w1, w2, w2t, pe)


def _softmax_tiles(scores, vts, carry):
    stats = []
    for s, (m, l, _) in zip(scores, carry):
        m_new = jnp.maximum(m, jnp.max(s, axis=0, keepdims=True))
        alpha = jnp.exp2(m - m_new)
        p = jnp.exp2(s - m_new)
        stats.append((m_new, alpha, alpha * l + jnp.sum(p, axis=0, keepdims=True), p.astype(BF16)))
    return tuple((m_new, l_new, alpha * acc + _dot(vt, p))
                 for (m_new, alpha, l_new, p), vt, (_, _, acc) in zip(stats, vts, carry))


def _nsa_kernel(qraw_ref, qrot_ref, kc_ref, vct_ref, ovt_ref, kaug_ref, vst_ref, kw_ref, vwt_ref,
                gate_ref, o_ref, qa_ref, *, tq, topk):
    i = pl.program_id(2)
    s0 = i * tq
    n_cmp = kc_ref.shape[0]
    n_sel = ovt_ref.shape[0]

    def init_carry():
        lane = lax.broadcasted_iota(jnp.int32, (1, tq), 1)
        one = (jnp.where(lane >= 0, NEG, 0.0), jnp.where(lane >= 0, 0.0, 1.0), jnp.zeros((DH, tq), F32))
        return tuple(one for _ in range(HPG))

    kc = kc_ref[...]
    vct = vct_ref[...]
    cend = lax.broadcasted_iota(jnp.int32, (n_cmp, tq), 0) * CMP_STRIDE + (CMP_LEN - 1)
    cmask = cend <= s0 + lax.broadcasted_iota(jnp.int32, (n_cmp, tq), 1)
    scores = [_dot(kc, qraw_ref[h * DH:(h + 1) * DH, :]) for h in range(HPG)]
    psum = jnp.zeros((n_cmp, tq), F32)
    probs = []
    for s in scores:
        s = jnp.where(cmask, s, NEG)
        m = jnp.max(s, axis=0, keepdims=True)
        e = jnp.where(cmask, jnp.exp(s - m), 0.0)
        l = jnp.sum(e, axis=0, keepdims=True)
        p = e * jnp.where(l > 0.0, 1.0 / l, 0.0)
        probs.append(p.astype(BF16))
        psum = psum + p
    oc_t = [_dot(vct, p) for p in probs]

    key_i = lax.broadcasted_iota(jnp.int32, (tq, tq), 0)
    qry_i = lax.broadcasted_iota(jnp.int32, (tq, tq), 1)
    n_back = WINDOW // tq
    w_tiles = [jnp.maximum(i - n_back + u, 0) for u in range(n_back)]
    w_keys = [kw_ref[pl.ds(pl.multiple_of(t * tq, tq), tq), :] for t in w_tiles]
    w_bands = [(i >= n_back - u) & ((qry_i < key_i) if u == 0 else True) for u in range(n_back)]
    w_scores = [jnp.concatenate(
        [jnp.where(band, _dot(kt, qrot_ref[h * DH:(h + 1) * DH, :]), NEG)
         for kt, band in zip(w_keys, w_bands)], axis=0) for h in range(HPG)]
    w_vt = jnp.concatenate([vwt_ref[t] for t in w_tiles], axis=1)
    win_carry = _softmax_tiles(w_scores, [w_vt] * HPG, init_carry())

    ovt = ovt_ref[...]
    p_hi = psum.astype(BF16)
    p_lo = (psum - p_hi.astype(F32)).astype(BF16)
    imp_t = _dot(ovt, p_hi) + _dot(ovt, p_lo)
    jrow = lax.broadcasted_iota(jnp.int32, (n_sel, tq), 0)
    tlane = s0 + lax.broadcasted_iota(jnp.int32, (n_sel, tq), 1)
    cur = tlane // SLC_LEN
    valid = jrow * SLC_LEN <= tlane
    forced = (jrow == 0) | (jrow == cur) | (jrow == cur - 1)
    score = jnp.where(valid, jnp.where(forced, FORCE_SCORE, imp_t), -jnp.inf)

    bits = pltpu.bitcast(score, jnp.int32)
    thr = jnp.zeros((1, tq), jnp.int32)
    for b in range(30, -1, -1):
        cand = thr | (1 << b)
        n_ge = jnp.sum(jnp.where(bits >= cand, 1.0, 0.0), axis=0, keepdims=True)
        thr = jnp.where(n_ge >= float(topk), cand, thr)
    above = bits > thr
    tied = bits == thr
    n_above = jnp.sum(jnp.where(above, 1.0, 0.0), axis=0, keepdims=True)
    lower = jnp.where(lax.broadcasted_iota(jnp.int32, (n_sel, n_sel), 1)
                      < lax.broadcasted_iota(jnp.int32, (n_sel, n_sel), 0), 1.0, 0.0).astype(BF16)
    tied_before = _dot(lower, jnp.where(tied, 1.0, 0.0).astype(BF16))
    chosen = above | (tied & (tied_before < float(topk) - n_above))
    bias_t = jnp.where(chosen & valid, 0.0, NEG).astype(BF16)

    n_half = n_sel // SEL_HALF
    for a in range(n_half):
        for h in range(HPG):
            qa_ref[a, h, 0:DH, :] = qrot_ref[h * DH:(h + 1) * DH, :]
            qa_ref[a, h, DH:DH + SEL_HALF, :] = bias_t[a * SEL_HALF:(a + 1) * SEL_HALF, :]

    tiles_per_half = SEL_HALF * SLC_LEN // tq

    def off_diagonal(lo, scores_fn, vt_ref, group):
        def step(first, count, carry):
            vt = vt_ref[first] if count == 1 else jnp.concatenate(
                [vt_ref[first + u] for u in range(count)], axis=1)
            return _softmax_tiles(scores_fn(first, count), [vt] * HPG, carry)

        n_grouped = (i - lo) // group
        carry = lax.fori_loop(0, n_grouped, lambda t, c: step(lo + t * group, group, c), init_carry())
        return lax.fori_loop(lo + n_grouped * group, i, lambda t, c: step(t, 1, c), carry)

    def sel_scores(first, count):
        kt = kaug_ref[pl.ds(pl.multiple_of(first * tq, tq), count * tq), :]
        half = first // tiles_per_half
        return [_dot(kt, qa_ref[half, h]) for h in range(HPG)]

    sel_carry = off_diagonal(0, sel_scores, vst_ref, SEL_GROUP)
    win_diag = [_dot(kw_ref[pl.ds(pl.multiple_of(s0, tq), tq), :], qa_ref[0, h, 0:DH, :])
                for h in range(HPG)]
    diag = [jnp.where(key_i <= qry_i, s, NEG) for s in sel_scores(i, 1) + win_diag]
    carry = _softmax_tiles(diag, [vst_ref[i]] * HPG + [vwt_ref[i]] * HPG, sel_carry + win_carry)
    os_t = [acc / l for (_, l, acc) in carry[:HPG]]
    ow_t = [acc / l for (_, l, acc) in carry[HPG:]]

    gt_t = gate_ref[...].T
    outs = [gt_t[h:h + 1, :] * oc_t[h] + gt_t[HPG + h:HPG + h + 1, :] * os_t[h]
            + gt_t[2 * HPG + h:2 * HPG + h + 1, :] * ow_t[h] for h in range(HPG)]
    o_ref[...] = jnp.concatenate(outs, axis=0).T.astype(BF16)


def _nsa(qraw_t, qrot_t, cmp_kv, cmp_kv_t, ovt, kaug, vslc_t, kwin, vwin_t, gates, *, batch, seq,
         tq=KEY_TILE):
    n = batch * seq
    nq = seq // tq
    nk = nq
    tk = tq
    n_cmp = cmp_kv.shape[2]
    n_sel = ovt.shape[0]
    topk = min(SLC_TOPK, seq // SLC_LEN)
    qmap = lambda b, g, i: (b * nq + i, g)
    qtmap = lambda b, g, i: (g, b * nq + i, 0, 0)
    kvmap = lambda b, g, i: (g, b, 0)
    vtmap = lambda b, g, i: (g, b, 0, 0)
    return pl.pallas_call(
        functools.partial(_nsa_kernel, tq=tq, topk=topk),
        grid=(batch, NSA_GROUPS, nq),
        in_specs=[
            pl.BlockSpec((None, None, HPG * DH, tq), qtmap),
            pl.BlockSpec((None, None, HPG * DH, tq), qtmap),
            pl.BlockSpec((None, None, n_cmp, DH), lambda b, g, i: (g, b, 0, 0)),
            pl.BlockSpec((None, None, DH, n_cmp), lambda b, g, i: (NSA_GROUPS + g, b, 0, 0)),
            pl.BlockSpec((n_sel, n_cmp), lambda b, g, i: (0, 0)),
            pl.BlockSpec((None, seq, 128), kvmap),
            pl.BlockSpec((None, nk, DH, tk), vtmap),
            pl.BlockSpec((None, seq, DH), kvmap),
            pl.BlockSpec((None, nk, DH, tk), vtmap),
            pl.BlockSpec((tq, LANE), qmap),
        ],
        out_specs=pl.BlockSpec((tq, HPG * DH), qmap),
        out_shape=jax.ShapeDtypeStruct((n, NSA_HEADS * DH), BF16),
        scratch_shapes=[
            pltpu.VMEM((n_sel // SEL_HALF, HPG, DH + SEL_HALF, tq), BF16),
        ],
        compiler_params=pltpu.CompilerParams(
            dimension_semantics=("parallel", "parallel", "arbitrary"),
            vmem_limit_bytes=VMEM_LIMIT),
        name="nsa",
    )(qraw_t, qrot_t, cmp_kv, cmp_kv_t, ovt, kaug, vslc_t, kwin, vwin_t, gates)


def _gla_kernel(q_ref, k_ref, v_ref, g_ref, r_ref, gn_ref, o_ref, st_ref, *, chunk):
    c = pl.program_id(1)

    @pl.when(c == 0)
    def _():
        st_ref[...] = jnp.zeros(st_ref.shape, F32)

    tril = jnp.where(lax.broadcasted_iota(jnp.int32, (chunk, chunk), 0)
                     >= lax.broadcasted_iota(jnp.int32, (chunk, chunk), 1), 1.0, 0.0).astype(BF16)
    gn = gn_ref[...]
    outs = []
    for h in range(GLA_HEADS):
        q = q_ref[:, h * GLA_DK:(h + 1) * GLA_DK].astype(F32)
        k = k_ref[:, h * GLA_DK:(h + 1) * GLA_DK].astype(F32)
        v = v_ref[:, h * GLA_DV:(h + 1) * GLA_DV]
        g = g_ref[:, h * GLA_DK:(h + 1) * GLA_DK]
        g_hi = g.astype(BF16)
        g_lo = (g - g_hi.astype(F32)).astype(BF16)
        cum = _dot(tril, g_hi) + _dot(tril, g_lo)
        last = cum[chunk - 1:chunk, :]
        st = st_ref[h]
        o = _dot_nt((q * jnp.exp(cum)).astype(BF16), st.astype(BF16))
        pieces = []
        for sb in range(chunk // GLA_SUB):
            r0 = sb * GLA_SUB
            n = r0 + GLA_SUB
            g0 = cum[r0:r0 + 1, :]
            qt = (q[r0:n] * jnp.exp(cum[r0:n] - g0)).astype(BF16)
            kt = (k[0:n] * jnp.exp(g0 - cum[0:n])).astype(BF16)
            a = _dot_nt(qt, kt)
            causal = (lax.broadcasted_iota(jnp.int32, (GLA_SUB, n), 1)
                      <= r0 + lax.broadcasted_iota(jnp.int32, (GLA_SUB, n), 0))
            a = jnp.where(causal, a, 0.0)
            pieces.append(_dot(a.astype(BF16), v[0:n]))
        o = o + jnp.concatenate(pieces, axis=0)
        kd = (k * jnp.exp(last - cum)).astype(BF16)
        v_t = v.astype(F32).T.astype(BF16)
        st_ref[h] = st * jnp.exp(last) + _dot(v_t, kd)
        ms = jnp.mean(o * o, axis=-1, keepdims=True)
        y = o * lax.rsqrt(ms + EPS) * gn
        outs.append(y * r_ref[:, h * GLA_DV:(h + 1) * GLA_DV].astype(F32))
    o_ref[...] = jnp.concatenate(outs, axis=1).astype(BF16)


def _gla(gq, gk, gv, ga, gr, gnorm, *, batch, seq, chunk=GLA_CHUNK):
    n = batch * seq
    nc = seq // chunk
    cmap = lambda b, c: (b * nc + c, 0)
    return pl.pallas_call(
        functools.partial(_gla_kernel, chunk=chunk),
        grid=(batch, nc),
        in_specs=[
            pl.BlockSpec((chunk, GLA_HEADS * GLA_DK), cmap),
            pl.BlockSpec((chunk, GLA_HEADS * GLA_DK), cmap),
            pl.BlockSpec((chunk, GLA_HEADS * GLA_DV), cmap),
            pl.BlockSpec((chunk, GLA_HEADS * GLA_DK), cmap),
            pl.BlockSpec((chunk, GLA_HEADS * GLA_DV), cmap),
            pl.BlockSpec((1, GLA_DV), lambda b, c: (0, 0)),
        ],
        out_specs=pl.BlockSpec((chunk, GLA_HEADS * GLA_DV), cmap),
        out_shape=jax.ShapeDtypeStruct((n, GLA_HEADS * GLA_DV), BF16),
        scratch_shapes=[pltpu.VMEM((GLA_HEADS, GLA_DV, GLA_DK), F32)],
        compiler_params=pltpu.CompilerParams(
            dimension_semantics=("parallel", "arbitrary"), vmem_limit_bytes=VMEM_LIMIT),
        name="gla",
    )(gq, gk, gv, ga, gr, gnorm)


def _merge_kernel(x_ref, oa_ref, ob_ref, mg_ref, wn_ref, wg_ref, wo_ref, o_ref):
    up_a = _dot(oa_ref[...], wn_ref[...])
    up_b = _dot(ob_ref[...], wg_ref[...])
    merged = mg_ref[:, 0:D_MODEL].astype(F32) * up_a + mg_ref[:, D_MODEL:2 * D_MODEL].astype(F32) * up_b
    o_ref[...] = x_ref[...] + _dot(merged.astype(BF16), wo_ref[...])


def _merge(x2d, oa, ob, mg, wn, wg, wo, *, tm=512):
    n = x2d.shape[0]
    row = lambda i: (i, 0)
    const = lambda i: (0, 0)
    return pl.pallas_call(
        _merge_kernel,
        grid=(n // tm,),
        in_specs=[
            pl.BlockSpec((tm, D_MODEL), row),
            pl.BlockSpec((tm, 512), row),
            pl.BlockSpec((tm, 512), row),
            pl.BlockSpec((tm, 2 * D_MODEL), row),
            pl.BlockSpec((512, D_MODEL), const),
            pl.BlockSpec((512, D_MODEL), const),
            pl.BlockSpec((D_MODEL, D_MODEL), const),
        ],
        out_specs=pl.BlockSpec((tm, D_MODEL), row),
        out_shape=jax.ShapeDtypeStruct((n, D_MODEL), F32),
        compiler_params=pltpu.CompilerParams(
            dimension_semantics=("parallel",), vmem_limit_bytes=VMEM_LIMIT),
        name="merge_out",
    )(x2d, oa, ob, mg, wn, wg, wo)


def _ffn_kernel(x_ref, g1_ref, wg_ref, wu_ref, wd_ref, g2_ref, o_ref):
    x = x_ref[...]
    ms = jnp.mean(x * x, axis=-1, keepdims=True)
    h = (x * lax.rsqrt(ms + EPS) * g1_ref[...]).astype(BF16)
    a = _dot(h, wg_ref[...])
    u = _dot(h, wu_ref[...])
    act = (a * _sigmoid(a) * u).astype(BF16)
    y = x + _dot(act, wd_ref[...])
    ms2 = jnp.mean(y * y, axis=-1, keepdims=True)
    o_ref[...] = y * lax.rsqrt(ms2 + EPS) * g2_ref[...]


def _ffn(x1, g1, wg, wu, wd, g2, *, tm=256):
    n = x1.shape[0]
    row = lambda i: (i, 0)
    const = lambda i: (0, 0)
    return pl.pallas_call(
        _ffn_kernel,
        grid=(n // tm,),
        in_specs=[
            pl.BlockSpec((tm, D_MODEL), row),
            pl.BlockSpec((1, D_MODEL), const),
            pl.BlockSpec((D_MODEL, D_FF), const),
            pl.BlockSpec((D_MODEL, D_FF), const),
            pl.BlockSpec((D_FF, D_MODEL), const),
            pl.BlockSpec((1, D_MODEL), const),
        ],
        out_specs=pl.BlockSpec((tm, D_MODEL), row),
        out_shape=jax.ShapeDtypeStruct((n, D_MODEL), F32),
        compiler_params=pltpu.CompilerParams(
            dimension_semantics=("parallel",), vmem_limit_bytes=VMEM_LIMIT),
        name="ffn",
    )(x1, g1, wg, wu, wd, g2)


def _rope_swap(w, n_heads):
    w3 = w.reshape(w.shape[0], n_heads, DH)
    half = ROPE_DIM // 2
    sw = jnp.concatenate([-w3[:, :, half:ROPE_DIM], w3[:, :, 0:half],
                          jnp.zeros_like(w3[:, :, ROPE_DIM:])], axis=2)
    return sw.reshape(w.shape)


def _rope_tables(seq):
    half = ROPE_DIM // 2
    inv_freq = ROPE_THETA ** (-np.arange(half, dtype=np.float64) / half)
    ang = np.arange(seq, dtype=np.float64)[:, None] * inv_freq[None, :]
    ones = np.ones((seq, DH - ROPE_DIM))
    cos64 = np.concatenate([np.cos(ang), np.cos(ang), ones], axis=1)
    sin64 = np.concatenate([np.sin(ang), np.sin(ang), 0.0 * ones], axis=1)
    return (jnp.asarray(np.concatenate([cos64, cos64], axis=1), F32),
            jnp.asarray(np.concatenate([sin64, sin64], axis=1), F32))


def _overlap_t(n_sel, n_cmp_pad, n_slc, n_cmp):
    c_start = np.arange(n_cmp_pad)[None, :] * CMP_STRIDE
    s_start = np.arange(n_sel)[:, None] * SLC_LEN
    ov = (c_start < s_start + SLC_LEN) & (c_start + CMP_LEN > s_start)
    ov &= (np.arange(n_cmp_pad)[None, :] < n_cmp) & (np.arange(n_sel)[:, None] < n_slc)
    return jnp.asarray(ov, dtype=BF16)


def _pack_in_weights(w_in):
    c = 0
    wq = w_in[:, c:c + 512]; c += 512
    wkv = w_in[:, c:c + 768]; c += 768
    wgate = w_in[:, c:c + 24]; c += 24
    wgq = w_in[:, c:c + 256]; c += 256
    wgk = w_in[:, c:c + 256]; c += 256
    wgv = w_in[:, c:c + 512]; c += 512
    wlr = w_in[:, c:c + GLA_RANK]; c += GLA_RANK
    wgr = w_in[:, c:c + 512]; c += 512
    wmg = w_in[:, c:c + 2048]
    d = w_in.shape[0]
    wk_slc = wkv[:, 256:384]
    wk_win = wkv[:, 512:640]
    wg3 = wgate.reshape(d, NSA_GROUPS, HPG, 3).transpose(0, 1, 3, 2).reshape(d, NSA_GROUPS, 3 * HPG)
    wg_pad = jnp.pad(wg3, ((0, 0), (0, 0), (0, LANE - 3 * HPG))).reshape(d, NSA_GROUPS * LANE)
    wlr_pad = jnp.pad(wlr, ((0, 0), (0, LANE - GLA_RANK)))
    w_all = jnp.concatenate([
        wq, _rope_swap(wq, NSA_HEADS), wkv,
        _rope_swap(wk_slc, NSA_GROUPS), _rope_swap(wk_win, NSA_GROUPS),
        wg_pad, wgq, wgk, wgv, wgr, wlr_pad, wmg], axis=1)
    return w_all.astype(BF16)


def kernel(x, norm_mix, w_in, cmp_pe_k, cmp_pe_v, cmp_k_w1, cmp_k_w2, cmp_v_w1, cmp_v_w2,
           gla_gate_w2, gla_gate_b, gla_norm, w_up_nsa, w_up_gla, w_out, norm_ffn,
           w_ffn_gate, w_ffn_up, w_ffn_down, norm_final):
    batch, seq, d = x.shape
    assert d == D_MODEL and w_in.shape[0] == 1
    assert seq % KEY_TILE == 0
    n = batch * seq
    x2d = x.reshape(n, d)

    w_all = _pack_in_weights(w_in[0])
    assert w_all.shape[1] == SEG_END
    cos_t, sin_t = _rope_tables(seq)
    w2p = jnp.pad(gla_gate_w2[0], ((0, LANE - GLA_RANK), (0, 0))).astype(BF16)
    b2 = gla_gate_b[0].reshape(1, -1)

    (qraw, qrot, cmp_src, kaug, vslc_t, kwin, vwin_t, gates, gq, gk, gv, ga, gr, mg) = _in_proj(
        x2d, norm_mix[0].reshape(1, d), w_all, cos_t, sin_t, w2p, b2, seq=seq)

    n_sub = seq // CMP_STRIDE
    n_cmp = n_sub - CMP_LEN // CMP_STRIDE + 1
    n_slc = seq // SLC_LEN
    cmp_in = cmp_src.reshape(4, batch, n_sub, CMP_STRIDE * DH)
    w1 = jnp.stack([cmp_k_w1[0], cmp_v_w1[0]]).astype(BF16)
    w2 = jnp.stack([cmp_k_w2[0], cmp_v_w2[0]]).astype(BF16)
    pe = jnp.stack([cmp_pe_k[0].reshape(1, -1), cmp_pe_v[0].reshape(1, -1)])
    pe = jnp.broadcast_to(pe, (2, 8, CMP_LEN * DH)).astype(BF16)
    w2t = jnp.stack([cmp_k_w2[0].T, cmp_v_w2[0].T]).astype(BF16)
    cmp_kv, cmp_kv_t = _compress(cmp_in, w1, w2, w2t, pe, batch=batch, n_sub=n_sub)

    n_sel = -(-n_slc // SEL_HALF) * SEL_HALF
    ovt = _overlap_t(n_sel, n_sub, n_slc, n_cmp)
    o_a = _nsa(qraw, qrot, cmp_kv, cmp_kv_t, ovt, kaug, vslc_t, kwin, vwin_t, gates,
               batch=batch, seq=seq)

    o_b = _gla(gq, gk, gv, ga, gr, gla_norm[0].reshape(1, -1), batch=batch, seq=seq)

    x1 = _merge(x2d, o_a, o_b, mg, w_up_nsa[0].astype(BF16), w_up_gla[0].astype(BF16),
                w_out[0].astype(BF16))
    out = _ffn(x1, norm_ffn[0].reshape(1, d), w_ffn_gate[0].astype(BF16), w_ffn_up[0].astype(BF16),
               w_ffn_down[0].astype(BF16), norm_final.reshape(1, d))
    return out.reshape(batch, seq, d)
```

```python
import functools

import numpy as np
import jax
import jax.numpy as jnp
from jax import lax
from jax.experimental import pallas as pl
from jax.experimental.pallas import tpu as pltpu

F32 = jnp.float32
BF16 = jnp.bfloat16

D_MODEL = 1024
NSA_HEADS = 8
NSA_GROUPS = 2
HPG = NSA_HEADS // NSA_GROUPS
DH = 64
CMP_LEN = 32
CMP_STRIDE = 16
CMP_HIDDEN = 128
SLC_LEN = 64
SLC_TOPK = 16
WINDOW = 512
FORCE_SCORE = 1.0e4
GLA_HEADS = 4
GLA_DK = 64
GLA_DV = 128
GLA_RANK = 16
GLA_TAU = 16.0
GLA_CHUNK = 64
GLA_SUB = 16
ROPE_THETA = 500000.0
ROPE_DIM = DH // 4
D_FF = 2816
EPS = 1e-6
NEG = -1e30
LOG2E = 1.4426950408889634
KEY_TILE = 256

LANE = 128
VMEM_LIMIT = 56 * 1024 * 1024
SEL_HALF = 64
SEL_GROUP = 4

SEG_Q = 0
SEG_QSW = SEG_Q + 512
SEG_KV = SEG_QSW + 512
SEG_KSW = SEG_KV + 768
SEG_GATE = SEG_KSW + 256
SEG_GLA = SEG_GATE + 256
SEG_LR = SEG_GLA + 1536
SEG_MERGE = SEG_LR + 128
SEG_END = SEG_MERGE + 2048


def _dot(a, b):
    return jnp.dot(a, b, preferred_element_type=F32)


def _dot_nt(a, b):
    return lax.dot_general(a, b, (((1,), (1,)), ((), ())), preferred_element_type=F32)


def _sigmoid(x):
    return 1.0 / (1.0 + jnp.exp(-x))


def _in_proj_kernel(x_ref, gain_ref, w_ref, cos_ref, sin_ref, w2_ref, b2_ref,
                    qraw_ref, qrot_ref, cmp_ref, kaug_ref, vslc_ref, kwin_ref, vwin_ref,
                    gate_ref, gq_ref, gk_ref, gv_ref, ga_ref, gr_ref, mg_ref, *, tm, seq):
    i = pl.program_id(0)
    x = x_ref[...]
    ms = jnp.mean(x * x, axis=-1, keepdims=True)
    h = (x * lax.rsqrt(ms + EPS) * gain_ref[...]).astype(BF16)

    def proj(a, b):
        return _dot(h, w_ref[:, a:b])

    cos = cos_ref[...]
    sin = sin_ref[...]
    cos4 = jnp.concatenate([cos] * 4, axis=1)
    sin4 = jnp.concatenate([sin] * 4, axis=1)

    scale = DH ** -0.5
    q = proj(SEG_Q, SEG_Q + 512)
    qsw = proj(SEG_QSW, SEG_QSW + 512)
    qraw_t = (q * scale).T
    qrot_t = ((q * cos4 + qsw * sin4) * (scale * LOG2E)).T
    gw = HPG * DH
    for g in range(NSA_GROUPS):
        qraw_ref[g, 0] = qraw_t[g * gw:(g + 1) * gw, :].astype(BF16)
        qrot_ref[g, 0] = qrot_t[g * gw:(g + 1) * gw, :].astype(BF16)

    kv = proj(SEG_KV, SEG_KV + 768)
    ksw = proj(SEG_KSW, SEG_KSW + 256)
    for j in range(2):
        for g in range(NSA_GROUPS):
            cmp_ref[2 * j + g] = kv[:, j * 128 + g * DH:j * 128 + (g + 1) * DH].astype(BF16)
    kslc = kv[:, 256:384] * cos + ksw[:, 0:128] * sin
    kwin = kv[:, 512:640] * cos + ksw[:, 128:256] * sin
    vslc_t = kv[:, 384:512].T
    vwin_t = kv[:, 640:768].T
    pos = (i * tm) % seq + lax.broadcasted_iota(jnp.int32, (tm, SEL_HALF), 0)
    blk = (pos // SLC_LEN) % SEL_HALF
    onehot = jnp.where(lax.broadcasted_iota(jnp.int32, (tm, SEL_HALF), 1) == blk, 1.0, 0.0)
    for g in range(NSA_GROUPS):
        sl = slice(g * DH, (g + 1) * DH)
        kaug_ref[g] = jnp.concatenate([kslc[:, sl], onehot], axis=1).astype(BF16)
        vslc_ref[g, 0] = vslc_t[sl, :].astype(BF16)
        kwin_ref[g] = kwin[:, sl].astype(BF16)
        vwin_ref[g, 0] = vwin_t[sl, :].astype(BF16)

    gate_ref[...] = _sigmoid(proj(SEG_GATE, SEG_GATE + 256))

    gla = proj(SEG_GLA, SEG_GLA + 1536)
    gq_ref[...] = (gla[:, 0:256] * (GLA_DK ** -0.5)).astype(BF16)
    gk_ref[...] = gla[:, 256:512].astype(BF16)
    gv_ref[...] = gla[:, 512:1024].astype(BF16)
    r = gla[:, 1024:1536]
    gr_ref[...] = (r * _sigmoid(r)).astype(BF16)

    lr = proj(SEG_LR, SEG_LR + 128)
    z = _dot(lr.astype(BF16), w2_ref[...]) + b2_ref[...]
    log_sig = jnp.minimum(z, 0.0) - jnp.log1p(jnp.exp(-jnp.abs(z)))
    ga_ref[...] = log_sig * (1.0 / GLA_TAU)

    mg_ref[...] = _sigmoid(proj(SEG_MERGE, SEG_MERGE + 2048)).astype(BF16)


def _in_proj(x2d, gain, w_all, cos_t, sin_t, w2p, b2, *, seq, tm=KEY_TILE):
    n = x2d.shape[0]
    nt = seq // tm
    row = lambda i: (i, 0)
    grp = lambda i: (0, i, 0)
    const = lambda i: (0, 0)
    tab = lambda i: (i % nt, 0)
    out_shape = (
        jax.ShapeDtypeStruct((NSA_GROUPS, n // tm, HPG * DH, tm), BF16),
        jax.ShapeDtypeStruct((NSA_GROUPS, n // tm, HPG * DH, tm), BF16),
        jax.ShapeDtypeStruct((4, n, DH), BF16),
        jax.ShapeDtypeStruct((NSA_GROUPS, n, 128), BF16),
        jax.ShapeDtypeStruct((NSA_GROUPS, n // tm, DH, tm), BF16),
        jax.ShapeDtypeStruct((NSA_GROUPS, n, DH), BF16),
        jax.ShapeDtypeStruct((NSA_GROUPS, n // tm, DH, tm), BF16),
        jax.ShapeDtypeStruct((n, 256), F32),
        jax.ShapeDtypeStruct((n, 256), BF16),
        jax.ShapeDtypeStruct((n, 256), BF16),
        jax.ShapeDtypeStruct((n, 512), BF16),
        jax.ShapeDtypeStruct((n, 256), F32),
        jax.ShapeDtypeStruct((n, 512), BF16),
        jax.ShapeDtypeStruct((n, 2048), BF16),
    )
    out_specs = (
        pl.BlockSpec((NSA_GROUPS, 1, HPG * DH, tm), lambda i: (0, i, 0, 0)),
        pl.BlockSpec((NSA_GROUPS, 1, HPG * DH, tm), lambda i: (0, i, 0, 0)),
        pl.BlockSpec((4, tm, DH), grp),
        pl.BlockSpec((NSA_GROUPS, tm, 128), grp),
        pl.BlockSpec((NSA_GROUPS, 1, DH, tm), lambda i: (0, i, 0, 0)),
        pl.BlockSpec((NSA_GROUPS, tm, DH), grp),
        pl.BlockSpec((NSA_GROUPS, 1, DH, tm), lambda i: (0, i, 0, 0)),
        pl.BlockSpec((tm, 256), row),
        pl.BlockSpec((tm, 256), row), pl.BlockSpec((tm, 256), row), pl.BlockSpec((tm, 512), row),
        pl.BlockSpec((tm, 256), row), pl.BlockSpec((tm, 512), row),
        pl.BlockSpec((tm, 2048), row),
    )
    in_specs = [
        pl.BlockSpec((tm, D_MODEL), row),
        pl.BlockSpec((1, D_MODEL), const),
        pl.BlockSpec((D_MODEL, SEG_END), const),
        pl.BlockSpec((tm, LANE), tab),
        pl.BlockSpec((tm, LANE), tab),
        pl.BlockSpec((LANE, 256), const),
        pl.BlockSpec((1, 256), const),
    ]
    return pl.pallas_call(
        functools.partial(_in_proj_kernel, tm=tm, seq=seq),
        grid=(n // tm,),
        in_specs=in_specs, out_specs=out_specs, out_shape=out_shape,
        compiler_params=pltpu.CompilerParams(
            dimension_semantics=("parallel",), vmem_limit_bytes=VMEM_LIMIT),
        name="in_proj",
    )(x2d, gain, w_all, cos_t, sin_t, w2p, b2)


def _compress_kernel(x_ref, w1_ref, w2_ref, w2t_ref, pe_ref, o_ref, ot_ref, *, n_sub):
    half = CMP_STRIDE * DH
    x = x_ref[...]
    u = _dot(x, w1_ref[0:half, :])
    v = _dot(x, w1_ref[half:2 * half, :])
    c = _dot(pe_ref[...], w1_ref[...])[0:1, :]
    hid = u + pltpu.roll(v, shift=n_sub - 1, axis=0) + c
    hid = (hid * _sigmoid(hid)).astype(BF16)
    out = _dot(hid, w2_ref[...])
    rowi = lax.broadcasted_iota(jnp.int32, out.shape, 0)
    o_ref[...] = jnp.where(rowi < n_sub - 1, out, 0.0).astype(BF16)
    out_t = _dot_nt(w2t_ref[...], hid)
    coli = lax.broadcasted_iota(jnp.int32, out_t.shape, 1)
    ot_ref[...] = jnp.where(coli < n_sub - 1, out_t, 0.0).astype(BF16)


def _compress(src, w1, w2, w2t, pe, *, batch, n_sub):
    return pl.pallas_call(
        functools.partial(_compress_kernel, n_sub=n_sub),
        grid=(4, batch),
        in_specs=[
            pl.BlockSpec((None, None, n_sub, CMP_STRIDE * DH), lambda j, b: (j, b, 0, 0)),
            pl.BlockSpec((None, CMP_LEN * DH, CMP_HIDDEN), lambda j, b: (j // 2, 0, 0)),
            pl.BlockSpec((None, CMP_HIDDEN, DH), lambda j, b: (j // 2, 0, 0)),
            pl.BlockSpec((None, DH, CMP_HIDDEN), lambda j, b: (j // 2, 0, 0)),
            pl.BlockSpec((None, 8, CMP_LEN * DH), lambda j, b: (j // 2, 0, 0)),
        ],
        out_specs=(pl.BlockSpec((None, None, n_sub, DH), lambda j, b: (j, b, 0, 0)),
                   pl.BlockSpec((None, None, DH, n_sub), lambda j, b: (j, b, 0, 0))),
        out_shape=(jax.ShapeDtypeStruct((4, batch, n_sub, DH), BF16),
                   jax.ShapeDtypeStruct((4, batch, DH, n_sub), BF16)),
        compiler_params=pltpu.CompilerParams(
            dimension_semantics=("parallel", "parallel"), vmem_limit_bytes=VMEM_LIMIT),
        name="compress",
    )(src, w1, w2, w2t, pe)


def _softmax_tiles(scores, vts, carry):
    stats = []
    for s, (m, l, _) in zip(scores, carry):
        m_new = jnp.maximum(m, jnp.max(s, axis=0, keepdims=True))
        alpha = jnp.exp2(m - m_new)
        p = jnp.exp2(s - m_new)
        stats.append((m_new, alpha, alpha * l + jnp.sum(p, axis=0, keepdims=True), p.astype(BF16)))
    return tuple((m_new, l_new, alpha * acc + _dot(vt, p))
                 for (m_new, alpha, l_new, p), vt, (_, _, acc) in zip(stats, vts, carry))


def _nsa_kernel(qraw_ref, qrot_ref, kc_ref, vct_ref, ovt_ref, kaug_ref, vst_ref, kw_ref, vwt_ref,
                gate_ref, o_ref, qa_ref, *, tq, topk):
    i = pl.program_id(2)
    s0 = i * tq
    n_cmp = kc_ref.shape[0]
    n_sel = ovt_ref.shape[0]

    def init_carry():
        lane = lax.broadcasted_iota(jnp.int32, (1, tq), 1)
        one = (jnp.where(lane >= 0, NEG, 0.0), jnp.where(lane >= 0, 0.0, 1.0), jnp.zeros((DH, tq), F32))
        return tuple(one for _ in range(HPG))

    kc = kc_ref[...]
    vct = vct_ref[...]
    cend = lax.broadcasted_iota(jnp.int32, (n_cmp, tq), 0) * CMP_STRIDE + (CMP_LEN - 1)
    cmask = cend <= s0 + lax.broadcasted_iota(jnp.int32, (n_cmp, tq), 1)
    scores = [_dot(kc, qraw_ref[h * DH:(h + 1) * DH, :]) for h in range(HPG)]
    psum = jnp.zeros((n_cmp, tq), F32)
    probs = []
    for s in scores:
        s = jnp.where(cmask, s, NEG)
        m = jnp.max(s, axis=0, keepdims=True)
        e = jnp.where(cmask, jnp.exp(s - m), 0.0)
        l = jnp.sum(e, axis=0, keepdims=True)
        p = e * jnp.where(l > 0.0, 1.0 / l, 0.0)
        probs.append(p.astype(BF16))
        psum = psum + p
    oc_t = [_dot(vct, p) for p in probs]

    key_i = lax.broadcasted_iota(jnp.int32, (tq, tq), 0)
    qry_i = lax.broadcasted_iota(jnp.int32, (tq, tq), 1)
    n_back = WINDOW // tq
    w_tiles = [jnp.maximum(i - n_back + u, 0) for u in range(n_back)]
    w_keys = [kw_ref[pl.ds(pl.multiple_of(t * tq, tq), tq), :] for t in w_tiles]
    w_bands = [(i >= n_back - u) & ((qry_i < key_i) if u == 0 else True) for u in range(n_back)]
    w_scores = [jnp.concatenate(
        [jnp.where(band, _dot(kt, qrot_ref[h * DH:(h + 1) * DH, :]), NEG)
         for kt, band in zip(w_keys, w_bands)], axis=0) for h in range(HPG)]
    w_vt = jnp.concatenate([vwt_ref[t] for t in w_tiles], axis=1)
    win_carry = _softmax_tiles(w_scores, [w_vt] * HPG, init_carry())

    ovt = ovt_ref[...]
    p_hi = psum.astype(BF16)
    p_lo = (psum - p_hi.astype(F32)).astype(BF16)
    imp_t = _dot(ovt, p_hi) + _dot(ovt, p_lo)
    jrow = lax.broadcasted_iota(jnp.int32, (n_sel, tq), 0)
    tlane = s0 + lax.broadcasted_iota(jnp.int32, (n_sel, tq), 1)
    cur = tlane // SLC_LEN
    valid = jrow * SLC_LEN <= tlane
    forced = (jrow == 0) | (jrow == cur) | (jrow == cur - 1)
    score = jnp.where(valid, jnp.where(forced, FORCE_SCORE, imp_t), -jnp.inf)

    jrow_f = jrow.astype(F32)
    work = score
    chosen = None
    for _ in range(topk):
        top = jnp.max(work, axis=0, keepdims=True)
        first = jnp.min(jnp.where(work == top, jrow_f, float(n_sel)), axis=0, keepdims=True)
        hit = jrow_f == first
        chosen = hit if chosen is None else chosen | hit
        work = jnp.where(hit, -jnp.inf, work)
    bias_t = jnp.where(chosen & valid, 0.0, NEG).astype(BF16)

    n_half = n_sel // SEL_HALF
    for a in range(n_half):
        for h in range(HPG):
            qa_ref[a, h, 0:DH, :] = qrot_ref[h * DH:(h + 1) * DH, :]
            qa_ref[a, h, DH:DH + SEL_HALF, :] = bias_t[a * SEL_HALF:(a + 1) * SEL_HALF, :]

    tiles_per_half = SEL_HALF * SLC_LEN // tq

    def off_diagonal(lo, scores_fn, vt_ref, group):
        def step(first, count, carry):
            vt = vt_ref[first] if count == 1 else jnp.concatenate(
                [vt_ref[first + u] for u in range(count)], axis=1)
            return _softmax_tiles(scores_fn(first, count), [vt] * HPG, carry)

        n_grouped = (i - lo) // group
        carry = lax.fori_loop(0, n_grouped, lambda t, c: step(lo + t * group, group, c), init_carry())
        return lax.fori_loop(lo + n_grouped * group, i, lambda t, c: step(t, 1, c), carry)

    def sel_scores(first, count):
        kt = kaug_ref[pl.ds(pl.multiple_of(first * tq, tq), count * tq), :]
        half = first // tiles_per_half
        return [_dot(kt, qa_ref[half, h]) for h in range(HPG)]

    sel_carry = off_diagonal(0, sel_scores, vst_ref, SEL_GROUP)
    win_diag = [_dot(kw_ref[pl.ds(pl.multiple_of(s0, tq), tq), :], qa_ref[0, h, 0:DH, :])
                for h in range(HPG)]
    diag = [jnp.where(key_i <= qry_i, s, NEG) for s in sel_scores(i, 1) + win_diag]
    carry = _softmax_tiles(diag, [vst_ref[i]] * HPG + [vwt_ref[i]] * HPG, sel_carry + win_carry)
    os_t = [acc / l for (_, l, acc) in carry[:HPG]]
    ow_t = [acc / l for (_, l, acc) in carry[HPG:]]

    gt_t = gate_ref[...].T
    outs = [gt_t[h:h + 1, :] * oc_t[h] + gt_t[HPG + h:HPG + h + 1, :] * os_t[h]
            + gt_t[2 * HPG + h:2 * HPG + h + 1, :] * ow_t[h] for h in range(HPG)]
    o_ref[...] = jnp.concatenate(outs, axis=0).T.astype(BF16)


def _nsa(qraw_t, qrot_t, cmp_kv, cmp_kv_t, ovt, kaug, vslc_t, kwin, vwin_t, gates, *, batch, seq,
         tq=KEY_TILE):
    n = batch * seq
    nq = seq // tq
    nk = nq
    tk = tq
    n_cmp = cmp_kv.shape[2]
    n_sel = ovt.shape[0]
    topk = min(SLC_TOPK, seq // SLC_LEN)
    qmap = lambda b, g, i: (b * nq + i, g)
    qtmap = lambda b, g, i: (g, b * nq + i, 0, 0)
    kvmap = lambda b, g, i: (g, b, 0)
    vtmap = lambda b, g, i: (g, b, 0, 0)
    return pl.pallas_call(
        functools.partial(_nsa_kernel, tq=tq, topk=topk),
        grid=(batch, NSA_GROUPS, nq),
        in_specs=[
            pl.BlockSpec((None, None, HPG * DH, tq), qtmap),
            pl.BlockSpec((None, None, HPG * DH, tq), qtmap),
            pl.BlockSpec((None, None, n_cmp, DH), lambda b, g, i: (g, b, 0, 0)),
            pl.BlockSpec((None, None, DH, n_cmp), lambda b, g, i: (NSA_GROUPS + g, b, 0, 0)),
            pl.BlockSpec((n_sel, n_cmp), lambda b, g, i: (0, 0)),
            pl.BlockSpec((None, seq, 128), kvmap),
            pl.BlockSpec((None, nk, DH, tk), vtmap),
            pl.BlockSpec((None, seq, DH), kvmap),
            pl.BlockSpec((None, nk, DH, tk), vtmap),
            pl.BlockSpec((tq, LANE), qmap),
        ],
        out_specs=pl.BlockSpec((tq, HPG * DH), qmap),
        out_shape=jax.ShapeDtypeStruct((n, NSA_HEADS * DH), BF16),
        scratch_shapes=[
            pltpu.VMEM((n_sel // SEL_HALF, HPG, DH + SEL_HALF, tq), BF16),
        ],
        compiler_params=pltpu.CompilerParams(
            dimension_semantics=("parallel", "parallel", "arbitrary"),
            vmem_limit_bytes=VMEM_LIMIT),
        name="nsa",
    )(qraw_t, qrot_t, cmp_kv, cmp_kv_t, ovt, kaug, vslc_t, kwin, vwin_t, gates)


def _gla_kernel(q_ref, k_ref, v_ref, g_ref, r_ref, gn_ref, o_ref, st_ref, *, tm, chunk):
    nb = q_ref.shape[0]
    n_chunk = tm // chunk
    n_sub = chunk // GLA_SUB
    dk_all = GLA_HEADS * GLA_DK
    dv_all = GLA_HEADS * GLA_DV

    @pl.when(pl.program_id(0) == 0)
    def _():
        st_ref[...] = jnp.zeros(st_ref.shape, F32)

    def iota(shape, axis):
        return lax.broadcasted_iota(jnp.int32, shape, axis)

    tril = jnp.where((iota((tm, tm), 0) >= iota((tm, tm), 1))
                     & (iota((tm, tm), 0) // chunk == iota((tm, tm), 1) // chunk), 1.0, 0.0).astype(BF16)
    k_head = iota((chunk, dk_all), 1) // GLA_DK
    v_head = iota((chunk, dv_all), 1) // GLA_DV
    a_rows = iota((n_sub * chunk, GLA_HEADS * chunk), 0)
    a_key = iota((n_sub * chunk, GLA_HEADS * chunk), 1) % chunk
    a_keep = (a_key // GLA_SUB == a_rows // chunk) & (a_key <= a_rows % chunk)
    st_keep = iota((dv_all, dk_all), 0) // GLA_DV == iota((dv_all, dk_all), 1) // GLA_DK
    gn = jnp.concatenate([gn_ref[...]] * GLA_HEADS, axis=1)

    work = []
    for b in range(nb):
        g = g_ref[b]
        g_hi = g.astype(BF16)
        g_lo = (g - g_hi.astype(F32)).astype(BF16)
        cum_all = _dot(tril, g_hi) + _dot(tril, g_lo)
        for c in range(n_chunk):
            rows = slice(c * chunk, (c + 1) * chunk)
            cum = cum_all[rows]
            q = q_ref[b, rows, :].astype(F32)
            k = k_ref[b, rows, :].astype(F32)
            v = v_ref[b, rows, :]
            last = cum[chunk - 1:chunk, :]
            refs = [cum[(j + 1) * GLA_SUB - 1:(j + 1) * GLA_SUB, :] for j in range(n_sub)]
            k_hat = jnp.concatenate(
                [k[j * GLA_SUB:(j + 1) * GLA_SUB] * jnp.exp(refs[j] - cum[j * GLA_SUB:(j + 1) * GLA_SUB])
                 for j in range(n_sub)], axis=0)
            q_hat = jnp.concatenate(
                [q * jnp.exp(jnp.minimum(cum - refs[j], 80.0)) for j in range(n_sub)], axis=0)
            k_stack = jnp.concatenate(
                [jnp.where(k_head == h, k_hat, 0.0) for h in range(GLA_HEADS)], axis=0)
            a_all = _dot_nt(q_hat.astype(BF16), k_stack.astype(BF16))
            k_dec = (k * jnp.exp(last - cum)).astype(BF16)
            v_t = v.astype(F32).T.astype(BF16)
            st_inc = _dot(v_t, k_dec)
            work.append(dict(b=b, rows=rows, a_all=a_all, st_inc=st_inc, v=v,
                             q_in=(q * jnp.exp(cum)).astype(BF16), decay=jnp.exp(last)))

    states = [st_ref[b] for b in range(nb)]
    for w in work:
        st = states[w["b"]]
        w["o_inter"] = _dot_nt(w["q_in"], st.astype(BF16))
        states[w["b"]] = st * w["decay"] + jnp.where(st_keep, w["st_inc"], 0.0)
    for b in range(nb):
        st_ref[b] = states[b]

    for w in work:
        a_m = jnp.where(a_keep, w["a_all"], 0.0)
        a = a_m[0:chunk]
        for j in range(1, n_sub):
            a = a + a_m[j * chunk:(j + 1) * chunk]
        v_blocks = jnp.concatenate(
            [jnp.where(v_head == h, w["v"], jnp.zeros_like(w["v"])) for h in range(GLA_HEADS)], axis=0)
        o = w["o_inter"] + _dot(a.astype(BF16), v_blocks)
        scale = jnp.concatenate(
            [jnp.broadcast_to(lax.rsqrt(jnp.mean(
                o[:, h * GLA_DV:(h + 1) * GLA_DV] ** 2, axis=-1, keepdims=True) + EPS), (chunk, GLA_DV))
             for h in range(GLA_HEADS)], axis=1)
        y = o * scale * gn * r_ref[w["b"], w["rows"], :].astype(F32)
        o_ref[w["b"], w["rows"], :] = y.astype(BF16)


def _gla(gq, gk, gv, ga, gr, gnorm, *, batch, seq, tm=KEY_TILE, chunk=GLA_CHUNK):
    tmap = lambda i: (0, i, 0)
    dk_all = GLA_HEADS * GLA_DK
    dv_all = GLA_HEADS * GLA_DV
    return pl.pallas_call(
        functools.partial(_gla_kernel, tm=tm, chunk=chunk),
        grid=(seq // tm,),
        in_specs=[
            pl.BlockSpec((batch, tm, dk_all), tmap),
            pl.BlockSpec((batch, tm, dk_all), tmap),
            pl.BlockSpec((batch, tm, dv_all), tmap),
            pl.BlockSpec((batch, tm, dk_all), tmap),
            pl.BlockSpec((batch, tm, dv_all), tmap),
            pl.BlockSpec((1, GLA_DV), lambda i: (0, 0)),
        ],
        out_specs=pl.BlockSpec((batch, tm, dv_all), tmap),
        out_shape=jax.ShapeDtypeStruct((batch, seq, dv_all), BF16),
        scratch_shapes=[pltpu.VMEM((batch, dv_all, dk_all), F32)],
        compiler_params=pltpu.CompilerParams(
            dimension_semantics=("arbitrary",), vmem_limit_bytes=VMEM_LIMIT),
        name="gla",
    )(gq, gk, gv, ga, gr, gnorm)


def _merge_kernel(x_ref, oa_ref, ob_ref, mg_ref, wn_ref, wg_ref, wo_ref, o_ref):
    up_a = _dot(oa_ref[...], wn_ref[...])
    up_b = _dot(ob_ref[...], wg_ref[...])
    merged = mg_ref[:, 0:D_MODEL].astype(F32) * up_a + mg_ref[:, D_MODEL:2 * D_MODEL].astype(F32) * up_b
    o_ref[...] = x_ref[...] + _dot(merged.astype(BF16), wo_ref[...])


def _merge(x2d, oa, ob, mg, wn, wg, wo, *, tm=512):
    n = x2d.shape[0]
    row = lambda i: (i, 0)
    const = lambda i: (0, 0)
    return pl.pallas_call(
        _merge_kernel,
        grid=(n // tm,),
        in_specs=[
            pl.BlockSpec((tm, D_MODEL), row),
            pl.BlockSpec((tm, 512), row),
            pl.BlockSpec((tm, 512), row),
            pl.BlockSpec((tm, 2 * D_MODEL), row),
            pl.BlockSpec((512, D_MODEL), const),
            pl.BlockSpec((512, D_MODEL), const),
            pl.BlockSpec((D_MODEL, D_MODEL), const),
        ],
        out_specs=pl.BlockSpec((tm, D_MODEL), row),
        out_shape=jax.ShapeDtypeStruct((n, D_MODEL), F32),
        compiler_params=pltpu.CompilerParams(
            dimension_semantics=("parallel",), vmem_limit_bytes=VMEM_LIMIT),
        name="merge_out",
    )(x2d, oa, ob, mg, wn, wg, wo)


def _ffn_kernel(x_ref, g1_ref, wg_ref, wu_ref, wd_ref, g2_ref, o_ref):
    x = x_ref[...]
    ms = jnp.mean(x * x, axis=-1, keepdims=True)
    h = (x * lax.rsqrt(ms + EPS) * g1_ref[...]).astype(BF16)
    a = _dot(h, wg_ref[...])
    u = _dot(h, wu_ref[...])
    act = (a * _sigmoid(a) * u).astype(BF16)
    y = x + _dot(act, wd_ref[...])
    ms2 = jnp.mean(y * y, axis=-1, keepdims=True)
    o_ref[...] = y * lax.rsqrt(ms2 + EPS) * g2_ref[...]


def _ffn(x1, g1, wg, wu, wd, g2, *, tm=256):
    n = x1.shape[0]
    row = lambda i: (i, 0)
    const = lambda i: (0, 0)
    return pl.pallas_call(
        _ffn_kernel,
        grid=(n // tm,),
        in_specs=[
            pl.BlockSpec((tm, D_MODEL), row),
            pl.BlockSpec((1, D_MODEL), const),
            pl.BlockSpec((D_MODEL, D_FF), const),
            pl.BlockSpec((D_MODEL, D_FF), const),
            pl.BlockSpec((D_FF, D_MODEL), const),
            pl.BlockSpec((1, D_MODEL), const),
        ],
        out_specs=pl.BlockSpec((tm, D_MODEL), row),
        out_shape=jax.ShapeDtypeStruct((n, D_MODEL), F32),
        compiler_params=pltpu.CompilerParams(
            dimension_semantics=("parallel",), vmem_limit_bytes=VMEM_LIMIT),
        name="ffn",
    )(x1, g1, wg, wu, wd, g2)


def _rope_swap(w, n_heads):
    w3 = w.reshape(w.shape[0], n_heads, DH)
    half = ROPE_DIM // 2
    sw = jnp.concatenate([-w3[:, :, half:ROPE_DIM], w3[:, :, 0:half],
                          jnp.zeros_like(w3[:, :, ROPE_DIM:])], axis=2)
    return sw.reshape(w.shape)


def _rope_tables(seq):
    half = ROPE_DIM // 2
    inv_freq = ROPE_THETA ** (-jnp.arange(half, dtype=F32) / half)
    ang = jnp.arange(seq).astype(F32)[:, None] * inv_freq[None, :]
    cos = jnp.cos(ang)
    sin = jnp.sin(ang)
    ones = jnp.ones((seq, DH - ROPE_DIM), F32)
    cos64 = jnp.concatenate([cos, cos, ones], axis=1)
    sin64 = jnp.concatenate([sin, sin, 0.0 * ones], axis=1)
    return jnp.concatenate([cos64, cos64], axis=1), jnp.concatenate([sin64, sin64], axis=1)


def _overlap_t(n_sel, n_cmp_pad, n_slc, n_cmp):
    c_start = np.arange(n_cmp_pad)[None, :] * CMP_STRIDE
    s_start = np.arange(n_sel)[:, None] * SLC_LEN
    ov = (c_start < s_start + SLC_LEN) & (c_start + CMP_LEN > s_start)
    ov &= (np.arange(n_cmp_pad)[None, :] < n_cmp) & (np.arange(n_sel)[:, None] < n_slc)
    return jnp.asarray(ov, dtype=BF16)


def _pack_in_weights(w_in):
    c = 0
    wq = w_in[:, c:c + 512]; c += 512
    wkv = w_in[:, c:c + 768]; c += 768
    wgate = w_in[:, c:c + 24]; c += 24
    wgq = w_in[:, c:c + 256]; c += 256
    wgk = w_in[:, c:c + 256]; c += 256
    wgv = w_in[:, c:c + 512]; c += 512
    wlr = w_in[:, c:c + GLA_RANK]; c += GLA_RANK
    wgr = w_in[:, c:c + 512]; c += 512
    wmg = w_in[:, c:c + 2048]
    d = w_in.shape[0]
    wk_slc = wkv[:, 256:384]
    wk_win = wkv[:, 512:640]
    wg3 = wgate.reshape(d, NSA_GROUPS, HPG, 3).transpose(0, 1, 3, 2).reshape(d, NSA_GROUPS, 3 * HPG)
    wg_pad = jnp.pad(wg3, ((0, 0), (0, 0), (0, LANE - 3 * HPG))).reshape(d, NSA_GROUPS * LANE)
    wlr_pad = jnp.pad(wlr, ((0, 0), (0, LANE - GLA_RANK)))
    w_all = jnp.concatenate([
        wq, _rope_swap(wq, NSA_HEADS), wkv,
        _rope_swap(wk_slc, NSA_GROUPS), _rope_swap(wk_win, NSA_GROUPS),
        wg_pad, wgq, wgk, wgv, wgr, wlr_pad, wmg], axis=1)
    return w_all.astype(BF16)


def kernel(x, norm_mix, w_in, cmp_pe_k, cmp_pe_v, cmp_k_w1, cmp_k_w2, cmp_v_w1, cmp_v_w2,
           gla_gate_w2, gla_gate_b, gla_norm, w_up_nsa, w_up_gla, w_out, norm_ffn,
           w_ffn_gate, w_ffn_up, w_ffn_down, norm_final):
    batch, seq, d = x.shape
    assert d == D_MODEL and w_in.shape[0] == 1
    assert seq % KEY_TILE == 0
    n = batch * seq
    x2d = x.reshape(n, d)

    w_all = _pack_in_weights(w_in[0])
    assert w_all.shape[1] == SEG_END
    cos_t, sin_t = _rope_tables(seq)
    w2p = jnp.pad(gla_gate_w2[0], ((0, LANE - GLA_RANK), (0, 0))).astype(BF16)
    b2 = gla_gate_b[0].reshape(1, -1)

    (qraw, qrot, cmp_src, kaug, vslc_t, kwin, vwin_t, gates, gq, gk, gv, ga, gr, mg) = _in_proj(
        x2d, norm_mix[0].reshape(1, d), w_all, cos_t, sin_t, w2p, b2, seq=seq)

    n_sub = seq // CMP_STRIDE
    n_cmp = n_sub - CMP_LEN // CMP_STRIDE + 1
    n_slc = seq // SLC_LEN
    cmp_in = cmp_src.reshape(4, batch, n_sub, CMP_STRIDE * DH)
    w1 = jnp.stack([cmp_k_w1[0], cmp_v_w1[0]]).astype(BF16)
    w2 = jnp.stack([cmp_k_w2[0], cmp_v_w2[0]]).astype(BF16)
    pe = jnp.stack([cmp_pe_k[0].reshape(1, -1), cmp_pe_v[0].reshape(1, -1)])
    pe = jnp.broadcast_to(pe, (2, 8, CMP_LEN * DH)).astype(BF16)
    w2t = jnp.stack([cmp_k_w2[0].T, cmp_v_w2[0].T]).astype(BF16)
    cmp_kv, cmp_kv_t = _compress(cmp_in, w1, w2, w2t, pe, batch=batch, n_sub=n_sub)

    n_sel = -(-n_slc // SEL_HALF) * SEL_HALF
    ovt = _overlap_t(n_sel, n_sub, n_slc, n_cmp)
    o_a = _nsa(qraw, qrot, cmp_kv, cmp_kv_t, ovt, kaug, vslc_t, kwin, vwin_t, gates,
               batch=batch, seq=seq)

    per_seq = lambda a: a.reshape(batch, seq, a.shape[-1])
    o_b = _gla(per_seq(gq), per_seq(gk), per_seq(gv), per_seq(ga), per_seq(gr),
               gla_norm[0].reshape(1, -1), batch=batch, seq=seq).reshape(n, -1)

    x1 = _merge(x2d, o_a, o_b, mg, w_up_nsa[0].astype(BF16), w_up_gla[0].astype(BF16),
                w_out[0].astype(BF16))
    out = _ffn(x1, norm_ffn[0].reshape(1, d), w_ffn_gate[0].astype(BF16), w_ffn_up[0].astype(BF16),
               w_ffn_down[0].astype(BF16), norm_final.reshape(1, d))
    return out.reshape(batch, seq, d)
```

```python
import functools

import numpy as np
import jax
import jax.numpy as jnp
from jax import lax
from jax.experimental import pallas as pl
from jax.experimental.pallas import tpu as pltpu

F32 = jnp.float32
BF16 = jnp.bfloat16

D_MODEL = 1024
NSA_HEADS = 8
NSA_GROUPS = 2
HPG = NSA_HEADS // NSA_GROUPS
DH = 64
CMP_LEN = 32
CMP_STRIDE = 16
CMP_HIDDEN = 128
SLC_LEN = 64
SLC_TOPK = 16
WINDOW = 512
FORCE_SCORE = 1.0e4
GLA_HEADS = 4
GLA_DK = 64
GLA_DV = 128
GLA_RANK = 16
GLA_TAU = 16.0
GLA_CHUNK = 64
GLA_SUB = 16
ROPE_THETA = 500000.0
ROPE_DIM = DH // 4
D_FF = 2816
EPS = 1e-6
NEG = -1e30
LOG2E = 1.4426950408889634
KEY_TILE = 256

LANE = 128
VMEM_LIMIT = 56 * 1024 * 1024
SEL_HALF = 64
V_ROWS = DH + 16
SEL_GROUP = 4

SEG_Q = 0
SEG_QSW = SEG_Q + 512
SEG_KV = SEG_QSW + 512
SEG_KSW = SEG_KV + 768
SEG_GATE = SEG_KSW + 256
SEG_GLA = SEG_GATE + 256
SEG_LR = SEG_GLA + 1536
SEG_MERGE = SEG_LR + 128
SEG_END = SEG_MERGE + 2048


def _dot(a, b):
    return jnp.dot(a, b, preferred_element_type=F32)


def _dot_nt(a, b):
    return lax.dot_general(a, b, (((1,), (1,)), ((), ())), preferred_element_type=F32)


def _sigmoid(x):
    return 1.0 / (1.0 + jnp.exp(-x))


def _in_proj_kernel(x_ref, gain_ref, w_ref, cos_ref, sin_ref, w2_ref, b2_ref,
                    qraw_ref, qrot_ref, cmp_ref, kaug_ref, vslc_ref, kwin_ref, vwin_ref,
                    gate_ref, gq_ref, gk_ref, gv_ref, ga_ref, gr_ref, mg_ref, *, tm, seq):
    i = pl.program_id(0)
    x = x_ref[...]
    ms = jnp.mean(x * x, axis=-1, keepdims=True)
    h = (x * lax.rsqrt(ms + EPS) * gain_ref[...]).astype(BF16)

    def proj(a, b):
        return _dot(h, w_ref[:, a:b])

    cos = cos_ref[...]
    sin = sin_ref[...]
    cos4 = jnp.concatenate([cos] * 4, axis=1)
    sin4 = jnp.concatenate([sin] * 4, axis=1)

    scale = DH ** -0.5
    q = proj(SEG_Q, SEG_Q + 512)
    qsw = proj(SEG_QSW, SEG_QSW + 512)
    qraw_t = (q * (scale * LOG2E)).T
    qrot_t = ((q * cos4 + qsw * sin4) * (scale * LOG2E)).T
    gw = HPG * DH
    for g in range(NSA_GROUPS):
        qraw_ref[g, 0] = qraw_t[g * gw:(g + 1) * gw, :].astype(BF16)
        qrot_ref[g, 0] = qrot_t[g * gw:(g + 1) * gw, :].astype(BF16)

    kv = proj(SEG_KV, SEG_KV + 768)
    ksw = proj(SEG_KSW, SEG_KSW + 256)
    for j in range(2):
        for g in range(NSA_GROUPS):
            cmp_ref[2 * j + g] = kv[:, j * 128 + g * DH:j * 128 + (g + 1) * DH].astype(BF16)
    kslc = kv[:, 256:384] * cos + ksw[:, 0:128] * sin
    kwin = kv[:, 512:640] * cos + ksw[:, 128:256] * sin
    vslc_t = kv[:, 384:512].T
    vwin_t = kv[:, 640:768].T
    pos = (i * tm) % seq + lax.broadcasted_iota(jnp.int32, (tm, SEL_HALF), 0)
    blk = (pos // SLC_LEN) % SEL_HALF
    onehot = jnp.where(lax.broadcasted_iota(jnp.int32, (tm, SEL_HALF), 1) == blk, 1.0, 0.0)
    ones_rows = jnp.ones((V_ROWS - DH, tm), F32)
    for g in range(NSA_GROUPS):
        sl = slice(g * DH, (g + 1) * DH)
        kaug_ref[g] =jnp.concatenate([kslc[:, sl], onehot], axis=1).astype(BF16)
        vslc_ref[g, 0] = jnp.concatenate([vslc_t[sl, :], ones_rows], axis=0).astype(BF16)
        kwin_ref[g] = kwin[:, sl].astype(BF16)
        vwin_ref[g, 0] = jnp.concatenate([vwin_t[sl, :], ones_rows], axis=0).astype(BF16)

    gate_ref[...] = _sigmoid(proj(SEG_GATE, SEG_GATE + 256))

    gla = proj(SEG_GLA, SEG_GLA + 1536)
    gq_ref[...] = (gla[:, 0:256] * (GLA_DK ** -0.5)).astype(BF16)
    gk_ref[...] = gla[:, 256:512].astype(BF16)
    gv_ref[...] = gla[:, 512:1024].astype(BF16)
    r = gla[:, 1024:1536]
    gr_ref[...] = (r * _sigmoid(r)).astype(BF16)

    lr = proj(SEG_LR, SEG_LR + 128)
    z = _dot(lr.astype(BF16), w2_ref[...]) + b2_ref[...]
    log_sig = jnp.minimum(z, 0.0) - jnp.log1p(jnp.exp(-jnp.abs(z)))
    ga_ref[...] = log_sig * (1.0 / GLA_TAU)

    mg_ref[...] = _sigmoid(proj(SEG_MERGE, SEG_MERGE + 2048)).astype(BF16)


def _in_proj(x2d, gain, w_all, cos_t, sin_t, w2p, b2, *, seq, tm=KEY_TILE):
    n = x2d.shape[0]
    nt = seq // tm
    row = lambda i: (i, 0)
    grp = lambda i: (0, i, 0)
    const = lambda i: (0, 0)
    tab = lambda i: (i % nt, 0)
    out_shape = (
        jax.ShapeDtypeStruct((NSA_GROUPS, n // tm, HPG * DH, tm), BF16),
        jax.ShapeDtypeStruct((NSA_GROUPS, n // tm, HPG * DH, tm), BF16),
        jax.ShapeDtypeStruct((4, n, DH), BF16),
        jax.ShapeDtypeStruct((NSA_GROUPS, n, 128), BF16),
        jax.ShapeDtypeStruct((NSA_GROUPS, n // tm, V_ROWS, tm), BF16),
        jax.ShapeDtypeStruct((NSA_GROUPS, n, DH), BF16),
        jax.ShapeDtypeStruct((NSA_GROUPS, n // tm, V_ROWS, tm), BF16),
        jax.ShapeDtypeStruct((n, 256), F32),
        jax.ShapeDtypeStruct((n, 256), BF16),
        jax.ShapeDtypeStruct((n, 256), BF16),
        jax.ShapeDtypeStruct((n, 512), BF16),
        jax.ShapeDtypeStruct((n, 256), F32),
        jax.ShapeDtypeStruct((n, 512), BF16),
        jax.ShapeDtypeStruct((n, 2048), BF16),
    )
    out_specs = (
        pl.BlockSpec((NSA_GROUPS, 1, HPG * DH, tm), lambda i: (0, i, 0, 0)),
        pl.BlockSpec((NSA_GROUPS, 1, HPG * DH, tm), lambda i: (0, i, 0, 0)),
        pl.BlockSpec((4, tm, DH), grp),
        pl.BlockSpec((NSA_GROUPS, tm, 128), grp),
        pl.BlockSpec((NSA_GROUPS, 1, V_ROWS, tm), lambda i: (0, i, 0, 0)),
        pl.BlockSpec((NSA_GROUPS, tm, DH), grp),
        pl.BlockSpec((NSA_GROUPS, 1, V_ROWS, tm), lambda i: (0, i, 0, 0)),
        pl.BlockSpec((tm, 256), row),
        pl.BlockSpec((tm, 256), row), pl.BlockSpec((tm, 256), row), pl.BlockSpec((tm, 512), row),
        pl.BlockSpec((tm, 256), row), pl.BlockSpec((tm, 512), row),
        pl.BlockSpec((tm, 2048), row),
    )
    in_specs = [
        pl.BlockSpec((tm, D_MODEL), row),
        pl.BlockSpec((1, D_MODEL), const),
        pl.BlockSpec((D_MODEL, SEG_END), const),
        pl.BlockSpec((tm, LANE), tab),
        pl.BlockSpec((tm, LANE), tab),
        pl.BlockSpec((LANE, 256), const),
        pl.BlockSpec((1, 256), const),
    ]
    return pl.pallas_call(
        functools.partial(_in_proj_kernel, tm=tm, seq=seq),
        grid=(n // tm,),
        in_specs=in_specs, out_specs=out_specs, out_shape=out_shape,
        compiler_params=pltpu.CompilerParams(
            dimension_semantics=("parallel",), vmem_limit_bytes=VMEM_LIMIT),
        name="in_proj",
    )(x2d, gain, w_all, cos_t, sin_t, w2p, b2)


def _compress_kernel(x_ref, w1_ref, w2_ref, w2t_ref, pe_ref, o_ref, ot_ref, *, n_sub):
    half = CMP_STRIDE * DH
    x = x_ref[...]
    u = _dot(x, w1_ref[0:half, :])
    v = _dot(x, w1_ref[half:2 * half, :])
    c = _dot(pe_ref[...], w1_ref[...])[0:1, :]
    hid = u + pltpu.roll(v, shift=n_sub - 1, axis=0) + c
    hid = (hid * _sigmoid(hid)).astype(BF16)
    out = _dot(hid, w2_ref[...])
    rowi = lax.broadcasted_iota(jnp.int32, out.shape, 0)
    o_ref[...] = jnp.where(rowi < n_sub - 1, out, 0.0).astype(BF16)
    out_t = _dot_nt(w2t_ref[...], hid)
    coli = lax.broadcasted_iota(jnp.int32, out_t.shape, 1)
    ot_ref[0:DH, :] = jnp.where(coli < n_sub - 1, out_t, 0.0).astype(BF16)
    ot_ref[DH:V_ROWS, :] = jnp.ones((V_ROWS - DH, n_sub), BF16)


def _compress(src, w1, w2, w2t, pe, *, batch, n_sub):
    return pl.pallas_call(
        functools.partial(_compress_kernel, n_sub=n_sub),
        grid=(4, batch),
        in_specs=[
            pl.BlockSpec((None, None, n_sub, CMP_STRIDE * DH), lambda j, b: (j, b, 0, 0)),
            pl.BlockSpec((None, CMP_LEN * DH, CMP_HIDDEN), lambda j, b: (j // 2, 0, 0)),
            pl.BlockSpec((None, CMP_HIDDEN, DH), lambda j, b: (j // 2, 0, 0)),
            pl.BlockSpec((None, DH, CMP_HIDDEN), lambda j, b: (j // 2, 0, 0)),
            pl.BlockSpec((None, 8, CMP_LEN * DH), lambda j, b: (j // 2, 0, 0)),
        ],
        out_specs=(pl.BlockSpec((None, None, n_sub, DH), lambda j, b: (j, b, 0, 0)),
                   pl.BlockSpec((None, None, V_ROWS, n_sub), lambda j, b: (j, b, 0, 0))),
        out_shape=(jax.ShapeDtypeStruct((4, batch, n_sub, DH), BF16),
                   jax.ShapeDtypeStruct((4, batch, V_ROWS, n_sub), BF16)),
        compiler_params=pltpu.CompilerParams(
            dimension_semantics=("parallel", "parallel"), vmem_limit_bytes=VMEM_LIMIT),
        name="compress",
    )(src, w1, w2, w2t, pe)


def _softmax_tiles(scores, vts, carry):
    stats = []
    for s, (m, _) in zip(scores, carry):
        m_new = jnp.maximum(m, jnp.max(s, axis=0, keepdims=True))
        stats.append((m_new, jnp.exp2(m - m_new), jnp.exp2((s - m_new).astype(BF16))))
    return tuple((m_new, alpha * acc + _dot(vt, p))
                 for (m_new, alpha, p), vt, (_, acc) in zip(stats, vts, carry))


def _nsa_kernel(qraw_ref, qrot_ref, kc_ref, vct_ref, ovt_ref, kaug_ref, vst_ref, kw_ref, vwt_ref,
                gate_ref, o_ref, qa_ref, *, tq, topk):
    i = pl.program_id(2)
    s0 = i * tq
    n_cmp = kc_ref.shape[0]
    n_sel = ovt_ref.shape[0]

    def init_carry():
        lane = lax.broadcasted_iota(jnp.int32, (1, tq), 1)
        one = (jnp.where(lane >= 0, NEG, 0.0), jnp.zeros((V_ROWS, tq), F32))
        return tuple(one for _ in range(HPG))

    def normalised(acc):
        return acc[0:DH] / acc[DH:DH + 1]

    kc = kc_ref[...]
    vct = vct_ref[...]
    cend = lax.broadcasted_iota(jnp.int32, (n_cmp, tq), 0) * CMP_STRIDE + (CMP_LEN - 1)
    cmask = cend <= s0 + lax.broadcasted_iota(jnp.int32, (n_cmp, tq), 1)
    scores = [_dot(kc, qraw_ref[h * DH:(h + 1) * DH, :]) for h in range(HPG)]
    expd, seen = [], []
    for s in scores:
        s = jnp.where(cmask, s, NEG)
        m = jnp.max(s, axis=0, keepdims=True)
        expd.append(jnp.exp2((s - m).astype(BF16)))
        seen.append(m > 0.5 * NEG)
    ovt = ovt_ref[...]
    oc_t, imp_t = [], 0.0
    for e, ok in zip(expd, seen):
        acc = _dot(vct, e)
        r = jnp.where(ok, 1.0 / acc[DH:DH + 1], 0.0)
        oc_t.append(acc[0:DH] * r)
        imp_t = imp_t + _dot(ovt, e) * r

    key_i = lax.broadcasted_iota(jnp.int32, (tq, tq), 0)
    qry_i = lax.broadcasted_iota(jnp.int32, (tq, tq), 1)
    n_back = WINDOW // tq
    w_tiles = [jnp.maximum(i - n_back + u, 0) for u in range(n_back)]
    w_keys = [kw_ref[pl.ds(pl.multiple_of(t * tq, tq), tq), :] for t in w_tiles]
    w_bands = [(i >= n_back - u) & ((qry_i < key_i) if u == 0 else True) for u in range(n_back)]
    w_scores = [jnp.concatenate(
        [jnp.where(band, _dot(kt, qrot_ref[h * DH:(h + 1) * DH, :]), NEG)
         for kt, band in zip(w_keys, w_bands)], axis=0) for h in range(HPG)]
    w_vt = jnp.concatenate([vwt_ref[t] for t in w_tiles], axis=1)
    win_carry = _softmax_tiles(w_scores, [w_vt] * HPG, init_carry())

    jrow =lax.broadcasted_iota(jnp.int32, (n_sel, tq), 0)
    tlane = s0 + lax.broadcasted_iota(jnp.int32, (n_sel, tq), 1)
    cur = tlane // SLC_LEN
    valid = jrow * SLC_LEN <= tlane
    forced = (jrow == 0) | (jrow == cur) | (jrow == cur - 1)
    score = jnp.where(valid, jnp.where(forced, FORCE_SCORE, imp_t), -jnp.inf)

    jrow_f = jrow.astype(F32)
    work = score
    for _ in range(topk):
        top = jnp.max(work, axis=0, keepdims=True)
        first = jnp.min(jnp.where(work == top, jrow_f, float(n_sel)), axis=0, keepdims=True)
        work = jnp.where(jrow_f == first, -jnp.inf, work)
    bias_t = jnp.where(work != score, 0.0, NEG).astype(BF16)

    n_half = n_sel // SEL_HALF
    for a in range(n_half):
        for h in range(HPG):
            qa_ref[a, h, 0:DH, :] = qrot_ref[h * DH:(h + 1) * DH, :]
            qa_ref[a, h, DH:DH + SEL_HALF, :] = bias_t[a * SEL_HALF:(a + 1) * SEL_HALF, :]

    tiles_per_half = SEL_HALF * SLC_LEN // tq

    def off_diagonal(lo, scores_fn, vt_ref, group):
        def step(first, count, carry):
            vt = vt_ref[first] if count == 1 else jnp.concatenate(
                [vt_ref[first + u] for u in range(count)], axis=1)
            return _softmax_tiles(scores_fn(first, count), [vt] * HPG, carry)

        n_grouped = (i - lo) // group
        carry = lax.fori_loop(0, n_grouped, lambda t, c: step(lo + t * group, group, c), init_carry())
        return lax.fori_loop(lo + n_grouped * group, i, lambda t, c: step(t, 1, c), carry)

    def sel_scores(first, count):
        kt = kaug_ref[pl.ds(pl.multiple_of(first * tq, tq), count * tq), :]
        half = first // tiles_per_half
        return [_dot(kt, qa_ref[half, h]) for h in range(HPG)]

    sel_carry = off_diagonal(0, sel_scores, vst_ref, SEL_GROUP)
    win_diag = [_dot(kw_ref[pl.ds(pl.multiple_of(s0, tq), tq), :], qa_ref[0, h, 0:DH, :])
                for h in range(HPG)]
    diag = [jnp.where(key_i <= qry_i, s, NEG) for s in sel_scores(i, 1) + win_diag]
    carry = _softmax_tiles(diag, [vst_ref[i]] * HPG + [vwt_ref[i]] * HPG, sel_carry + win_carry)
    os_t = [normalised(acc) for (_, acc) in carry[:HPG]]
    ow_t = [normalised(acc) for (_, acc) in carry[HPG:]]

    gt_t = gate_ref[...].T
    outs = [gt_t[h:h + 1, :] * oc_t[h] + gt_t[HPG + h:HPG + h + 1, :] * os_t[h]
            + gt_t[2 * HPG + h:2 * HPG + h + 1, :] * ow_t[h] for h in range(HPG)]
    o_ref[...] = jnp.concatenate(outs, axis=0).T.astype(BF16)


def _nsa(qraw_t, qrot_t, cmp_kv, cmp_kv_t, ovt, kaug, vslc_t, kwin, vwin_t, gates, *, batch, seq,
         tq=KEY_TILE):
    n = batch * seq
    nq = seq // tq
    nk = nq
    tk = tq
    n_cmp = cmp_kv.shape[2]
    n_sel = ovt.shape[0]
    topk = min(SLC_TOPK, seq // SLC_LEN)
    qmap = lambda b, g, i: (b * nq + i, g)
    qtmap = lambda b, g, i: (g, b * nq + i, 0, 0)
    kvmap = lambda b, g, i: (g, b, 0)
    vtmap = lambda b, g, i: (g, b, 0, 0)
    return pl.pallas_call(
        functools.partial(_nsa_kernel, tq=tq, topk=topk),
        grid=(batch, NSA_GROUPS, nq),
        in_specs=[
            pl.BlockSpec((None, None, HPG * DH, tq), qtmap),
            pl.BlockSpec((None, None, HPG * DH, tq), qtmap),
            pl.BlockSpec((None, None, n_cmp, DH), lambda b, g, i: (g, b, 0, 0)),
            pl.BlockSpec((None, None, V_ROWS, n_cmp), lambda b, g, i: (NSA_GROUPS + g, b, 0, 0)),
            pl.BlockSpec((n_sel, n_cmp), lambda b, g, i: (0, 0)),
            pl.BlockSpec((None, seq, 128), kvmap),
            pl.BlockSpec((None, nk, V_ROWS, tk), vtmap),
            pl.BlockSpec((None, seq, DH), kvmap),
            pl.BlockSpec((None, nk, V_ROWS, tk), vtmap),
            pl.BlockSpec((tq, LANE), qmap),
        ],
        out_specs=pl.BlockSpec((tq, HPG * DH), qmap),
        out_shape=jax.ShapeDtypeStruct((n, NSA_HEADS * DH), BF16),
        scratch_shapes=[
            pltpu.VMEM((n_sel // SEL_HALF, HPG, DH + SEL_HALF, tq), BF16),
        ],
        compiler_params=pltpu.CompilerParams(
            dimension_semantics=("parallel", "parallel", "arbitrary"),
            vmem_limit_bytes=VMEM_LIMIT),
        name="nsa",
    )(qraw_t, qrot_t, cmp_kv, cmp_kv_t, ovt, kaug, vslc_t, kwin, vwin_t, gates)


def _gla_kernel(q_ref, k_ref, v_ref, g_ref, r_ref, gn_ref, o_ref, st_ref, *, tm, chunk):
    nb = q_ref.shape[0]
    n_chunk = tm // chunk
    n_sub = chunk // GLA_SUB
    dk_all = GLA_HEADS * GLA_DK
    dv_all = GLA_HEADS * GLA_DV

    @pl.when(pl.program_id(0) == 0)
    def _():
        st_ref[...] = jnp.zeros(st_ref.shape, F32)

    def iota(shape, axis):
        return lax.broadcasted_iota(jnp.int32, shape, axis)

    tril = jnp.where((iota((tm, tm), 0) >= iota((tm, tm), 1))
                     & (iota((tm, tm), 0) // chunk == iota((tm, tm), 1) // chunk), 1.0, 0.0).astype(BF16)
    k_head = iota((chunk, dk_all), 1) // GLA_DK
    v_head = iota((chunk, dv_all), 1) // GLA_DV
    a_rows = iota((n_sub * chunk, GLA_HEADS * chunk), 0)
    a_key = iota((n_sub * chunk, GLA_HEADS * chunk), 1) % chunk
    a_keep = (a_key // GLA_SUB == a_rows // chunk) & (a_key <= a_rows % chunk)
    st_keep = iota((dv_all, dk_all), 0) // GLA_DV == iota((dv_all, dk_all), 1) // GLA_DK
    gn = jnp.concatenate([gn_ref[...]] * GLA_HEADS, axis=1)

    work = []
    for b in range(nb):
        g = g_ref[b]
        g_hi = g.astype(BF16)
        g_lo = (g - g_hi.astype(F32)).astype(BF16)
        cum_all = _dot(tril, g_hi) + _dot(tril, g_lo)
        for c in range(n_chunk):
            rows = slice(c * chunk, (c + 1) * chunk)
            cum = cum_all[rows]
            q = q_ref[b, rows, :].astype(F32)
            k = k_ref[b, rows, :].astype(F32)
            v = v_ref[b, rows, :]
            last = cum[chunk - 1:chunk, :]
            refs = [cum[(j + 1) * GLA_SUB - 1:(j + 1) * GLA_SUB, :] for j in range(n_sub)]
            k_hat = jnp.concatenate(
                [k[j * GLA_SUB:(j + 1) * GLA_SUB] * jnp.exp(refs[j] - cum[j * GLA_SUB:(j + 1) * GLA_SUB])
                 for j in range(n_sub)], axis=0)
            q_hat = jnp.concatenate(
                [q * jnp.exp(jnp.minimum(cum - refs[j], 80.0)) for j in range(n_sub)], axis=0)
            k_stack = jnp.concatenate(
                [jnp.where(k_head == h, k_hat, 0.0) for h in range(GLA_HEADS)], axis=0)
            a_all = _dot_nt(q_hat.astype(BF16), k_stack.astype(BF16))
            k_dec = (k * jnp.exp(last - cum)).astype(BF16)
            v_t = v.astype(F32).T.astype(BF16)
            st_inc = _dot(v_t, k_dec)
            work.append(dict(b=b, rows=rows, a_all=a_all, st_inc=st_inc, v=v,
                             q_in=(q * jnp.exp(cum)).astype(BF16), decay=jnp.exp(last)))

    states = [st_ref[b] for b in range(nb)]
    for w in work:
        st = states[w["b"]]
        w["o_inter"] = _dot_nt(w["q_in"], st.astype(BF16))
        states[w["b"]] = st * w["decay"] + jnp.where(st_keep, w["st_inc"], 0.0)
    for b in range(nb):
        st_ref[b] = states[b]

    for w in work:
        a_m = jnp.where(a_keep, w["a_all"], 0.0)
        a = a_m[0:chunk]
        for j in range(1, n_sub):
            a = a + a_m[j * chunk:(j + 1) * chunk]
        v_blocks = jnp.concatenate(
            [jnp.where(v_head == h, w["v"], jnp.zeros_like(w["v"])) for h in range(GLA_HEADS)], axis=0)
        o = w["o_inter"] + _dot(a.astype(BF16), v_blocks)
        scale = jnp.concatenate(
            [jnp.broadcast_to(lax.rsqrt(jnp.mean(
                o[:, h * GLA_DV:(h + 1) * GLA_DV] ** 2, axis=-1, keepdims=True) + EPS), (chunk, GLA_DV))
             for h in range(GLA_HEADS)], axis=1)
        y = o * scale * gn * r_ref[w["b"], w["rows"], :].astype(F32)
        o_ref[w["b"], w["rows"], :] = y.astype(BF16)


def _gla(gq, gk, gv, ga, gr, gnorm, *, batch, seq, tm=KEY_TILE, chunk=GLA_CHUNK):
    tmap = lambda i: (0, i, 0)
    dk_all = GLA_HEADS * GLA_DK
    dv_all = GLA_HEADS * GLA_DV
    return pl.pallas_call(
        functools.partial(_gla_kernel, tm=tm, chunk=chunk),
        grid=(seq // tm,),
        in_specs=[
            pl.BlockSpec((batch, tm, dk_all), tmap),
            pl.BlockSpec((batch, tm, dk_all), tmap),
            pl.BlockSpec((batch, tm, dv_all), tmap),
            pl.BlockSpec((batch, tm, dk_all), tmap),
            pl.BlockSpec((batch, tm, dv_all), tmap),
            pl.BlockSpec((1, GLA_DV), lambda i: (0, 0)),
        ],
        out_specs=pl.BlockSpec((batch, tm, dv_all), tmap),
        out_shape=jax.ShapeDtypeStruct((batch, seq, dv_all), BF16),
        scratch_shapes=[pltpu.VMEM((batch, dv_all, dk_all), F32)],
        compiler_params=pltpu.CompilerParams(
            dimension_semantics=("arbitrary",), vmem_limit_bytes=VMEM_LIMIT),
        name="gla",
    )(gq, gk, gv, ga, gr, gnorm)


def _merge_kernel(x_ref, oa_ref, ob_ref, mg_ref, wn_ref, wg_ref, wo_ref, o_ref):
    up_a = _dot(oa_ref[...], wn_ref[...])
    up_b = _dot(ob_ref[...], wg_ref[...])
    merged = mg_ref[:, 0:D_MODEL].astype(F32) * up_a + mg_ref[:, D_MODEL:2 * D_MODEL].astype(F32) * up_b
    o_ref[...] = x_ref[...] + _dot(merged.astype(BF16), wo_ref[...])


def _merge(x2d, oa, ob, mg, wn, wg, wo, *, tm=512):
    n = x2d.shape[0]
    row = lambda i: (i, 0)
    const = lambda i: (0, 0)
    return pl.pallas_call(
        _merge_kernel,
        grid=(n // tm,),
        in_specs=[
            pl.BlockSpec((tm, D_MODEL), row),
            pl.BlockSpec((tm, 512), row),
            pl.BlockSpec((tm, 512), row),
            pl.BlockSpec((tm, 2 * D_MODEL), row),
            pl.BlockSpec((512, D_MODEL), const),
            pl.BlockSpec((512, D_MODEL), const),
            pl.BlockSpec((D_MODEL, D_MODEL), const),
        ],
        out_specs=pl.BlockSpec((tm, D_MODEL), row),
        out_shape=jax.ShapeDtypeStruct((n, D_MODEL), F32),
        compiler_params=pltpu.CompilerParams(
            dimension_semantics=("parallel",), vmem_limit_bytes=VMEM_LIMIT),
        name="merge_out",
    )(x2d, oa, ob, mg, wn, wg, wo)


def _ffn_kernel(x_ref, g1_ref, wg_ref, wu_ref, wd_ref, g2_ref, o_ref):
    x = x_ref[...]
    ms = jnp.mean(x * x, axis=-1, keepdims=True)
    h = (x * lax.rsqrt(ms + EPS) * g1_ref[...]).astype(BF16)
    a = _dot(h, wg_ref[...])
    u = _dot(h, wu_ref[...])
    act = (a * _sigmoid(a) * u).astype(BF16)
    y = x + _dot(act, wd_ref[...])
    ms2 = jnp.mean(y * y, axis=-1, keepdims=True)
    o_ref[...] = y * lax.rsqrt(ms2 + EPS) * g2_ref[...]


def _ffn(x1, g1, wg, wu, wd, g2, *, tm=256):
    n = x1.shape[0]
    row = lambda i: (i, 0)
    const = lambda i: (0, 0)
    return pl.pallas_call(
        _ffn_kernel,
        grid=(n // tm,),
        in_specs=[
            pl.BlockSpec((tm, D_MODEL), row),
            pl.BlockSpec((1, D_MODEL), const),
            pl.BlockSpec((D_MODEL, D_FF), const),
            pl.BlockSpec((D_MODEL, D_FF), const),
            pl.BlockSpec((D_FF, D_MODEL), const),
            pl.BlockSpec((1, D_MODEL), const),
        ],
        out_specs=pl.BlockSpec((tm, D_MODEL), row),
        out_shape=jax.ShapeDtypeStruct((n, D_MODEL), F32),
        compiler_params=pltpu.CompilerParams(
            dimension_semantics=("parallel",), vmem_limit_bytes=VMEM_LIMIT),
        name="ffn",
    )(x1, g1, wg, wu, wd, g2)


def _rope_swap(w, n_heads):
    w3 = w.reshape(w.shape[0], n_heads, DH)
    half = ROPE_DIM // 2
    sw = jnp.concatenate([-w3[:, :, half:ROPE_DIM], w3[:, :, 0:half],
                          jnp.zeros_like(w3[:, :, ROPE_DIM:])], axis=2)
    return sw.reshape(w.shape)


def _rope_tables(seq):
    half = ROPE_DIM // 2
    inv_freq = ROPE_THETA ** (-jnp.arange(half, dtype=F32) / half)
    ang = jnp.arange(seq).astype(F32)[:, None] * inv_freq[None, :]
    cos = jnp.cos(ang)
    sin = jnp.sin(ang)
    ones = jnp.ones((seq, DH - ROPE_DIM), F32)
    cos64 = jnp.concatenate([cos, cos, ones], axis=1)
    sin64 = jnp.concatenate([sin, sin, 0.0 * ones], axis=1)
    return jnp.concatenate([cos64, cos64], axis=1), jnp.concatenate([sin64, sin64], axis=1)


def _overlap_t(n_sel, n_cmp_pad, n_slc, n_cmp):
    c_start = np.arange(n_cmp_pad)[None, :] * CMP_STRIDE
    s_start = np.arange(n_sel)[:, None] * SLC_LEN
    ov = (c_start < s_start + SLC_LEN) & (c_start + CMP_LEN > s_start)
    ov &= (np.arange(n_cmp_pad)[None, :] < n_cmp) & (np.arange(n_sel)[:, None] < n_slc)
    return jnp.asarray(ov, dtype=BF16)


def _pack_in_weights(w_in):
    w_in = w_in.astype(BF16)
    c = 0
    wq = w_in[:, c:c + 512]; c += 512
    wkv = w_in[:, c:c + 768]; c += 768
    wgate = w_in[:, c:c + 24]; c += 24
    wgq = w_in[:, c:c + 256]; c += 256
    wgk = w_in[:, c:c + 256]; c += 256
    wgv = w_in[:, c:c + 512]; c += 512
    wlr = w_in[:, c:c + GLA_RANK]; c += GLA_RANK
    wgr = w_in[:, c:c + 512]; c += 512
    wmg = w_in[:, c:c + 2048]
    d = w_in.shape[0]
    wk_slc = wkv[:, 256:384]
    wk_win = wkv[:, 512:640]
    wg3 = wgate.reshape(d, NSA_GROUPS, HPG, 3).transpose(0, 1, 3, 2).reshape(d, NSA_GROUPS, 3 * HPG)
    wg_pad = jnp.pad(wg3, ((0, 0), (0, 0), (0, LANE - 3 * HPG))).reshape(d, NSA_GROUPS * LANE)
    wlr_pad = jnp.pad(wlr, ((0, 0), (0, LANE - GLA_RANK)))
    w_all = jnp.concatenate([
        wq, _rope_swap(wq, NSA_HEADS), wkv,
        _rope_swap(wk_slc, NSA_GROUPS), _rope_swap(wk_win, NSA_GROUPS),
        wg_pad, wgq, wgk, wgv, wgr, wlr_pad, wmg], axis=1)
    return w_all


def kernel(x, norm_mix, w_in, cmp_pe_k, cmp_pe_v, cmp_k_w1, cmp_k_w2, cmp_v_w1, cmp_v_w2,
           gla_gate_w2, gla_gate_b, gla_norm, w_up_nsa, w_up_gla, w_out, norm_ffn,
           w_ffn_gate, w_ffn_up, w_ffn_down, norm_final):
    batch, seq, d = x.shape
    assert d == D_MODEL and w_in.shape[0] == 1
    assert seq % KEY_TILE == 0
    n = batch * seq
    x2d = x.reshape(n, d)

    w_all = _pack_in_weights(w_in[0])
    assert w_all.shape[1] == SEG_END
    cos_t, sin_t = _rope_tables(seq)
    w2p =jnp.pad(gla_gate_w2[0], ((0, LANE - GLA_RANK), (0, 0))).astype(BF16)
    b2 = gla_gate_b[0].reshape(1, -1)

    (qraw, qrot, cmp_src, kaug, vslc_t, kwin, vwin_t, gates, gq, gk, gv, ga, gr, mg) = _in_proj(
        x2d, norm_mix[0].reshape(1, d), w_all, cos_t, sin_t, w2p, b2, seq=seq)

    n_sub = seq // CMP_STRIDE
    n_cmp = n_sub - CMP_LEN // CMP_STRIDE + 1
    n_slc = seq // SLC_LEN
    cmp_in = cmp_src.reshape(4, batch, n_sub, CMP_STRIDE * DH)
    w1 = jnp.stack([cmp_k_w1[0], cmp_v_w1[0]]).astype(BF16)
    w2 = jnp.stack([cmp_k_w2[0], cmp_v_w2[0]]).astype(BF16)
    pe = jnp.stack([cmp_pe_k[0].reshape(1, -1), cmp_pe_v[0].reshape(1, -1)])
    pe = jnp.broadcast_to(pe, (2, 8, CMP_LEN * DH)).astype(BF16)
    w2t = jnp.stack([cmp_k_w2[0].T, cmp_v_w2[0].T]).astype(BF16)
    cmp_kv, cmp_kv_t = _compress(cmp_in, w1, w2, w2t, pe, batch=batch, n_sub=n_sub)

    n_sel = -(-n_slc // SEL_HALF) * SEL_HALF
    ovt = _overlap_t(n_sel, n_sub, n_slc, n_cmp)
    o_a = _nsa(qraw, qrot, cmp_kv, cmp_kv_t, ovt, kaug, vslc_t, kwin, vwin_t, gates,
               batch=batch, seq=seq)

    per_seq = lambda a: a.reshape(batch, seq, a.shape[-1])
    o_b = _gla(per_seq(gq), per_seq(gk), per_seq(gv), per_seq(ga), per_seq(gr),
               gla_norm[0].reshape(1, -1), batch=batch, seq=seq).reshape(n, -1)

    x1 = _merge(x2d, o_a, o_b, mg, w_up_nsa[0].astype(BF16), w_up_gla[0].astype(BF16),
                w_out[0].astype(BF16))
    out = _ffn(x1, norm_ffn[0].reshape(1, d), w_ffn_gate[0].astype(BF16), w_ffn_up[0].astype(BF16),
               w_ffn_down[0].astype(BF16), norm_final.reshape(1, d))
    return out.reshape(batch, seq, d)
```

```python
import functools

import numpy as np
import jax
import jax.numpy as jnp
from jax import lax
from jax.experimental import pallas as pl
from jax.experimental.pallas import tpu as pltpu

F32 = jnp.float32
BF16 = jnp.bfloat16

D_MODEL = 1024
NSA_HEADS = 8
NSA_GROUPS = 2
HPG = NSA_HEADS // NSA_GROUPS
DH = 64
CMP_LEN = 32
CMP_STRIDE = 16
CMP_HIDDEN = 128
SLC_LEN = 64
SLC_TOPK = 16
WINDOW = 512
FORCE_SCORE = 1.0e4
GLA_HEADS = 4
GLA_DK = 64
GLA_DV = 128
GLA_RANK = 16
GLA_TAU = 16.0
GLA_CHUNK = 64
GLA_SUB = 16
ROPE_THETA = 500000.0
ROPE_DIM = DH // 4
D_FF = 2816
EPS = 1e-6
NEG = -1e30
LOG2E = 1.4426950408889634
KEY_TILE = 256

LANE = 128
VMEM_LIMIT = 56 * 1024 * 1024
SEL_HALF = 64
V_ROWS = DH + 16
SEL_GROUP = 4

SEG_Q = 0
SEG_KV = SEG_Q + 512
SEG_MISC = SEG_KV + 768
SEG_GLA = SEG_MISC + 128
SEG_END = SEG_GLA + 1536
GATE_LANES = 16
LR_LANE = 64


def _dot(a, b):
    return jnp.dot(a, b, preferred_element_type=F32)


def _dot_nt(a, b):
    return lax.dot_general(a, b, (((1,), (1,)), ((), ())), preferred_element_type=F32)


def _sigmoid(x):
    return 1.0 / (1.0 + jnp.exp(-x))


def _in_proj_kernel(x_ref, gain_ref, w_ref, cos_ref, sin_ref, w2_ref, b2_ref,
                    qraw_ref, qrot_ref, cmp_ref, kaug_ref, vslc_ref, kwin_ref, vwin_ref,
                    gate_ref, gq_ref, gk_ref, gv_ref, ga_ref, gr_ref, *, tm, seq):
    i = pl.program_id(0)
    x = x_ref[...]
    ms = jnp.mean(x * x, axis=-1, keepdims=True)
    h = (x * lax.rsqrt(ms + EPS) * gain_ref[...]).astype(BF16)

    def proj(a, b):
        return _dot(h, w_ref[:, a:b])

    cos = cos_ref[...]
    sin = sin_ref[...]

    def rope(v):
        width = v.shape[1]
        half = ROPE_DIM // 2
        first = lax.broadcasted_iota(jnp.int32, v.shape, 1) % DH < half
        partner = jnp.where(first, pltpu.roll(v, width - half, axis=1), pltpu.roll(v, half, axis=1))
        reps = width // LANE
        c = cos if reps == 1 else jnp.concatenate([cos] * reps, axis=1)
        s = sin if reps == 1 else jnp.concatenate([sin] * reps, axis=1)
        return v * c + partner * s

    scale = DH ** -0.5
    q = proj(SEG_Q, SEG_Q + 512)
    qraw_t = (q * (scale * LOG2E)).T
    qrot_t = (rope(q) * (scale * LOG2E)).T
    gw = HPG * DH
    for g in range(NSA_GROUPS):
        qraw_ref[g, 0] = qraw_t[g * gw:(g + 1) * gw, :].astype(BF16)
        qrot_ref[g, 0] = qrot_t[g * gw:(g + 1) * gw, :].astype(BF16)

    kv = proj(SEG_KV, SEG_KV + 768)
    for j in range(2):
        for g in range(NSA_GROUPS):
            cmp_ref[2 * j + g] = kv[:, j * 128 + g * DH:j * 128 + (g + 1) * DH].astype(BF16)
    kslc = rope(kv[:, 256:384])
    kwin = rope(kv[:, 512:640])
    vslc_t = kv[:, 384:512].T
    vwin_t = kv[:, 640:768].T
    pos = (i * tm) % seq + lax.broadcasted_iota(jnp.int32, (tm, SEL_HALF), 0)
    blk = (pos // SLC_LEN) % SEL_HALF
    onehot = jnp.where(lax.broadcasted_iota(jnp.int32, (tm, SEL_HALF), 1) == blk, 1.0, 0.0)
    ones_rows = jnp.ones((V_ROWS - DH, tm), F32)
    for g in range(NSA_GROUPS):
        sl = slice(g * DH, (g + 1) * DH)
        kaug_ref[g] =jnp.concatenate([kslc[:, sl], onehot], axis=1).astype(BF16)
        vslc_ref[g, 0] = jnp.concatenate([vslc_t[sl, :], ones_rows], axis=0).astype(BF16)
        kwin_ref[g] = kwin[:, sl].astype(BF16)
        vwin_ref[g, 0] = jnp.concatenate([vwin_t[sl, :], ones_rows], axis=0).astype(BF16)

    misc = proj(SEG_MISC, SEG_MISC + 128)
    gate_ref[...] = _sigmoid(misc)

    gla = proj(SEG_GLA, SEG_GLA + 1536)
    gq_ref[...] = (gla[:, 0:256] * (GLA_DK ** -0.5)).astype(BF16)
    gk_ref[...] = gla[:, 256:512].astype(BF16)
    gv_ref[...] = gla[:, 512:1024].astype(BF16)
    r = gla[:, 1024:1536]
    gr_ref[...] = (r * _sigmoid(r)).astype(BF16)

    z = _dot(misc.astype(BF16), w2_ref[...]) + b2_ref[...]
    log_sig = jnp.minimum(z, 0.0) - jnp.log1p(jnp.exp(-jnp.abs(z)))
    ga_ref[...] = log_sig * (1.0 / GLA_TAU)


def _in_proj(x2d, gain, w_all, cos_t, sin_t, w2p, b2, *, seq, tm=KEY_TILE):
    n = x2d.shape[0]
    nt = seq // tm
    row = lambda i: (i, 0)
    grp = lambda i: (0, i, 0)
    const = lambda i: (0, 0)
    tab = lambda i: (i % nt, 0)
    out_shape = (
        jax.ShapeDtypeStruct((NSA_GROUPS, n // tm, HPG * DH, tm), BF16),
        jax.ShapeDtypeStruct((NSA_GROUPS, n // tm, HPG * DH, tm), BF16),
        jax.ShapeDtypeStruct((4, n, DH), BF16),
        jax.ShapeDtypeStruct((NSA_GROUPS, n, 128), BF16),
        jax.ShapeDtypeStruct((NSA_GROUPS, n // tm, V_ROWS, tm), BF16),
        jax.ShapeDtypeStruct((NSA_GROUPS, n, DH), BF16),
        jax.ShapeDtypeStruct((NSA_GROUPS, n // tm, V_ROWS, tm), BF16),
        jax.ShapeDtypeStruct((n, LANE), F32),
        jax.ShapeDtypeStruct((n, 256), BF16),
        jax.ShapeDtypeStruct((n, 256), BF16),
        jax.ShapeDtypeStruct((n, 512), BF16),
        jax.ShapeDtypeStruct((n, 256), F32),
        jax.ShapeDtypeStruct((n, 512), BF16),
    )
    out_specs = (
        pl.BlockSpec((NSA_GROUPS, 1, HPG * DH, tm), lambda i: (0, i, 0, 0)),
        pl.BlockSpec((NSA_GROUPS, 1, HPG * DH, tm), lambda i: (0, i, 0, 0)),
        pl.BlockSpec((4, tm, DH), grp),
        pl.BlockSpec((NSA_GROUPS, tm, 128), grp),
        pl.BlockSpec((NSA_GROUPS, 1, V_ROWS, tm), lambda i: (0, i, 0, 0)),
        pl.BlockSpec((NSA_GROUPS, tm, DH), grp),
        pl.BlockSpec((NSA_GROUPS, 1, V_ROWS, tm), lambda i: (0, i, 0, 0)),
        pl.BlockSpec((tm, LANE), row),
        pl.BlockSpec((tm, 256), row), pl.BlockSpec((tm, 256), row), pl.BlockSpec((tm, 512), row),
        pl.BlockSpec((tm, 256), row), pl.BlockSpec((tm, 512), row),
    )
    in_specs = [
        pl.BlockSpec((tm, D_MODEL), row),
        pl.BlockSpec((1, D_MODEL), const),
        pl.BlockSpec((D_MODEL, SEG_END), const, pipeline_mode=pl.Buffered(1)),
        pl.BlockSpec((tm, LANE), tab),
        pl.BlockSpec((tm, LANE), tab),
        pl.BlockSpec((LANE, 256), const),
        pl.BlockSpec((1, 256), const),
    ]
    return pl.pallas_call(
        functools.partial(_in_proj_kernel, tm=tm, seq=seq),
        grid=(n // tm,),
        in_specs=in_specs, out_specs=out_specs, out_shape=out_shape,
        compiler_params=pltpu.CompilerParams(
            dimension_semantics=("parallel",), vmem_limit_bytes=VMEM_LIMIT),
        name="in_proj",
    )(x2d, gain, w_all, cos_t, sin_t, w2p, b2)


def _compress_kernel(x_ref, w1_ref, w2_ref, w2t_ref, pe_ref, o_ref, ot_ref, *, n_sub):
    half = CMP_STRIDE * DH
    x = x_ref[...]
    u = _dot(x, w1_ref[0:half, :])
    v = _dot(x, w1_ref[half:2 * half, :])
    c = _dot(pe_ref[...], w1_ref[...])[0:1, :]
    hid = u + pltpu.roll(v, shift=n_sub - 1, axis=0) + c
    hid = (hid * _sigmoid(hid)).astype(BF16)
    out = _dot(hid, w2_ref[...])
    rowi = lax.broadcasted_iota(jnp.int32, out.shape, 0)
    o_ref[...] = jnp.where(rowi < n_sub - 1, out, 0.0).astype(BF16)
    out_t = _dot_nt(w2t_ref[...], hid)
    coli = lax.broadcasted_iota(jnp.int32, out_t.shape, 1)
    ot_ref[0:DH, :] = jnp.where(coli < n_sub - 1, out_t, 0.0).astype(BF16)
    ot_ref[DH:V_ROWS, :] = jnp.ones((V_ROWS - DH, n_sub), BF16)


def _compress(src, w1, w2, w2t, pe, *, batch, n_sub):
    return pl.pallas_call(
        functools.partial(_compress_kernel, n_sub=n_sub),
        grid=(4, batch),
        in_specs=[
            pl.BlockSpec((None, None, n_sub, CMP_STRIDE * DH), lambda j, b: (j, b, 0, 0)),
            pl.BlockSpec((None, CMP_LEN * DH, CMP_HIDDEN), lambda j, b: (j // 2, 0, 0)),
            pl.BlockSpec((None, CMP_HIDDEN, DH), lambda j, b: (j // 2, 0, 0)),
            pl.BlockSpec((None, DH, CMP_HIDDEN), lambda j, b: (j // 2, 0, 0)),
            pl.BlockSpec((None, 8, CMP_LEN * DH), lambda j, b: (j // 2, 0, 0)),
        ],
        out_specs=(pl.BlockSpec((None, None, n_sub, DH), lambda j, b: (j, b, 0, 0)),
                   pl.BlockSpec((None, None, V_ROWS, n_sub), lambda j, b: (j, b, 0, 0))),
        out_shape=(jax.ShapeDtypeStruct((4, batch, n_sub, DH), BF16),
                   jax.ShapeDtypeStruct((4, batch, V_ROWS, n_sub), BF16)),
        compiler_params=pltpu.CompilerParams(
            dimension_semantics=("parallel", "parallel"), vmem_limit_bytes=VMEM_LIMIT),
        name="compress",
    )(src, w1, w2, w2t, pe)


def _softmax_tiles(scores, vts, carry):
    stats = []
    for s, (m, _) in zip(scores, carry):
        m_new = jnp.maximum(m, jnp.max(s, axis=0, keepdims=True))
        stats.append((m_new, jnp.exp2(m - m_new), jnp.exp2((s - m_new).astype(BF16))))
    return tuple((m_new, alpha * acc + _dot(vt, p))
                 for (m_new, alpha, p), vt, (_, acc) in zip(stats, vts, carry))


def _nsa_kernel(qraw_ref, qrot_ref, kc_ref, vct_ref, ovt_ref, kaug_ref, vst_ref, kw_ref, vwt_ref,
                gate_ref, o_ref, qa_ref, gt_ref, *, tq, topk):
    i = pl.program_id(2)
    s0 = i * tq
    n_cmp = kc_ref.shape[0]
    n_sel = ovt_ref.shape[0]

    def init_carry():
        lane = lax.broadcasted_iota(jnp.int32, (1, tq), 1)
        one = (jnp.where(lane >= 0, NEG, 0.0), jnp.zeros((V_ROWS, tq), F32))
        return tuple(one for _ in range(HPG))

    def normalised(acc):
        return acc[0:DH] / acc[DH:DH + 1]

    kc = kc_ref[...]
    vct = vct_ref[...]
    cend = lax.broadcasted_iota(jnp.int32, (n_cmp, tq), 0) * CMP_STRIDE + (CMP_LEN - 1)
    cmask = cend <= s0 + lax.broadcasted_iota(jnp.int32, (n_cmp, tq), 1)
    scores = [_dot(kc, qraw_ref[h * DH:(h + 1) * DH, :]) for h in range(HPG)]
    expd, seen = [], []
    for s in scores:
        s = jnp.where(cmask, s, NEG)
        m = jnp.max(s, axis=0, keepdims=True)
        expd.append(jnp.exp2((s - m).astype(BF16)))
        seen.append(m > 0.5 * NEG)
    ovt = ovt_ref[...]
    oc_t, imp_t = [], 0.0
    for e, ok in zip(expd, seen):
        acc = _dot(vct, e)
        r = jnp.where(ok, 1.0 / acc[DH:DH + 1], 0.0)
        oc_t.append(acc[0:DH] * r)
        imp_t = imp_t + _dot(ovt, e) * r

    key_i = lax.broadcasted_iota(jnp.int32, (tq, tq), 0)
    qry_i = lax.broadcasted_iota(jnp.int32, (tq, tq), 1)
    n_back = WINDOW // tq
    w_tiles = [jnp.maximum(i - n_back + u, 0) for u in range(n_back)]
    w_keys = [kw_ref[pl.ds(pl.multiple_of(t * tq, tq), tq), :] for t in w_tiles]
    w_bands = [(i >= n_back - u) & ((qry_i < key_i) if u == 0 else True) for u in range(n_back)]
    w_scores = [jnp.concatenate(
        [jnp.where(band, _dot(kt, qrot_ref[h * DH:(h + 1) * DH, :]), NEG)
         for kt, band in zip(w_keys, w_bands)], axis=0) for h in range(HPG)]
    w_vt = jnp.concatenate([vwt_ref[t] for t in w_tiles], axis=1)
    win_carry = _softmax_tiles(w_scores, [w_vt] * HPG, init_carry())

    jrow =lax.broadcasted_iota(jnp.int32, (n_sel, tq), 0)
    tlane = s0 + lax.broadcasted_iota(jnp.int32, (n_sel, tq), 1)
    cur = tlane // SLC_LEN
    valid = jrow * SLC_LEN <= tlane
    forced = (jrow == 0) | (jrow == cur) | (jrow == cur - 1)
    score = jnp.where(valid, jnp.where(forced, FORCE_SCORE, imp_t), -jnp.inf)

    jrow_f = jrow.astype(F32)
    work = score
    for _ in range(topk):
        top = jnp.max(work, axis=0, keepdims=True)
        first = jnp.min(jnp.where(work == top, jrow_f, float(n_sel)), axis=0, keepdims=True)
        work = jnp.where(jrow_f == first, -jnp.inf, work)
    bias_t = jnp.where(work != score, 0.0, NEG).astype(BF16)

    n_half = n_sel // SEL_HALF
    for a in range(n_half):
        for h in range(HPG):
            qa_ref[a, h, 0:DH, :] = qrot_ref[h * DH:(h + 1) * DH, :]
            qa_ref[a, h, DH:DH + SEL_HALF, :] = bias_t[a * SEL_HALF:(a + 1) * SEL_HALF, :]

    tiles_per_half = SEL_HALF * SLC_LEN // tq

    def sel_scores(first, count):
        kt = kaug_ref[pl.ds(pl.multiple_of(first * tq, tq), count * tq), :]
        half = first // tiles_per_half
        return [_dot(kt, qa_ref[half, h]) for h in range(HPG)]

    def sel_step(first, count, carry):
        vt = vst_ref[first] if count == 1 else jnp.concatenate(
            [vst_ref[first + u] for u in range(count)], axis=1)
        return _softmax_tiles(sel_scores(first, count), [vt] * HPG, carry)

    n_groups = i // SEL_GROUP
    sel_carry = lax.fori_loop(
        0, n_groups, lambda t, c: sel_step(t * SEL_GROUP, SEL_GROUP, c), init_carry())
    sel_carry = lax.fori_loop(n_groups * SEL_GROUP, i, lambda t, c: sel_step(t, 1, c), sel_carry)
    win_diag = [_dot(kw_ref[pl.ds(pl.multiple_of(s0, tq), tq), :], qa_ref[0, h, 0:DH, :])
                for h in range(HPG)]
    diag = [jnp.where(key_i <= qry_i, s, NEG) for s in sel_scores(i, 1) + win_diag]
    carry = _softmax_tiles(diag, [vst_ref[i]] * HPG + [vwt_ref[i]] * HPG, sel_carry + win_carry)
    os_t = [normalised(acc) for (_, acc) in carry[:HPG]]
    ow_t = [normalised(acc) for (_, acc) in carry[HPG:]]

    gt_ref[...] = gate_ref[...].T
    gt_t = gt_ref[pl.ds(pl.multiple_of(pl.program_id(1) * GATE_LANES, GATE_LANES), GATE_LANES), :]
    outs =[gt_t[h:h + 1, :] * oc_t[h] + gt_t[HPG + h:HPG + h + 1, :] * os_t[h]
            + gt_t[2 * HPG + h:2 * HPG + h + 1, :] * ow_t[h] for h in range(HPG)]
    o_ref[...] = jnp.concatenate(outs, axis=0).T.astype(BF16)


def _nsa(qraw_t, qrot_t, cmp_kv, cmp_kv_t, ovt, kaug, vslc_t, kwin, vwin_t, gates, *, batch, seq,
         tq=KEY_TILE):
    n = batch * seq
    nq = seq // tq
    nk = nq
    tk = tq
    n_cmp = cmp_kv.shape[2]
    n_sel = ovt.shape[0]
    topk = min(SLC_TOPK, seq // SLC_LEN)
    qmap = lambda b, g, i: (b * nq + i, g)
    qtmap = lambda b, g, i: (g, b * nq + i, 0, 0)
    kvmap = lambda b, g, i: (g, b, 0)
    vtmap = lambda b, g, i: (g, b, 0, 0)
    return pl.pallas_call(
        functools.partial(_nsa_kernel, tq=tq, topk=topk),
        grid=(batch, NSA_GROUPS, nq),
        in_specs=[
            pl.BlockSpec((None, None, HPG * DH, tq), qtmap),
            pl.BlockSpec((None, None, HPG * DH, tq), qtmap),
            pl.BlockSpec((None, None, n_cmp, DH), lambda b, g, i: (g, b, 0, 0)),
            pl.BlockSpec((None, None, V_ROWS, n_cmp), lambda b, g, i: (NSA_GROUPS + g, b, 0, 0)),
            pl.BlockSpec((n_sel, n_cmp), lambda b, g, i: (0, 0)),
            pl.BlockSpec((None, seq, 128), kvmap),
            pl.BlockSpec((None, nk, V_ROWS, tk), vtmap),
            pl.BlockSpec((None, seq, DH), kvmap),
            pl.BlockSpec((None, nk, V_ROWS, tk), vtmap),
            pl.BlockSpec((tq, LANE), lambda b, g, i: (b * nq + i, 0)),
        ],
        out_specs=pl.BlockSpec((tq, HPG * DH), qmap),
        out_shape=jax.ShapeDtypeStruct((n, NSA_HEADS * DH), BF16),
        scratch_shapes=[
            pltpu.VMEM((n_sel // SEL_HALF, HPG, DH + SEL_HALF, tq), BF16),
            pltpu.VMEM((LANE, tq), F32),
        ],
        compiler_params=pltpu.CompilerParams(
            dimension_semantics=("parallel", "parallel", "arbitrary"),
            vmem_limit_bytes=VMEM_LIMIT),
        name="nsa",
    )(qraw_t, qrot_t, cmp_kv, cmp_kv_t, ovt, kaug, vslc_t, kwin, vwin_t, gates)


def _gla_kernel(q_ref, k_ref, v_ref, g_ref, r_ref, gn_ref, o_ref, st_ref, *, tm, chunk):
    nb = q_ref.shape[0]
    n_chunk = tm // chunk
    n_sub = chunk // GLA_SUB
    dk_all = GLA_HEADS * GLA_DK
    dv_all = GLA_HEADS * GLA_DV

    @pl.when(pl.program_id(0) == 0)
    def _():
        st_ref[...] = jnp.zeros(st_ref.shape, F32)

    def iota(shape, axis):
        return lax.broadcasted_iota(jnp.int32, shape, axis)

    tril = jnp.where((iota((tm, tm), 0) >= iota((tm, tm), 1))
                     & (iota((tm, tm), 0) // chunk == iota((tm, tm), 1) // chunk), 1.0, 0.0).astype(BF16)
    k_head = iota((chunk, dk_all), 1) // GLA_DK
    v_head = iota((chunk, dv_all), 1) // GLA_DV
    a_rows = iota((n_sub * chunk, GLA_HEADS * chunk), 0)
    a_key = iota((n_sub * chunk, GLA_HEADS * chunk), 1) % chunk
    a_keep = (a_key // GLA_SUB == a_rows // chunk) & (a_key <= a_rows % chunk)
    st_keep = iota((dv_all, dk_all), 0) // GLA_DV == iota((dv_all, dk_all), 1) // GLA_DK
    gn = jnp.concatenate([gn_ref[...]] * GLA_HEADS, axis=1)

    work = []
    for b in range(nb):
        g = g_ref[b]
        g_hi = g.astype(BF16)
        g_lo = (g - g_hi.astype(F32)).astype(BF16)
        cum_all = _dot(tril, g_hi) + _dot(tril, g_lo)
        for c in range(n_chunk):
            rows = slice(c * chunk, (c + 1) * chunk)
            cum = cum_all[rows]
            q = q_ref[b, rows, :].astype(F32)
            k = k_ref[b, rows, :].astype(F32)
            v = v_ref[b, rows, :]
            last = cum[chunk - 1:chunk, :]
            refs = [cum[(j + 1) * GLA_SUB - 1:(j + 1) * GLA_SUB, :] for j in range(n_sub)]
            k_hat = jnp.concatenate(
                [k[j * GLA_SUB:(j + 1) * GLA_SUB] * jnp.exp(refs[j] - cum[j * GLA_SUB:(j + 1) * GLA_SUB])
                 for j in range(n_sub)], axis=0)
            q_hat = jnp.concatenate(
                [q * jnp.exp(jnp.minimum(cum - refs[j], 80.0)) for j in range(n_sub)], axis=0)
            k_stack = jnp.concatenate(
                [jnp.where(k_head == h, k_hat, 0.0) for h in range(GLA_HEADS)], axis=0)
            a_all = _dot_nt(q_hat.astype(BF16), k_stack.astype(BF16))
            k_dec = (k * jnp.exp(last - cum)).astype(BF16)
            v_t = v.astype(F32).T.astype(BF16)
            st_inc = _dot(v_t, k_dec)
            work.append(dict(b=b, rows=rows, a_all=a_all, st_inc=st_inc, v=v,
                             q_in=(q * jnp.exp(cum)).astype(BF16), decay=jnp.exp(last)))

    states = [st_ref[b] for b in range(nb)]
    for w in work:
        st = states[w["b"]]
        w["o_inter"] = _dot_nt(w["q_in"], st.astype(BF16))
        states[w["b"]] = st * w["decay"] + jnp.where(st_keep, w["st_inc"], 0.0)
    for b in range(nb):
        st_ref[b] = states[b]

    for w in work:
        a_m = jnp.where(a_keep, w["a_all"], 0.0)
        a = a_m[0:chunk]
        for j in range(1, n_sub):
            a = a + a_m[j * chunk:(j + 1) * chunk]
        v_blocks = jnp.concatenate(
            [jnp.where(v_head == h, w["v"], jnp.zeros_like(w["v"])) for h in range(GLA_HEADS)], axis=0)
        o = w["o_inter"] + _dot(a.astype(BF16), v_blocks)
        scale = jnp.concatenate(
            [jnp.broadcast_to(lax.rsqrt(jnp.mean(
                o[:, h * GLA_DV:(h + 1) * GLA_DV] ** 2, axis=-1, keepdims=True) + EPS), (chunk, GLA_DV))
             for h in range(GLA_HEADS)], axis=1)
        y = o * scale * gn * r_ref[w["b"], w["rows"], :].astype(F32)
        o_ref[w["b"], w["rows"], :] = y.astype(BF16)


def _gla(gq, gk, gv, ga, gr, gnorm, *, batch, seq, tm=KEY_TILE, chunk=GLA_CHUNK):
    tmap = lambda i: (0, i, 0)
    dk_all = GLA_HEADS * GLA_DK
    dv_all = GLA_HEADS * GLA_DV
    return pl.pallas_call(
        functools.partial(_gla_kernel, tm=tm, chunk=chunk),
        grid=(seq // tm,),
        in_specs=[
            pl.BlockSpec((batch, tm, dk_all), tmap),
            pl.BlockSpec((batch, tm, dk_all), tmap),
            pl.BlockSpec((batch, tm, dv_all), tmap),
            pl.BlockSpec((batch, tm, dk_all), tmap),
            pl.BlockSpec((batch, tm, dv_all), tmap),
            pl.BlockSpec((1, GLA_DV), lambda i: (0, 0)),
        ],
        out_specs=pl.BlockSpec((batch, tm, dv_all), tmap),
        out_shape=jax.ShapeDtypeStruct((batch, seq, dv_all), BF16),
        scratch_shapes=[pltpu.VMEM((batch, dv_all, dk_all), F32)],
        compiler_params=pltpu.CompilerParams(
            dimension_semantics=("arbitrary",), vmem_limit_bytes=VMEM_LIMIT),
        name="gla",
    )(gq, gk, gv, ga, gr, gnorm)


def _tail_kernel(x_ref, oa_ref, ob_ref, g0_ref, wm_ref, wn_ref, wb_ref, wo_ref,
                 g1_ref, wg_ref, wu_ref, wd_ref, g2_ref, o_ref):
    x = x_ref[...]

    def rms(v, gain_ref):
        return v * lax.rsqrt(jnp.mean(v * v, axis=-1, keepdims=True) + EPS) * gain_ref[...]

    gates = _sigmoid(_dot(rms(x, g0_ref).astype(BF16), wm_ref[...]))
    merged = (gates[:, 0:D_MODEL] * _dot(oa_ref[...], wn_ref[...])
              + gates[:, D_MODEL:2 * D_MODEL] * _dot(ob_ref[...], wb_ref[...]))
    x1 = x + _dot(merged.astype(BF16), wo_ref[...])
    h = rms(x1, g1_ref).astype(BF16)
    a = _dot(h, wg_ref[...])
    u = _dot(h, wu_ref[...])
    y = x1 + _dot((a * _sigmoid(a) * u).astype(BF16), wd_ref[...])
    o_ref[...] = rms(y, g2_ref)


def _tail(x2d, oa, ob, g0, wm, wn, wb, wo, g1, wg, wu, wd, g2, *, tm=256):
    n = x2d.shape[0]
    row = lambda i: (i, 0)
    const = lambda i: (0, 0)
    weight = lambda r, c: pl.BlockSpec((r, c), const, pipeline_mode=pl.Buffered(1))
    return pl.pallas_call(
        _tail_kernel,
        grid=(n // tm,),
        in_specs=[
            pl.BlockSpec((tm, D_MODEL), row),
            pl.BlockSpec((tm, NSA_HEADS * DH), row),
            pl.BlockSpec((tm, GLA_HEADS * GLA_DV), row),
            weight(1, D_MODEL), weight(D_MODEL, 2 * D_MODEL),
            weight(NSA_HEADS * DH, D_MODEL), weight(GLA_HEADS * GLA_DV, D_MODEL),
            weight(D_MODEL, D_MODEL),
            weight(1, D_MODEL), weight(D_MODEL, D_FF), weight(D_MODEL, D_FF), weight(D_FF, D_MODEL),
            weight(1, D_MODEL),
        ],
        out_specs=pl.BlockSpec((tm, D_MODEL), row),
        out_shape=jax.ShapeDtypeStruct((n, D_MODEL), F32),
        compiler_params=pltpu.CompilerParams(
            dimension_semantics=("parallel",), vmem_limit_bytes=VMEM_LIMIT),
        name="tail",
    )(x2d, oa, ob, g0, wm, wn, wb, wo, g1, wg, wu, wd, g2)


def _rope_tables(seq):
    half = ROPE_DIM // 2
    inv_freq = ROPE_THETA ** (-jnp.arange(half, dtype=F32) / half)
    ang = jnp.arange(seq).astype(F32)[:, None] * inv_freq[None, :]
    cos = jnp.cos(ang)
    sin = jnp.sin(ang)
    ones = jnp.ones((seq, DH - ROPE_DIM), F32)
    cos_head = jnp.concatenate([cos, cos, ones], axis=1)
    sin_head = jnp.concatenate([-sin, sin, 0.0 * ones], axis=1)
    reps = LANE // DH
    return jnp.concatenate([cos_head] * reps, axis=1), jnp.concatenate([sin_head] * reps, axis=1)


def _overlap_t(n_sel, n_cmp_pad, n_slc, n_cmp):
    c_start = np.arange(n_cmp_pad)[None, :] * CMP_STRIDE
    s_start = np.arange(n_sel)[:, None] * SLC_LEN
    ov = (c_start < s_start + SLC_LEN) & (c_start + CMP_LEN > s_start)
    ov &= (np.arange(n_cmp_pad)[None, :] < n_cmp) & (np.arange(n_sel)[:, None] < n_slc)
    return jnp.asarray(ov, dtype=BF16)


def _pack_in_weights(w_in):
    w_in = w_in.astype(BF16)
    c = 0
    wq = w_in[:, c:c + 512]; c += 512
    wkv = w_in[:, c:c + 768]; c += 768
    wgate = w_in[:, c:c + 24]; c += 24
    wgq = w_in[:, c:c + 256]; c += 256
    wgk = w_in[:, c:c + 256]; c += 256
    wgv = w_in[:, c:c + 512]; c += 512
    wlr = w_in[:, c:c + GLA_RANK]; c += GLA_RANK
    wgr = w_in[:, c:c + 512]; c += 512
    wmg = w_in[:, c:c + 2048]
    d = w_in.shape[0]
    wg3 = wgate.reshape(d, NSA_GROUPS, HPG, 3).transpose(0, 1, 3, 2).reshape(d, NSA_GROUPS, 3 * HPG)
    wg_pad = jnp.pad(wg3, ((0, 0), (0, 0), (0, GATE_LANES - 3 * HPG))).reshape(d, NSA_GROUPS * GATE_LANES)
    misc = jnp.concatenate([
        wg_pad, jnp.zeros((d, LR_LANE - NSA_GROUPS * GATE_LANES), BF16),
        wlr, jnp.zeros((d, LANE - LR_LANE - GLA_RANK), BF16)], axis=1)
    w_all = jnp.concatenate([wq, wkv, misc, wgq, wgk, wgv, wgr], axis=1)
    return w_all, wmg


def kernel(x, norm_mix, w_in, cmp_pe_k, cmp_pe_v, cmp_k_w1, cmp_k_w2, cmp_v_w1, cmp_v_w2,
           gla_gate_w2, gla_gate_b, gla_norm, w_up_nsa, w_up_gla, w_out, norm_ffn,
           w_ffn_gate, w_ffn_up, w_ffn_down, norm_final):
    batch, seq, d = x.shape
    assert d == D_MODEL and w_in.shape[0] == 1
    assert seq % KEY_TILE == 0
    n = batch * seq
    x2d = x.reshape(n, d)

    w_all, w_merge = _pack_in_weights(w_in[0])
    assert w_all.shape[1] == SEG_END
    cos_t, sin_t = _rope_tables(seq)
    w2p = jnp.pad(gla_gate_w2[0], ((LR_LANE, LANE - LR_LANE - GLA_RANK), (0, 0))).astype(BF16)
    b2 = gla_gate_b[0].reshape(1, -1)
    gain_mix = norm_mix[0].reshape(1, d)

    (qraw, qrot, cmp_src, kaug, vslc_t, kwin, vwin_t, gates, gq, gk, gv, ga, gr) = _in_proj(
        x2d, gain_mix, w_all, cos_t, sin_t, w2p, b2, seq=seq)

    n_sub = seq // CMP_STRIDE
    n_cmp = n_sub - CMP_LEN // CMP_STRIDE + 1
    n_slc = seq // SLC_LEN
    cmp_in = cmp_src.reshape(4, batch, n_sub, CMP_STRIDE * DH)
    w1 = jnp.stack([cmp_k_w1[0], cmp_v_w1[0]]).astype(BF16)
    w2 = jnp.stack([cmp_k_w2[0], cmp_v_w2[0]]).astype(BF16)
    pe = jnp.stack([cmp_pe_k[0].reshape(1, -1), cmp_pe_v[0].reshape(1, -1)])
    pe = jnp.broadcast_to(pe, (2, 8, CMP_LEN * DH)).astype(BF16)
    w2t = jnp.stack([cmp_k_w2[0].T, cmp_v_w2[0].T]).astype(BF16)
    cmp_kv, cmp_kv_t = _compress(cmp_in, w1, w2, w2t, pe, batch=batch, n_sub=n_sub)

    n_sel = -(-n_slc // SEL_HALF) * SEL_HALF
    ovt = _overlap_t(n_sel, n_sub, n_slc, n_cmp)
    o_a = _nsa(qraw, qrot, cmp_kv, cmp_kv_t, ovt, kaug, vslc_t, kwin, vwin_t, gates,
               batch=batch, seq=seq)

    per_seq = lambda a: a.reshape(batch, seq, a.shape[-1])
    o_b = _gla(per_seq(gq), per_seq(gk), per_seq(gv), per_seq(ga), per_seq(gr),
               gla_norm[0].reshape(1, -1), batch=batch, seq=seq).reshape(n, -1)

    out = _tail(x2d, o_a, o_b, gain_mix, w_merge, w_up_nsa[0].astype(BF16), w_up_gla[0].astype(BF16),
                w_out[0].astype(BF16), norm_ffn[0].reshape(1, d), w_ffn_gate[0].astype(BF16),
                w_ffn_up[0].astype(BF16), w_ffn_down[0].astype(BF16), norm_final.reshape(1, d))
    return out.reshape(batch, seq, d)
```

```python
import functools

import numpy as np
import jax
import jax.numpy as jnp
from jax import lax
from jax.experimental import pallas as pl
from jax.experimental.pallas import tpu as pltpu

F32 = jnp.float32
BF16 = jnp.bfloat16

D_MODEL = 1024
NSA_HEADS = 8
NSA_GROUPS = 2
HPG = NSA_HEADS // NSA_GROUPS
DH = 64
CMP_LEN = 32
CMP_STRIDE = 16
CMP_HIDDEN = 128
SLC_LEN = 64
SLC_TOPK = 16
WINDOW = 512
FORCE_SCORE = 1.0e4
GLA_HEADS = 4
GLA_DK = 64
GLA_DV = 128
GLA_RANK = 16
GLA_TAU = 16.0
GLA_CHUNK = 64
GLA_SUB = 16
ROPE_THETA = 500000.0
ROPE_DIM = DH // 4
D_FF = 2816
EPS = 1e-6
NEG = -1e30
LOG2E = 1.4426950408889634
KEY_TILE = 256

LANE = 128
VMEM_LIMIT = 56 * 1024 * 1024
SEL_HALF = 64
V_ROWS = DH + 16
MXU_LEAD = 8
SEL_GROUP = 4

SEG_Q = 0
SEG_KV = SEG_Q + 512
SEG_MISC = SEG_KV + 768
SEG_GLA = SEG_MISC + 128
SEG_END = SEG_GLA + 1536
GATE_LANES = 16
LR_LANE = 64


def _dot(a, b):
    return jnp.dot(a, b, preferred_element_type=F32)


def _dot_nt(a, b):
    return lax.dot_general(a, b, (((1,), (1,)), ((), ())), preferred_element_type=F32)


def _sigmoid(x):
    return 1.0 / (1.0 + jnp.exp(-x))


def _in_proj_kernel(x_ref, gain_ref, w_ref, cos_ref, sin_ref, w2_ref, b2_ref,
                    qraw_ref, qrot_ref, cmp_ref, kaug_ref, vslc_ref, kwin_ref, vwin_ref,
                    gate_ref, gq_ref, gk_ref, gv_ref, ga_ref, gr_ref, flat_ref, *, tm, seq):
    i = pl.program_id(0)
    x = x_ref[...]
    ms = jnp.mean(x * x, axis=-1, keepdims=True)
    h = (x * lax.rsqrt(ms + EPS) * gain_ref[...]).astype(BF16)

    def proj(a, b):
        return _dot(h, w_ref[:, a:b])

    cos = cos_ref[...]
    sin = sin_ref[...]

    def rope(v):
        width = v.shape[1]
        half = ROPE_DIM // 2
        first = lax.broadcasted_iota(jnp.int32, v.shape, 1) % DH < half
        partner = jnp.where(first, pltpu.roll(v, width - half, axis=1), pltpu.roll(v, half, axis=1))
        reps = width // LANE
        c = cos if reps == 1 else jnp.concatenate([cos] * reps, axis=1)
        s = sin if reps == 1 else jnp.concatenate([sin] * reps, axis=1)
        return v * c + partner * s

    scale = DH ** -0.5
    q = proj(SEG_Q, SEG_Q + 512)
    qraw_t = (q * (scale * LOG2E)).T
    qrot_t = (rope(q) * (scale * LOG2E)).T
    gw = HPG * DH
    for g in range(NSA_GROUPS):
        qraw_ref[g, 0] = qraw_t[g * gw:(g + 1) * gw, :].astype(BF16)
        qrot_ref[g, 0] = qrot_t[g * gw:(g + 1) * gw, :].astype(BF16)

    kv = proj(SEG_KV, SEG_KV + 768)
    n_rows = tm // CMP_STRIDE
    for j in range(2):
        for g in range(NSA_GROUPS):
            flat_ref[2 * j + g] = kv[:, j * 128 + g * DH:j * 128 + (g + 1) * DH]
    for jg in range(2 * NSA_GROUPS):
        for l in range(0, CMP_STRIDE, 2):
            pair = jnp.concatenate(
                [flat_ref[jg, pl.ds(l, n_rows, stride=CMP_STRIDE), :],
                 flat_ref[jg, pl.ds(l + 1, n_rows, stride=CMP_STRIDE), :]], axis=1)
            cmp_ref[jg, :, l * DH:(l + 2) * DH] = pair.astype(BF16)
    kslc = rope(kv[:, 256:384])
    kwin = rope(kv[:, 512:640])
    vslc_t = kv[:, 384:512].T
    vwin_t = kv[:, 640:768].T
    pos = (i * tm) % seq + lax.broadcasted_iota(jnp.int32, (tm, SEL_HALF), 0)
    blk = (pos // SLC_LEN) % SEL_HALF
    onehot = jnp.where(lax.broadcasted_iota(jnp.int32, (tm, SEL_HALF), 1) == blk, 1.0, 0.0)
    ones_rows = jnp.ones((V_ROWS - DH, tm), F32)
    for g in range(NSA_GROUPS):
        sl = slice(g * DH, (g + 1) * DH)
        kaug_ref[g] =jnp.concatenate([kslc[:, sl], onehot], axis=1).astype(BF16)
        vslc_ref[g, 0] = jnp.concatenate([vslc_t[sl, :], ones_rows], axis=0).astype(BF16)
        kwin_ref[g] = kwin[:, sl].astype(BF16)
        vwin_ref[g, 0] = jnp.concatenate([vwin_t[sl, :], ones_rows], axis=0).astype(BF16)

    misc = proj(SEG_MISC, SEG_MISC + 128)
    gate_ref[...] = _sigmoid(misc)

    gla = proj(SEG_GLA, SEG_GLA + 1536)
    gq_ref[...] = (gla[:, 0:256] * (GLA_DK ** -0.5)).astype(BF16)
    gk_ref[...] = gla[:, 256:512].astype(BF16)
    gv_ref[...] = gla[:, 512:1024].astype(BF16)
    r = gla[:, 1024:1536]
    gr_ref[...] = (r * _sigmoid(r)).astype(BF16)

    z = _dot(misc.astype(BF16), w2_ref[...]) + b2_ref[...]
    log_sig = jnp.minimum(z, 0.0) - jnp.log1p(jnp.exp(-jnp.abs(z)))
    ga_ref[...] = log_sig * (1.0 / GLA_TAU)


def _in_proj(x2d, gain, w_all, cos_t, sin_t, w2p, b2, *, seq, tm=KEY_TILE):
    n = x2d.shape[0]
    nt = seq // tm
    row = lambda i: (i, 0)
    grp = lambda i: (0, i, 0)
    const = lambda i: (0, 0)
    tab = lambda i: (i % nt, 0)
    out_shape = (
        jax.ShapeDtypeStruct((NSA_GROUPS, n // tm, HPG * DH, tm), BF16),
        jax.ShapeDtypeStruct((NSA_GROUPS, n // tm, HPG * DH, tm), BF16),
        jax.ShapeDtypeStruct((4, n // CMP_STRIDE, CMP_STRIDE * DH), BF16),
        jax.ShapeDtypeStruct((NSA_GROUPS, n, 128), BF16),
        jax.ShapeDtypeStruct((NSA_GROUPS, n // tm, V_ROWS, tm), BF16),
        jax.ShapeDtypeStruct((NSA_GROUPS, n, DH), BF16),
        jax.ShapeDtypeStruct((NSA_GROUPS, n // tm, V_ROWS, tm), BF16),
        jax.ShapeDtypeStruct((n, LANE), F32),
        jax.ShapeDtypeStruct((n, 256), BF16),
        jax.ShapeDtypeStruct((n, 256), BF16),
        jax.ShapeDtypeStruct((n, 512), BF16),
        jax.ShapeDtypeStruct((n, 256), F32),
        jax.ShapeDtypeStruct((n, 512), BF16),
    )
    out_specs = (
        pl.BlockSpec((NSA_GROUPS, 1, HPG * DH, tm), lambda i: (0, i, 0, 0)),
        pl.BlockSpec((NSA_GROUPS, 1, HPG * DH, tm), lambda i: (0, i, 0, 0)),
        pl.BlockSpec((4, tm // CMP_STRIDE, CMP_STRIDE * DH), grp),
        pl.BlockSpec((NSA_GROUPS, tm, 128), grp),
        pl.BlockSpec((NSA_GROUPS, 1, V_ROWS, tm), lambda i: (0, i, 0, 0)),
        pl.BlockSpec((NSA_GROUPS, tm, DH), grp),
        pl.BlockSpec((NSA_GROUPS, 1, V_ROWS, tm), lambda i: (0, i, 0, 0)),
        pl.BlockSpec((tm, LANE), row),
        pl.BlockSpec((tm, 256), row), pl.BlockSpec((tm, 256), row), pl.BlockSpec((tm, 512), row),
        pl.BlockSpec((tm, 256), row), pl.BlockSpec((tm, 512), row),
    )
    in_specs = [
        pl.BlockSpec((tm, D_MODEL), row),
        pl.BlockSpec((1, D_MODEL), const),
        pl.BlockSpec((D_MODEL, SEG_END), const, pipeline_mode=pl.Buffered(1)),
        pl.BlockSpec((tm, LANE), tab),
        pl.BlockSpec((tm, LANE), tab),
        pl.BlockSpec((LANE, 256), const),
        pl.BlockSpec((1, 256), const),
    ]
    return pl.pallas_call(
        functools.partial(_in_proj_kernel, tm=tm, seq=seq),
        grid=(n // tm,),
        in_specs=in_specs, out_specs=out_specs, out_shape=out_shape,
        scratch_shapes=[pltpu.VMEM((2 * NSA_GROUPS, tm, DH), F32)],
        compiler_params=pltpu.CompilerParams(
            dimension_semantics=("parallel",), vmem_limit_bytes=VMEM_LIMIT),
        name="in_proj",
    )(x2d, gain, w_all, cos_t, sin_t, w2p, b2)


def _compress_kernel(x_ref, w1_ref, w2_ref, w2t_ref, pe_ref, o_ref, ot_ref, *, n_sub):
    half = CMP_STRIDE * DH
    x = x_ref[...]
    u = _dot(x, w1_ref[0:half, :])
    v = _dot(x, w1_ref[half:2 * half, :])
    c = _dot(pe_ref[...], w1_ref[...])[0:1, :]
    hid = u + pltpu.roll(v, shift=n_sub - 1, axis=0) + c
    hid = (hid * _sigmoid(hid)).astype(BF16)
    out = _dot(hid, w2_ref[...])
    rowi = lax.broadcasted_iota(jnp.int32, out.shape, 0)
    o_ref[...] = jnp.where(rowi < n_sub - 1, out, 0.0).astype(BF16)
    out_t = _dot_nt(w2t_ref[...], hid)
    coli = lax.broadcasted_iota(jnp.int32, out_t.shape, 1)
    ot_ref[0:DH, :] = jnp.where(coli < n_sub - 1, out_t, 0.0).astype(BF16)
    ot_ref[DH:V_ROWS, :] = jnp.ones((V_ROWS - DH, n_sub), BF16)


def _compress(src, w1, w2, w2t, pe, *, batch, n_sub):
    return pl.pallas_call(
        functools.partial(_compress_kernel, n_sub=n_sub),
        grid=(4, batch),
        in_specs=[
            pl.BlockSpec((None, None, n_sub, CMP_STRIDE * DH), lambda j, b: (j, b, 0, 0)),
            pl.BlockSpec((None, CMP_LEN * DH, CMP_HIDDEN), lambda j, b: (j // 2, 0, 0)),
            pl.BlockSpec((None, CMP_HIDDEN, DH), lambda j, b: (j // 2, 0, 0)),
            pl.BlockSpec((None, DH, CMP_HIDDEN), lambda j, b: (j // 2, 0, 0)),
            pl.BlockSpec((None, 8, CMP_LEN * DH), lambda j, b: (j // 2, 0, 0)),
        ],
        out_specs=(pl.BlockSpec((None, None, n_sub, DH), lambda j, b: (j, b, 0, 0)),
                   pl.BlockSpec((None, None, V_ROWS, n_sub), lambda j, b: (j, b, 0, 0))),
        out_shape=(jax.ShapeDtypeStruct((4, batch, n_sub, DH), BF16),
                   jax.ShapeDtypeStruct((4, batch, V_ROWS, n_sub), BF16)),
        compiler_params=pltpu.CompilerParams(
            dimension_semantics=("parallel", "parallel"), vmem_limit_bytes=VMEM_LIMIT),
        name="compress",
    )(src, w1, w2, w2t, pe)


def _softmax_tiles(score_fns, vts, carry, lead=MXU_LEAD):
    n = len(score_fns)
    scores = [fn() for fn in score_fns[:lead]] + [None] * max(n - lead, 0)
    out = []
    for j in range(n):
        m, acc = carry[j]
        m_new = jnp.maximum(m, jnp.max(scores[j], axis=0, keepdims=True))
        p = jnp.exp2((scores[j] - m_new).astype(BF16))
        out.append((m_new, jnp.exp2(m - m_new) * acc + _dot(vts[j], p)))
        if j + lead < n:
            scores[j + lead] = score_fns[j + lead]()
    return tuple(out)


def _nsa_kernel(qraw_ref, qrot_ref, kc_ref, vct_ref, ovt_ref, kaug_ref, vst_ref, kw_ref, vwt_ref,
                gate_ref, o_ref, qa_ref, gt_ref, *, tq, topk):
    i = pl.program_id(2)
    s0 = i * tq
    n_cmp = kc_ref.shape[0]
    n_sel = ovt_ref.shape[0]

    def init_carry():
        lane = lax.broadcasted_iota(jnp.int32, (1, tq), 1)
        one = (jnp.where(lane >= 0, NEG, 0.0), jnp.zeros((V_ROWS, tq), F32))
        return tuple(one for _ in range(HPG))

    def normalised(acc):
        return acc[0:DH] / acc[DH:DH + 1]

    kc = kc_ref[...]
    vct = vct_ref[...]
    cend = lax.broadcasted_iota(jnp.int32, (n_cmp, tq), 0) * CMP_STRIDE + (CMP_LEN - 1)
    cmask = cend <= s0 + lax.broadcasted_iota(jnp.int32, (n_cmp, tq), 1)
    scores = [_dot(kc, qraw_ref[h * DH:(h + 1) * DH, :]) for h in range(HPG)]
    expd, seen = [], []
    for s in scores:
        s = jnp.where(cmask, s, NEG)
        m = jnp.max(s, axis=0, keepdims=True)
        expd.append(jnp.exp2((s - m).astype(BF16)))
        seen.append(m > 0.5 * NEG)
    ovt = ovt_ref[...]
    oc_t, imp_t = [], 0.0
    for e, ok in zip(expd, seen):
        acc = _dot(vct, e)
        r = jnp.where(ok, 1.0 / acc[DH:DH + 1], 0.0)
        oc_t.append(acc[0:DH] * r)
        imp_t = imp_t + _dot(ovt, e) * r

    key_i = lax.broadcasted_iota(jnp.int32, (tq, tq), 0)
    qry_i = lax.broadcasted_iota(jnp.int32, (tq, tq), 1)
    n_back = WINDOW // tq
    w_tiles = [jnp.maximum(i - n_back + u, 0) for u in range(n_back)]
    w_keys = [kw_ref[pl.ds(pl.multiple_of(t * tq, tq), tq), :] for t in w_tiles]
    w_bands = [(i >= n_back - u) & ((qry_i < key_i) if u == 0 else True) for u in range(n_back)]
    w_scores = [lambda h=h: jnp.concatenate(
        [jnp.where(band, _dot(kt, qrot_ref[h * DH:(h + 1) * DH, :]), NEG)
         for kt, band in zip(w_keys, w_bands)], axis=0) for h in range(HPG)]
    w_vt = jnp.concatenate([vwt_ref[t] for t in w_tiles], axis=1)
    win_carry = _softmax_tiles(w_scores, [w_vt] * HPG, init_carry())

    jrow =lax.broadcasted_iota(jnp.int32, (n_sel, tq), 0)
    tlane = s0 + lax.broadcasted_iota(jnp.int32, (n_sel, tq), 1)
    cur = tlane // SLC_LEN
    valid = jrow * SLC_LEN <= tlane
    forced = (jrow == 0) | (jrow == cur) | (jrow == cur - 1)
    score = jnp.where(valid, jnp.where(forced, FORCE_SCORE, imp_t), -jnp.inf)

    jrow_f = jrow.astype(F32)
    work = score
    for _ in range(topk):
        top = jnp.max(work, axis=0, keepdims=True)
        first = jnp.min(jnp.where(work == top, jrow_f, float(n_sel)), axis=0, keepdims=True)
        work = jnp.where(jrow_f == first, -jnp.inf, work)
    bias_t = jnp.where(work != score, 0.0, NEG).astype(BF16)

    n_half = n_sel // SEL_HALF
    for a in range(n_half):
        for h in range(HPG):
            qa_ref[a, h, 0:DH, :] = qrot_ref[h * DH:(h + 1) * DH, :]
            qa_ref[a, h, DH:DH + SEL_HALF, :] = bias_t[a * SEL_HALF:(a + 1) * SEL_HALF, :]

    tiles_per_half = SEL_HALF * SLC_LEN // tq

    def sel_scores(first, count, diagonal=False):
        kt = kaug_ref[pl.ds(pl.multiple_of(first * tq, tq), count * tq), :]
        half = first // tiles_per_half

        def one(h):
            s = _dot(kt, qa_ref[half, h])
            return jnp.where(key_i <= qry_i, s, NEG) if diagonal else s

        return [functools.partial(one, h) for h in range(HPG)]

    def sel_step(first, count, carry):
        vt = vst_ref[first] if count == 1 else jnp.concatenate(
            [vst_ref[first + u] for u in range(count)], axis=1)
        return _softmax_tiles(sel_scores(first, count), [vt] * HPG, carry)

    n_groups = i // SEL_GROUP
    sel_carry = lax.fori_loop(
        0, n_groups, lambda t, c: sel_step(t * SEL_GROUP, SEL_GROUP, c), init_carry())
    sel_carry = lax.fori_loop(n_groups * SEL_GROUP, i, lambda t, c: sel_step(t, 1, c), sel_carry)
    kw_diag = kw_ref[pl.ds(pl.multiple_of(s0, tq), tq), :]
    win_diag = [lambda h=h: jnp.where(key_i <= qry_i, _dot(kw_diag, qa_ref[0, h, 0:DH, :]), NEG)
                for h in range(HPG)]
    carry = _softmax_tiles(sel_scores(i, 1, True) + win_diag,
                           [vst_ref[i]] * HPG + [vwt_ref[i]] * HPG, sel_carry + win_carry)
    os_t = [normalised(acc) for (_, acc) in carry[:HPG]]
    ow_t = [normalised(acc) for (_, acc) in carry[HPG:]]

    gt_ref[...] = gate_ref[...].T
    gt_t = gt_ref[pl.ds(pl.multiple_of(pl.program_id(1) * GATE_LANES, GATE_LANES), GATE_LANES), :]
    outs =[gt_t[h:h + 1, :] * oc_t[h] + gt_t[HPG + h:HPG + h + 1, :] * os_t[h]
            + gt_t[2 * HPG + h:2 * HPG + h + 1, :] * ow_t[h] for h in range(HPG)]
    o_ref[...] = jnp.concatenate(outs, axis=0).T.astype(BF16)


def _nsa(qraw_t, qrot_t, cmp_kv, cmp_kv_t, ovt, kaug, vslc_t, kwin, vwin_t, gates, *, batch, seq,
         tq=KEY_TILE):
    n = batch * seq
    nq = seq // tq
    nk = nq
    tk = tq
    n_cmp = cmp_kv.shape[2]
    n_sel = ovt.shape[0]
    topk = min(SLC_TOPK, seq // SLC_LEN)
    qmap = lambda b, g, i: (b * nq + i, g)
    qtmap = lambda b, g, i: (g, b * nq + i, 0, 0)
    kvmap = lambda b, g, i: (g, b, 0)
    vtmap = lambda b, g, i: (g, b, 0, 0)
    return pl.pallas_call(
        functools.partial(_nsa_kernel, tq=tq, topk=topk),
        grid=(batch, NSA_GROUPS, nq),
        in_specs=[
            pl.BlockSpec((None, None, HPG * DH, tq), qtmap),
            pl.BlockSpec((None, None, HPG * DH, tq), qtmap),
            pl.BlockSpec((None, None, n_cmp, DH), lambda b, g, i: (g, b, 0, 0)),
            pl.BlockSpec((None, None, V_ROWS, n_cmp), lambda b, g, i: (NSA_GROUPS + g, b, 0, 0)),
            pl.BlockSpec((n_sel, n_cmp), lambda b, g, i: (0, 0)),
            pl.BlockSpec((None, seq, 128), kvmap),
            pl.BlockSpec((None, nk, V_ROWS, tk), vtmap),
            pl.BlockSpec((None, seq, DH), kvmap),
            pl.BlockSpec((None, nk, V_ROWS, tk), vtmap),
            pl.BlockSpec((tq, LANE), lambda b, g, i: (b * nq + i, 0)),
        ],
        out_specs=pl.BlockSpec((tq, HPG * DH), qmap),
        out_shape=jax.ShapeDtypeStruct((n, NSA_HEADS * DH), BF16),
        scratch_shapes=[
            pltpu.VMEM((n_sel // SEL_HALF, HPG, DH + SEL_HALF, tq), BF16),
            pltpu.VMEM((LANE, tq), F32),
        ],
        compiler_params=pltpu.CompilerParams(
            dimension_semantics=("parallel", "parallel", "arbitrary"),
            vmem_limit_bytes=VMEM_LIMIT),
        name="nsa",
    )(qraw_t, qrot_t, cmp_kv, cmp_kv_t, ovt, kaug, vslc_t, kwin, vwin_t, gates)


def _gla_kernel(q_ref, k_ref, v_ref, g_ref, r_ref, gn_ref, o_ref, st_ref, *, tm, chunk):
    nb = q_ref.shape[0]
    n_chunk = tm // chunk
    n_sub = chunk // GLA_SUB
    dk_all = GLA_HEADS * GLA_DK
    dv_all = GLA_HEADS * GLA_DV

    @pl.when(pl.program_id(0) == 0)
    def _():
        st_ref[...] = jnp.zeros(st_ref.shape, F32)

    def iota(shape, axis):
        return lax.broadcasted_iota(jnp.int32, shape, axis)

    tril = jnp.where((iota((tm, tm), 0) >= iota((tm, tm), 1))
                     & (iota((tm, tm), 0) // chunk == iota((tm, tm), 1) // chunk), 1.0, 0.0).astype(BF16)
    k_head = iota((chunk, dk_all), 1) // GLA_DK
    v_head = iota((chunk, dv_all), 1) // GLA_DV
    a_rows = iota((n_sub * chunk, GLA_HEADS * chunk), 0)
    a_key = iota((n_sub * chunk, GLA_HEADS * chunk), 1) % chunk
    a_keep = (a_key // GLA_SUB == a_rows // chunk) & (a_key <= a_rows % chunk)
    st_keep = iota((dv_all, dk_all), 0) // GLA_DV == iota((dv_all, dk_all), 1) // GLA_DK
    gn = jnp.concatenate([gn_ref[...]] * GLA_HEADS, axis=1)

    work = []
    for b in range(nb):
        g = g_ref[b]
        g_hi = g.astype(BF16)
        g_lo = (g - g_hi.astype(F32)).astype(BF16)
        cum_all = _dot(tril, g_hi) + _dot(tril, g_lo)
        for c in range(n_chunk):
            rows = slice(c * chunk, (c + 1) * chunk)
            cum = cum_all[rows]
            q = q_ref[b, rows, :].astype(F32)
            k = k_ref[b, rows, :].astype(F32)
            v = v_ref[b, rows, :]
            last = cum[chunk - 1:chunk, :]
            refs = [cum[(j + 1) * GLA_SUB - 1:(j + 1) * GLA_SUB, :] for j in range(n_sub)]
            k_hat = jnp.concatenate(
                [k[j * GLA_SUB:(j + 1) * GLA_SUB] * jnp.exp(refs[j] - cum[j * GLA_SUB:(j + 1) * GLA_SUB])
                 for j in range(n_sub)], axis=0)
            q_hat = jnp.concatenate(
                [q * jnp.exp(jnp.minimum(cum - refs[j], 80.0)) for j in range(n_sub)], axis=0)
            k_stack = jnp.concatenate(
                [jnp.where(k_head == h, k_hat, 0.0) for h in range(GLA_HEADS)], axis=0)
            a_all = _dot_nt(q_hat.astype(BF16), k_stack.astype(BF16))
            k_dec = (k * jnp.exp(last - cum)).astype(BF16)
            v_t = v.astype(F32).T.astype(BF16)
            st_inc = _dot(v_t, k_dec)
            work.append(dict(b=b, rows=rows, a_all=a_all, st_inc=st_inc, v=v,
                             q_in=(q * jnp.exp(cum)).astype(BF16), decay=jnp.exp(last)))

    states = [st_ref[b] for b in range(nb)]
    for w in work:
        st = states[w["b"]]
        w["o_inter"] = _dot_nt(w["q_in"], st.astype(BF16))
        states[w["b"]] = st * w["decay"] + jnp.where(st_keep, w["st_inc"], 0.0)
    for b in range(nb):
        st_ref[b] = states[b]

    for w in work:
        a_m = jnp.where(a_keep, w["a_all"], 0.0)
        a = a_m[0:chunk]
        for j in range(1, n_sub):
            a = a + a_m[j * chunk:(j + 1) * chunk]
        v_blocks = jnp.concatenate(
            [jnp.where(v_head == h, w["v"], jnp.zeros_like(w["v"])) for h in range(GLA_HEADS)], axis=0)
        o = w["o_inter"] + _dot(a.astype(BF16), v_blocks)
        scale = jnp.concatenate(
            [jnp.broadcast_to(lax.rsqrt(jnp.mean(
                o[:, h * GLA_DV:(h + 1) * GLA_DV] ** 2, axis=-1, keepdims=True) + EPS), (chunk, GLA_DV))
             for h in range(GLA_HEADS)], axis=1)
        y = o * scale * gn * r_ref[w["b"], w["rows"], :].astype(F32)
        o_ref[w["b"], w["rows"], :] = y.astype(BF16)


def _gla(gq, gk, gv, ga, gr, gnorm, *, batch, seq, tm=KEY_TILE, chunk=GLA_CHUNK):
    tmap = lambda i: (0, i, 0)
    dk_all = GLA_HEADS * GLA_DK
    dv_all = GLA_HEADS * GLA_DV
    return pl.pallas_call(
        functools.partial(_gla_kernel, tm=tm, chunk=chunk),
        grid=(seq // tm,),
        in_specs=[
            pl.BlockSpec((batch, tm, dk_all), tmap),
            pl.BlockSpec((batch, tm, dk_all), tmap),
            pl.BlockSpec((batch, tm, dv_all), tmap),
            pl.BlockSpec((batch, tm, dk_all), tmap),
            pl.BlockSpec((batch, tm, dv_all), tmap),
            pl.BlockSpec((1, GLA_DV), lambda i: (0, 0)),
        ],
        out_specs=pl.BlockSpec((batch, tm, dv_all), tmap),
        out_shape=jax.ShapeDtypeStruct((batch, seq, dv_all), BF16),
        scratch_shapes=[pltpu.VMEM((batch, dv_all, dk_all), F32)],
        compiler_params=pltpu.CompilerParams(
            dimension_semantics=("arbitrary",), vmem_limit_bytes=VMEM_LIMIT),
        name="gla",
    )(gq, gk, gv, ga, gr, gnorm)


def _tail_kernel(x_ref, oa_ref, ob_ref, g0_ref, wm_ref, wn_ref, wb_ref, wo_ref,
                 g1_ref, wg_ref, wu_ref, wd_ref, g2_ref, o_ref):
    x = x_ref[...]

    def rms(v, gain_ref):
        return v * lax.rsqrt(jnp.mean(v * v, axis=-1, keepdims=True) + EPS) * gain_ref[...]

    gates = _sigmoid(_dot(rms(x, g0_ref).astype(BF16), wm_ref[...]))
    merged = (gates[:, 0:D_MODEL] * _dot(oa_ref[...], wn_ref[...])
              + gates[:, D_MODEL:2 * D_MODEL] * _dot(ob_ref[...], wb_ref[...]))
    x1 = x + _dot(merged.astype(BF16), wo_ref[...])
    h = rms(x1, g1_ref).astype(BF16)
    a = _dot(h, wg_ref[...])
    u = _dot(h, wu_ref[...])
    y = x1 + _dot((a * _sigmoid(a) * u).astype(BF16), wd_ref[...])
    o_ref[...] = rms(y, g2_ref)


def _tail(x2d, oa, ob, g0, wm, wn, wb, wo, g1, wg, wu, wd, g2, *, tm=256):
    n = x2d.shape[0]
    row = lambda i: (i, 0)
    const = lambda i: (0, 0)
    weight = lambda r, c: pl.BlockSpec((r, c), const, pipeline_mode=pl.Buffered(1))
    return pl.pallas_call(
        _tail_kernel,
        grid=(n // tm,),
        in_specs=[
            pl.BlockSpec((tm, D_MODEL), row),
            pl.BlockSpec((tm, NSA_HEADS * DH), row),
            pl.BlockSpec((tm, GLA_HEADS * GLA_DV), row),
            weight(1, D_MODEL), weight(D_MODEL, 2 * D_MODEL),
            weight(NSA_HEADS * DH, D_MODEL), weight(GLA_HEADS * GLA_DV, D_MODEL),
            weight(D_MODEL, D_MODEL),
            weight(1, D_MODEL), weight(D_MODEL, D_FF), weight(D_MODEL, D_FF), weight(D_FF, D_MODEL),
            weight(1, D_MODEL),
        ],
        out_specs=pl.BlockSpec((tm, D_MODEL), row),
        out_shape=jax.ShapeDtypeStruct((n, D_MODEL), F32),
        compiler_params=pltpu.CompilerParams(
            dimension_semantics=("parallel",), vmem_limit_bytes=VMEM_LIMIT),
        name="tail",
    )(x2d, oa, ob, g0, wm, wn, wb, wo, g1, wg, wu, wd, g2)


def _rope_tables(seq):
    half = ROPE_DIM // 2
    inv_freq = ROPE_THETA ** (-jnp.arange(half, dtype=F32) / half)
    dim = np.arange(LANE) % DH
    freq = jnp.where(dim < ROPE_DIM, inv_freq[dim % half], 0.0)
    sign = np.where(dim < half, -1.0, np.where(dim < ROPE_DIM, 1.0, 0.0)).astype(np.float32)
    ang = jnp.arange(seq).astype(F32)[:, None] * freq[None, :]
    return jnp.cos(ang), jnp.sin(ang) * sign[None, :]


def _overlap_t(n_sel, n_cmp_pad, n_slc, n_cmp):
    c_start = np.arange(n_cmp_pad)[None, :] * CMP_STRIDE
    s_start = np.arange(n_sel)[:, None] * SLC_LEN
    ov = (c_start < s_start + SLC_LEN) & (c_start + CMP_LEN > s_start)
    ov &= (np.arange(n_cmp_pad)[None, :] < n_cmp) & (np.arange(n_sel)[:, None] < n_slc)
    return jnp.asarray(ov, dtype=BF16)


def _pack_in_weights(w_in):
    w_in = w_in.astype(BF16)
    c = 0
    wq = w_in[:, c:c + 512]; c += 512
    wkv = w_in[:, c:c + 768]; c += 768
    wgate = w_in[:, c:c + 24]; c += 24
    wgq = w_in[:, c:c + 256]; c += 256
    wgk = w_in[:, c:c + 256]; c += 256
    wgv = w_in[:, c:c + 512]; c += 512
    wlr = w_in[:, c:c + GLA_RANK]; c += GLA_RANK
    wgr = w_in[:, c:c + 512]; c += 512
    wmg = w_in[:, c:c + 2048]
    d = w_in.shape[0]
    wg3 = wgate.reshape(d, NSA_GROUPS, HPG, 3).transpose(0, 1, 3, 2).reshape(d, NSA_GROUPS, 3 * HPG)
    wg_pad = jnp.pad(wg3, ((0, 0), (0, 0), (0, GATE_LANES - 3 * HPG))).reshape(d, NSA_GROUPS * GATE_LANES)
    misc = jnp.concatenate([
        wg_pad, jnp.zeros((d, LR_LANE - NSA_GROUPS * GATE_LANES), BF16),
        wlr, jnp.zeros((d, LANE - LR_LANE - GLA_RANK), BF16)], axis=1)
    w_all = jnp.concatenate([wq, wkv, misc, wgq, wgk, wgv, wgr], axis=1)
    return w_all, wmg


def kernel(x, norm_mix, w_in, cmp_pe_k, cmp_pe_v, cmp_k_w1, cmp_k_w2, cmp_v_w1, cmp_v_w2,
           gla_gate_w2, gla_gate_b, gla_norm, w_up_nsa, w_up_gla, w_out, norm_ffn,
           w_ffn_gate, w_ffn_up, w_ffn_down, norm_final):
    batch, seq, d = x.shape
    assert d == D_MODEL and w_in.shape[0] == 1
    assert seq % KEY_TILE == 0
    n = batch * seq
    x2d = x.reshape(n, d)

    w_all, w_merge = _pack_in_weights(w_in[0])
    assert w_all.shape[1] == SEG_END
    cos_t, sin_t = _rope_tables(seq)
    w2p = jnp.pad(gla_gate_w2[0], ((LR_LANE, LANE - LR_LANE - GLA_RANK), (0, 0))).astype(BF16)
    b2 = gla_gate_b[0].reshape(1, -1)
    gain_mix = norm_mix[0].reshape(1, d)

    (qraw, qrot, cmp_src, kaug, vslc_t, kwin, vwin_t, gates, gq, gk, gv, ga, gr) = _in_proj(
        x2d, gain_mix, w_all, cos_t, sin_t, w2p, b2, seq=seq)

    n_sub = seq // CMP_STRIDE
    n_cmp = n_sub - CMP_LEN // CMP_STRIDE + 1
    n_slc = seq // SLC_LEN
    cmp_in = cmp_src.reshape(4, batch, n_sub, CMP_STRIDE * DH)
    w1 = jnp.stack([cmp_k_w1[0], cmp_v_w1[0]]).astype(BF16)
    w2 = jnp.stack([cmp_k_w2[0], cmp_v_w2[0]]).astype(BF16)
    pe = jnp.stack([cmp_pe_k[0].reshape(1, -1), cmp_pe_v[0].reshape(1, -1)])
    pe = jnp.broadcast_to(pe, (2, 8, CMP_LEN * DH)).astype(BF16)
    w2t = jnp.stack([cmp_k_w2[0].T, cmp_v_w2[0].T]).astype(BF16)
    cmp_kv, cmp_kv_t = _compress(cmp_in, w1, w2, w2t, pe, batch=batch, n_sub=n_sub)

    n_sel = -(-n_slc // SEL_HALF) * SEL_HALF
    ovt = _overlap_t(n_sel, n_sub, n_slc, n_cmp)
    o_a = _nsa(qraw, qrot, cmp_kv, cmp_kv_t, ovt, kaug, vslc_t, kwin, vwin_t, gates,
               batch=batch, seq=seq)

    per_seq = lambda a: a.reshape(batch, seq, a.shape[-1])
    o_b = _gla(per_seq(gq), per_seq(gk), per_seq(gv), per_seq(ga), per_seq(gr),
               gla_norm[0].reshape(1, -1), batch=batch, seq=seq).reshape(n, -1)

    out = _tail(x2d, o_a, o_b, gain_mix, w_merge, w_up_nsa[0].astype(BF16), w_up_gla[0].astype(BF16),
                w_out[0].astype(BF16), norm_ffn[0].reshape(1, d), w_ffn_gate[0].astype(BF16),
                w_ffn_up[0].astype(BF16), w_ffn_down[0].astype(BF16), norm_final.reshape(1, d))
    return out.reshape(batch, seq, d)
```

```python
import functools

import numpy as np
import jax
import jax.numpy as jnp
from jax import lax
from jax.experimental import pallas as pl
from jax.experimental.pallas import tpu as pltpu

F32 = jnp.float32
BF16 = jnp.bfloat16

D_MODEL = 1024
NSA_HEADS = 8
NSA_GROUPS = 2
HPG = NSA_HEADS // NSA_GROUPS
DH = 64
CMP_LEN = 32
CMP_STRIDE = 16
CMP_HIDDEN = 128
SLC_LEN = 64
SLC_TOPK = 16
WINDOW = 512
FORCE_SCORE = 1.0e4
GLA_HEADS = 4
GLA_DK = 64
GLA_DV = 128
GLA_RANK = 16
GLA_TAU = 16.0
GLA_CHUNK = 64
GLA_SUB = 16
ROPE_THETA = 500000.0
ROPE_DIM = DH // 4
D_FF = 2816
EPS = 1e-6
NEG = -1e30
OVERFLOW_GUARD = 1e37
LOG2E = 1.4426950408889634
KEY_TILE = 256

LANE = 128
VMEM_LIMIT = 56 * 1024 * 1024
SEL_HALF = 64
V_ROWS = DH + 16
MXU_LEAD = 8
SEL_GROUP = 4

SEG_Q = 0
SEG_KV = SEG_Q + 512
SEG_MISC = SEG_KV + 768
SEG_GLA = SEG_MISC + 128
SEG_END = SEG_GLA + 1536
GATE_LANES = 16
LR_LANE = 64


def _dot(a, b):
    return jnp.dot(a, b, preferred_element_type=F32)


def _dot_nt(a, b):
    return lax.dot_general(a, b, (((1,), (1,)), ((), ())), preferred_element_type=F32)


def _sigmoid(x):
    return 1.0 / (1.0 + jnp.exp(-x))


def _in_proj_kernel(x_ref, gain_ref, w_ref, cos_ref, sin_ref, w2_ref, b2_ref,
                    qraw_ref, qrot_ref, cmp_ref, kaug_ref, vslc_ref, kwin_ref, vwin_ref,
                    gate_ref, gq_ref, gk_ref, gv_ref, ga_ref, gr_ref, flat_ref, *, tm, seq):
    i = pl.program_id(0)
    x = x_ref[...]
    ms = jnp.mean(x * x, axis=-1, keepdims=True)
    h = (x * lax.rsqrt(ms + EPS) * gain_ref[...]).astype(BF16)

    def proj(a, b):
        return _dot(h, w_ref[:, a:b])

    cos = cos_ref[...]
    sin = sin_ref[...]

    def rope(v):
        width = v.shape[1]
        half = ROPE_DIM // 2
        first = lax.broadcasted_iota(jnp.int32, v.shape, 1) % DH < half
        partner = jnp.where(first, pltpu.roll(v, width - half, axis=1), pltpu.roll(v, half, axis=1))
        reps = width // LANE
        c = cos if reps == 1 else jnp.concatenate([cos] * reps, axis=1)
        s = sin if reps == 1 else jnp.concatenate([sin] * reps, axis=1)
        return v * c + partner * s

    scale = DH ** -0.5
    q = proj(SEG_Q, SEG_Q + 512)
    qraw_t = (q * (scale * LOG2E)).T
    qrot_t = (rope(q) * (scale * LOG2E)).T
    gw = HPG * DH
    for g in range(NSA_GROUPS):
        qraw_ref[g, 0] = qraw_t[g * gw:(g + 1) * gw, :].astype(BF16)
        qrot_ref[g, 0] = qrot_t[g * gw:(g + 1) * gw, :].astype(BF16)

    kv = proj(SEG_KV, SEG_KV + 768)
    n_rows = tm // CMP_STRIDE
    for j in range(2):
        for g in range(NSA_GROUPS):
            flat_ref[2 * j + g] = kv[:, j * 128 + g * DH:j * 128 + (g + 1) * DH]
    for jg in range(2 * NSA_GROUPS):
        for l in range(0, CMP_STRIDE, 2):
            pair = jnp.concatenate(
                [flat_ref[jg, pl.ds(l, n_rows, stride=CMP_STRIDE), :],
                 flat_ref[jg, pl.ds(l + 1, n_rows, stride=CMP_STRIDE), :]], axis=1)
            cmp_ref[jg, :, l * DH:(l + 2) * DH] = pair.astype(BF16)
    kslc = rope(kv[:, 256:384])
    kwin = rope(kv[:, 512:640])
    vslc_t = kv[:, 384:512].T
    vwin_t = kv[:, 640:768].T
    pos = (i * tm) % seq + lax.broadcasted_iota(jnp.int32, (tm, SEL_HALF), 0)
    blk = (pos // SLC_LEN) % SEL_HALF
    onehot = jnp.where(lax.broadcasted_iota(jnp.int32, (tm, SEL_HALF), 1) == blk, 1.0, 0.0)
    ones_rows = jnp.ones((V_ROWS - DH, tm), F32)
    for g in range(NSA_GROUPS):
        sl = slice(g * DH, (g + 1) * DH)
        kaug_ref[g] =jnp.concatenate([kslc[:, sl], onehot], axis=1).astype(BF16)
        vslc_ref[g, 0] = jnp.concatenate([vslc_t[sl, :], ones_rows], axis=0).astype(BF16)
        kwin_ref[g] = kwin[:, sl].astype(BF16)
        vwin_ref[g, 0] = jnp.concatenate([vwin_t[sl, :], ones_rows], axis=0).astype(BF16)

    misc = proj(SEG_MISC, SEG_MISC + 128)
    gate_ref[...] = _sigmoid(misc)

    gla = proj(SEG_GLA, SEG_GLA + 1536)
    gq_ref[...] = (gla[:, 0:256] * (GLA_DK ** -0.5)).astype(BF16)
    gk_ref[...] = gla[:, 256:512].astype(BF16)
    gv_ref[...] = gla[:, 512:1024].astype(BF16)
    r = gla[:, 1024:1536]
    gr_ref[...] = (r * _sigmoid(r)).astype(BF16)

    z = _dot(misc.astype(BF16), w2_ref[...]) + b2_ref[...]
    log_sig = jnp.minimum(z, 0.0) - jnp.log1p(jnp.exp(-jnp.abs(z)))
    ga_ref[...] = log_sig * (1.0 / GLA_TAU)


def _in_proj(x2d, gain, w_all, cos_t, sin_t, w2p, b2, *, seq, tm=KEY_TILE):
    n = x2d.shape[0]
    nt = seq // tm
    row = lambda i: (i, 0)
    grp = lambda i: (0, i, 0)
    const = lambda i: (0, 0)
    tab = lambda i: (i % nt, 0)
    out_shape = (
        jax.ShapeDtypeStruct((NSA_GROUPS, n // tm, HPG * DH, tm), BF16),
        jax.ShapeDtypeStruct((NSA_GROUPS, n // tm, HPG * DH, tm), BF16),
        jax.ShapeDtypeStruct((4, n // CMP_STRIDE, CMP_STRIDE * DH), BF16),
        jax.ShapeDtypeStruct((NSA_GROUPS, n, 128), BF16),
        jax.ShapeDtypeStruct((NSA_GROUPS, n // tm, V_ROWS, tm), BF16),
        jax.ShapeDtypeStruct((NSA_GROUPS, n, DH), BF16),
        jax.ShapeDtypeStruct((NSA_GROUPS, n // tm, V_ROWS, tm), BF16),
        jax.ShapeDtypeStruct((n, LANE), F32),
        jax.ShapeDtypeStruct((n, 256), BF16),
        jax.ShapeDtypeStruct((n, 256), BF16),
        jax.ShapeDtypeStruct((n, 512), BF16),
        jax.ShapeDtypeStruct((n, 256), F32),
        jax.ShapeDtypeStruct((n, 512), BF16),
    )
    out_specs = (
        pl.BlockSpec((NSA_GROUPS, 1, HPG * DH, tm), lambda i: (0, i, 0, 0)),
        pl.BlockSpec((NSA_GROUPS, 1, HPG * DH, tm), lambda i: (0, i, 0, 0)),
        pl.BlockSpec((4, tm // CMP_STRIDE, CMP_STRIDE * DH), grp),
        pl.BlockSpec((NSA_GROUPS, tm, 128), grp),
        pl.BlockSpec((NSA_GROUPS, 1, V_ROWS, tm), lambda i: (0, i, 0, 0)),
        pl.BlockSpec((NSA_GROUPS, tm, DH), grp),
        pl.BlockSpec((NSA_GROUPS, 1, V_ROWS, tm), lambda i: (0, i, 0, 0)),
        pl.BlockSpec((tm, LANE), row),
        pl.BlockSpec((tm, 256), row), pl.BlockSpec((tm, 256), row), pl.BlockSpec((tm, 512), row),
        pl.BlockSpec((tm, 256), row), pl.BlockSpec((tm, 512), row),
    )
    in_specs = [
        pl.BlockSpec((tm, D_MODEL), row),
        pl.BlockSpec((1, D_MODEL), const),
        pl.BlockSpec((D_MODEL, SEG_END), const, pipeline_mode=pl.Buffered(1)),
        pl.BlockSpec((tm, LANE), tab),
        pl.BlockSpec((tm, LANE), tab),
        pl.BlockSpec((LANE, 256), const),
        pl.BlockSpec((1, 256), const),
    ]
    return pl.pallas_call(
        functools.partial(_in_proj_kernel, tm=tm, seq=seq),
        grid=(n // tm,),
        in_specs=in_specs, out_specs=out_specs, out_shape=out_shape,
        scratch_shapes=[pltpu.VMEM((2 * NSA_GROUPS, tm, DH), F32)],
        compiler_params=pltpu.CompilerParams(
            dimension_semantics=("parallel",), vmem_limit_bytes=VMEM_LIMIT),
        name="in_proj",
    )(x2d, gain, w_all, cos_t, sin_t, w2p, b2)


def _compress_kernel(x_ref, w1_ref, w2_ref, w2t_ref, pe_ref, o_ref, ot_ref, *, n_sub):
    half = CMP_STRIDE * DH
    x = x_ref[...]
    u = _dot(x, w1_ref[0:half, :])
    v = _dot(x, w1_ref[half:2 * half, :])
    c = _dot(pe_ref[...], w1_ref[...])[0:1, :]
    hid = u + pltpu.roll(v, shift=n_sub - 1, axis=0) + c
    hid = (hid * _sigmoid(hid)).astype(BF16)
    out = _dot(hid, w2_ref[...])
    rowi = lax.broadcasted_iota(jnp.int32, out.shape, 0)
    o_ref[...] = jnp.where(rowi < n_sub - 1, out, 0.0).astype(BF16)
    out_t = _dot_nt(w2t_ref[...], hid)
    coli = lax.broadcasted_iota(jnp.int32, out_t.shape, 1)
    ot_ref[0:DH, :] = jnp.where(coli < n_sub - 1, out_t, 0.0).astype(BF16)
    ot_ref[DH:V_ROWS, :] = jnp.ones((V_ROWS - DH, n_sub), BF16)


def _compress(src, w1, w2, w2t, pe, *, batch, n_sub):
    return pl.pallas_call(
        functools.partial(_compress_kernel, n_sub=n_sub),
        grid=(4, batch),
        in_specs=[
            pl.BlockSpec((None, None, n_sub, CMP_STRIDE * DH), lambda j, b: (j, b, 0, 0)),
            pl.BlockSpec((None, CMP_LEN * DH, CMP_HIDDEN), lambda j, b: (j // 2, 0, 0)),
            pl.BlockSpec((None, CMP_HIDDEN, DH), lambda j, b: (j // 2, 0, 0)),
            pl.BlockSpec((None, DH, CMP_HIDDEN), lambda j, b: (j // 2, 0, 0)),
            pl.BlockSpec((None, 8, CMP_LEN * DH), lambda j, b: (j // 2, 0, 0)),
        ],
        out_specs=(pl.BlockSpec((None, None, n_sub, DH), lambda j, b: (j, b, 0, 0)),
                   pl.BlockSpec((None, None, V_ROWS, n_sub), lambda j, b: (j, b, 0, 0))),
        out_shape=(jax.ShapeDtypeStruct((4, batch, n_sub, DH), BF16),
                   jax.ShapeDtypeStruct((4, batch, V_ROWS, n_sub), BF16)),
        compiler_params=pltpu.CompilerParams(
            dimension_semantics=("parallel", "parallel"), vmem_limit_bytes=VMEM_LIMIT),
        name="compress",
    )(src, w1, w2, w2t, pe)


def _softmax_tiles(score_fns, vts, carry, lead=MXU_LEAD):
    n = len(score_fns)
    scores = [fn() for fn in score_fns[:lead]] + [None] * max(n - lead, 0)
    out = []
    for j in range(n):
        m, acc = carry[j]
        m_new = jnp.maximum(m, jnp.max(scores[j], axis=0, keepdims=True))
        p = jnp.exp2((scores[j] - m_new).astype(BF16))
        out.append((m_new, jnp.exp2(m - m_new) * acc + _dot(vts[j], p)))
        if j + lead < n:
            scores[j + lead] = score_fns[j + lead]()
    return tuple(out)


def _nsa_kernel(qraw_ref, qrot_ref, kc_ref, vct_ref, ovt_ref, kaug_ref, vst_ref, kw_ref, vwt_ref,
                gate_ref, o_ref, qa_ref, gt_ref, bias_ref, *, tq, topk):
    i = pl.program_id(2)
    s0 = i * tq
    n_cmp = kc_ref.shape[0]
    n_sel = ovt_ref.shape[0]

    def init_carry():
        lane = lax.broadcasted_iota(jnp.int32, (1, tq), 1)
        one = (jnp.where(lane >= 0, NEG, 0.0), jnp.zeros((V_ROWS, tq), F32))
        return tuple(one for _ in range(HPG))

    def normalised(acc):
        return acc[0:DH] / acc[DH:DH + 1]

    def compressed(n_keys):
        kc = kc_ref[0:n_keys, :]
        cend = lax.broadcasted_iota(jnp.int32, (n_keys, tq), 0) * CMP_STRIDE + (CMP_LEN - 1)
        cmask = cend <= s0 + lax.broadcasted_iota(jnp.int32, (n_keys, tq), 1)
        scores = [_dot(kc, qraw_ref[h * DH:(h + 1) * DH, :]) for h in range(HPG)]
        expd, seen = [], []
        for s in scores:
            s = jnp.where(cmask, s, NEG)
            m = jnp.max(s, axis=0, keepdims=True)
            expd.append(jnp.exp2((s - m).astype(BF16)))
            seen.append(m > 0.5 * NEG)
        vct = vct_ref[:, 0:n_keys]
        ovt = ovt_ref[:, 0:n_keys]
        outs, imp = [], 0.0
        for e, ok in zip(expd, seen):
            acc = _dot(vct, e)
            r = jnp.where(ok, 1.0 / acc[DH:DH + 1], 0.0)
            outs.append(acc[0:DH] * r)
            imp = imp + _dot(ovt, e) * r
        return outs, imp

    oc_t, imp_t = compressed(n_cmp)

    key_i = lax.broadcasted_iota(jnp.int32, (tq, tq), 0)
    qry_i = lax.broadcasted_iota(jnp.int32, (tq, tq), 1)
    causal = key_i <= qry_i
    n_back = WINDOW // tq
    w_tiles = [jnp.maximum(i - n_back + u, 0) for u in range(n_back)] + [i]
    w_keys = [kw_ref[pl.ds(pl.multiple_of(t * tq, tq), tq), :] for t in w_tiles]
    w_bands = [(i >= n_back - u) & ((qry_i < key_i) if u == 0 else True) for u in range(n_back)]
    w_bands.append(causal)
    w_scores = [lambda h=h: jnp.concatenate(
        [jnp.where(band, _dot(kt, qrot_ref[h * DH:(h + 1) * DH, :]), NEG)
         for kt, band in zip(w_keys, w_bands)], axis=0) for h in range(HPG)]
    w_vt = jnp.concatenate([vwt_ref[t] for t in w_tiles], axis=1)
    ow_t = [normalised(acc) for (_, acc) in _softmax_tiles(w_scores, [w_vt] * HPG, init_carry())]

    jrow = lax.broadcasted_iota(jnp.int32, (n_sel, tq), 0)
    tlane = s0 + lax.broadcasted_iota(jnp.int32, (n_sel, tq), 1)
    cur = tlane // SLC_LEN
    valid = jrow * SLC_LEN <= tlane
    forced = (jrow == 0) | (jrow == cur) | (jrow == cur - 1)
    score = jnp.where(valid, jnp.where(forced, FORCE_SCORE, imp_t), -jnp.inf)

    jrow_f = jrow.astype(F32)
    work = score
    for _ in range(topk):
        top = jnp.max(work, axis=0, keepdims=True)
        first = jnp.min(jnp.where(work == top, jrow_f, float(n_sel)), axis=0, keepdims=True)
        work = jnp.where(jrow_f == first, -jnp.inf, work)
    bias_t = jnp.where(work != score, 0.0, NEG).astype(BF16)

    n_half = n_sel // SEL_HALF
    tiles_per_half = SEL_HALF * SLC_LEN // tq
    bias_ref[...] = bias_t

    def write_weights(shifts):
        for a in range(n_half):
            rows = bias_t[a * SEL_HALF:(a + 1) * SEL_HALF, :]
            for h in range(HPG):
                qa_ref[a, h, 0:DH, :] = qrot_ref[h * DH:(h + 1) * DH, :]
                qa_ref[a, h, DH:DH + SEL_HALF, :] = (
                    rows if shifts is None else (rows.astype(F32) - shifts[h]).astype(BF16))

    def key_tiles(first, count):
        return kaug_ref[pl.ds(pl.multiple_of(first * tq, tq), count * tq), :]

    def value_tiles(first, count):
        return vst_ref[first] if count == 1 else jnp.concatenate(
            [vst_ref[first + u] for u in range(count)], axis=1)

    diag_bias = bias_ref[pl.ds(pl.multiple_of((i // tiles_per_half) * SEL_HALF, SEL_HALF), SEL_HALF), :]
    kt_diag = key_tiles(i, 1)
    diag_scores = [jnp.where(causal, _dot(kt_diag, jnp.concatenate(
        [qrot_ref[h * DH:(h + 1) * DH, :], diag_bias], axis=0)), NEG) for h in range(HPG)]
    shifts = [jnp.max(s, axis=0, keepdims=True).astype(BF16).astype(F32) for s in diag_scores]
    vt_diag = vst_ref[i]
    accs = tuple(_dot(vt_diag, jnp.exp2((s - c).astype(BF16))) for s, c in zip(diag_scores, shifts))
    write_weights(shifts)

    def fast_step(first, count, accs):
        kt = key_tiles(first, count)
        vt = value_tiles(first, count)
        half = first // tiles_per_half
        scores = [_dot(kt, qa_ref[half, h]) for h in range(HPG)]
        return tuple(acc + _dot(vt, jnp.exp2(s.astype(BF16))) for s, acc in zip(scores, accs))

    n_groups = i // SEL_GROUP
    accs = lax.fori_loop(0, n_groups, lambda t, a: fast_step(t * SEL_GROUP, SEL_GROUP, a), accs)
    leftovers = [lambda a: a] + [
        functools.partial(lambda a, k: fast_step(n_groups * SEL_GROUP, k, a), k=k)
        for k in range(1, SEL_GROUP)]
    accs = lax.switch(i - n_groups * SEL_GROUP, leftovers, accs)

    def rescaled_sweep():
        write_weights(None)

        def tile_scores(t, diagonal):
            kt = key_tiles(t, 1)
            half = t // tiles_per_half
            return [functools.partial(
                lambda h: jnp.where(causal, _dot(kt, qa_ref[half, h]), NEG) if diagonal
                else _dot(kt, qa_ref[half, h]), h) for h in range(HPG)]

        carry = lax.fori_loop(
            0, i, lambda t, c: _softmax_tiles(tile_scores(t, False), [vst_ref[t]] * HPG, c),
            init_carry())
        carry = _softmax_tiles(tile_scores(i, True), [vt_diag] * HPG, carry)
        return tuple(normalised(acc) for (_, acc) in carry)

    denominators = jnp.concatenate([acc[DH:DH + 1] for acc in accs], axis=0)
    no_overflow = jnp.max(denominators) < OVERFLOW_GUARD
    os_t = lax.cond(no_overflow, lambda: tuple(normalised(acc) for acc in accs), rescaled_sweep)

    gt_ref[...] = gate_ref[...].T
    gt_t = gt_ref[pl.ds(pl.multiple_of(pl.program_id(1) * GATE_LANES, GATE_LANES), GATE_LANES), :]
    outs = [gt_t[h:h + 1, :] * oc_t[h] + gt_t[HPG + h:HPG + h + 1, :] * os_t[h]
            + gt_t[2 * HPG + h:2 * HPG + h + 1, :] * ow_t[h] for h in range(HPG)]
    o_ref[...] = jnp.concatenate(outs, axis=0).T.astype(BF16)


def _nsa(qraw_t, qrot_t, cmp_kv, cmp_kv_t, ovt, kaug, vslc_t, kwin, vwin_t, gates, *, batch, seq,
         tq=KEY_TILE):
    n = batch * seq
    nq = seq // tq
    nk = nq
    tk = tq
    n_cmp = cmp_kv.shape[2]
    n_sel = ovt.shape[0]
    topk = min(SLC_TOPK, seq // SLC_LEN)
    qmap = lambda b, g, i: (b * nq + i, g)
    qtmap = lambda b, g, i: (g, b * nq + i, 0, 0)
    kvmap = lambda b, g, i: (g, b, 0)
    vtmap = lambda b, g, i: (g, b, 0, 0)
    return pl.pallas_call(
        functools.partial(_nsa_kernel, tq=tq, topk=topk),
        grid=(batch, NSA_GROUPS, nq),
        in_specs=[
            pl.BlockSpec((None, None, HPG * DH, tq), qtmap),
            pl.BlockSpec((None, None, HPG * DH, tq), qtmap),
            pl.BlockSpec((None, None, n_cmp, DH), lambda b, g, i: (g, b, 0, 0)),
            pl.BlockSpec((None, None, V_ROWS, n_cmp), lambda b, g, i: (NSA_GROUPS + g, b, 0, 0)),
            pl.BlockSpec((n_sel, n_cmp), lambda b, g, i: (0, 0)),
            pl.BlockSpec((None, seq, 128), kvmap),
            pl.BlockSpec((None, nk, V_ROWS, tk), vtmap),
            pl.BlockSpec((None, seq, DH), kvmap),
            pl.BlockSpec((None, nk, V_ROWS, tk), vtmap),
            pl.BlockSpec((tq, LANE), lambda b, g, i: (b * nq + i, 0)),
        ],
        out_specs=pl.BlockSpec((tq, HPG * DH), qmap),
        out_shape=jax.ShapeDtypeStruct((n, NSA_HEADS * DH), BF16),
        scratch_shapes=[
            pltpu.VMEM((n_sel // SEL_HALF, HPG, DH + SEL_HALF, tq), BF16),
            pltpu.VMEM((LANE, tq), F32),
            pltpu.VMEM((n_sel, tq), BF16),
        ],
        compiler_params=pltpu.CompilerParams(
            dimension_semantics=("parallel", "parallel", "arbitrary"),
            vmem_limit_bytes=VMEM_LIMIT),
        name="nsa",
    )(qraw_t, qrot_t, cmp_kv, cmp_kv_t, ovt, kaug, vslc_t, kwin, vwin_t, gates)


def _gla_kernel(q_ref, k_ref, v_ref, g_ref, r_ref, gn_ref, o_ref, st_ref, *, tm, chunk):
    nb = q_ref.shape[0]
    n_chunk = tm // chunk
    n_sub = chunk // GLA_SUB
    dk_all = GLA_HEADS * GLA_DK
    dv_all = GLA_HEADS * GLA_DV

    @pl.when(pl.program_id(0) == 0)
    def _():
        st_ref[...] = jnp.zeros(st_ref.shape, F32)

    def iota(shape, axis):
        return lax.broadcasted_iota(jnp.int32, shape, axis)

    tril = jnp.where((iota((tm, tm), 0) >= iota((tm, tm), 1))
                     & (iota((tm, tm), 0) // chunk == iota((tm, tm), 1) // chunk), 1.0, 0.0).astype(BF16)
    k_head = iota((chunk, dk_all), 1) // GLA_DK
    v_head = iota((chunk, dv_all), 1) // GLA_DV
    a_rows = iota((n_sub * chunk, GLA_HEADS * chunk), 0)
    a_key = iota((n_sub * chunk, GLA_HEADS * chunk), 1) % chunk
    a_keep = (a_key // GLA_SUB == a_rows // chunk) & (a_key <= a_rows % chunk)
    st_keep = iota((dv_all, dk_all), 0) // GLA_DV == iota((dv_all, dk_all), 1) // GLA_DK
    gn = jnp.concatenate([gn_ref[...]] * GLA_HEADS, axis=1)

    work = []
    for b in range(nb):
        g = g_ref[b]
        g_hi = g.astype(BF16)
        g_lo = (g - g_hi.astype(F32)).astype(BF16)
        cum_all = _dot(tril, g_hi) + _dot(tril, g_lo)
        for c in range(n_chunk):
            rows = slice(c * chunk, (c + 1) * chunk)
            cum = cum_all[rows]
            q = q_ref[b, rows, :].astype(F32)
            k = k_ref[b, rows, :].astype(F32)
            v = v_ref[b, rows, :]
            last = cum[chunk - 1:chunk, :]
            refs = [cum[(j + 1) * GLA_SUB - 1:(j + 1) * GLA_SUB, :] for j in range(n_sub)]
            k_hat = jnp.concatenate(
                [k[j * GLA_SUB:(j + 1) * GLA_SUB] * jnp.exp(refs[j] - cum[j * GLA_SUB:(j + 1) * GLA_SUB])
                 for j in range(n_sub)], axis=0)
            q_hat = jnp.concatenate(
                [q * jnp.exp(jnp.minimum(cum - refs[j], 80.0)) for j in range(n_sub)], axis=0)
            k_stack = jnp.concatenate(
                [jnp.where(k_head == h, k_hat, 0.0) for h in range(GLA_HEADS)], axis=0)
            a_all = _dot_nt(q_hat.astype(BF16), k_stack.astype(BF16))
            k_dec = (k * jnp.exp(last - cum)).astype(BF16)
            v_t = v.astype(F32).T.astype(BF16)
            st_inc = _dot(v_t, k_dec)
            work.append(dict(b=b, rows=rows, a_all=a_all, st_inc=st_inc, v=v,
                             q_in=(q * jnp.exp(cum)).astype(BF16), decay=jnp.exp(last)))

    states = [st_ref[b] for b in range(nb)]
    for w in work:
        st = states[w["b"]]
        w["o_inter"] = _dot_nt(w["q_in"], st.astype(BF16))
        states[w["b"]] = st * w["decay"] + jnp.where(st_keep, w["st_inc"], 0.0)
    for b in range(nb):
        st_ref[b] = states[b]

    for w in work:
        a_m = jnp.where(a_keep, w["a_all"], 0.0)
        a = a_m[0:chunk]
        for j in range(1, n_sub):
            a = a + a_m[j * chunk:(j + 1) * chunk]
        v_blocks = jnp.concatenate(
            [jnp.where(v_head == h, w["v"], jnp.zeros_like(w["v"])) for h in range(GLA_HEADS)], axis=0)
        o = w["o_inter"] + _dot(a.astype(BF16), v_blocks)
        scale = jnp.concatenate(
            [jnp.broadcast_to(lax.rsqrt(jnp.mean(
                o[:, h * GLA_DV:(h + 1) * GLA_DV] ** 2, axis=-1, keepdims=True) + EPS), (chunk, GLA_DV))
             for h in range(GLA_HEADS)], axis=1)
        y = o * scale * gn * r_ref[w["b"], w["rows"], :].astype(F32)
        o_ref[w["b"], w["rows"], :] = y.astype(BF16)


def _gla(gq, gk, gv, ga, gr, gnorm, *, batch, seq, tm=KEY_TILE, chunk=GLA_CHUNK):
    tmap = lambda i: (0, i, 0)
    dk_all = GLA_HEADS * GLA_DK
    dv_all = GLA_HEADS * GLA_DV
    return pl.pallas_call(
        functools.partial(_gla_kernel, tm=tm, chunk=chunk),
        grid=(seq // tm,),
        in_specs=[
            pl.BlockSpec((batch, tm, dk_all), tmap),
            pl.BlockSpec((batch, tm, dk_all), tmap),
            pl.BlockSpec((batch, tm, dv_all), tmap),
            pl.BlockSpec((batch, tm, dk_all), tmap),
            pl.BlockSpec((batch, tm, dv_all), tmap),
            pl.BlockSpec((1, GLA_DV), lambda i: (0, 0)),
        ],
        out_specs=pl.BlockSpec((batch, tm, dv_all), tmap),
        out_shape=jax.ShapeDtypeStruct((batch, seq, dv_all), BF16),
        scratch_shapes=[pltpu.VMEM((batch, dv_all, dk_all), F32)],
        compiler_params=pltpu.CompilerParams(
            dimension_semantics=("arbitrary",), vmem_limit_bytes=VMEM_LIMIT),
        name="gla",
    )(gq, gk, gv, ga, gr, gnorm)


def _tail_kernel(x_ref, oa_ref, ob_ref, g0_ref, wm_ref, wn_ref, wb_ref, wo_ref,
                 g1_ref, wg_ref, wu_ref, wd_ref, g2_ref, o_ref):
    x = x_ref[...]

    def rms(v, gain_ref):
        return v * lax.rsqrt(jnp.mean(v * v, axis=-1, keepdims=True) + EPS) * gain_ref[...]

    gates = _sigmoid(_dot(rms(x, g0_ref).astype(BF16), wm_ref[...]))
    merged = (gates[:, 0:D_MODEL] * _dot(oa_ref[...], wn_ref[...])
              + gates[:, D_MODEL:2 * D_MODEL] * _dot(ob_ref[...], wb_ref[...]))
    x1 = x + _dot(merged.astype(BF16), wo_ref[...])
    h = rms(x1, g1_ref).astype(BF16)
    a = _dot(h, wg_ref[...])
    u = _dot(h, wu_ref[...])
    y = x1 + _dot((a * _sigmoid(a) * u).astype(BF16), wd_ref[...])
    o_ref[...] = rms(y, g2_ref)


def _tail(x2d, oa, ob, g0, wm, wn, wb, wo, g1, wg, wu, wd, g2, *, tm=256):
    n = x2d.shape[0]
    row = lambda i: (i, 0)
    const = lambda i: (0, 0)
    weight = lambda r, c: pl.BlockSpec((r, c), const, pipeline_mode=pl.Buffered(1))
    return pl.pallas_call(
        _tail_kernel,
        grid=(n // tm,),
        in_specs=[
            pl.BlockSpec((tm, D_MODEL), row),
            pl.BlockSpec((tm, NSA_HEADS * DH), row),
            pl.BlockSpec((tm, GLA_HEADS * GLA_DV), row),
            weight(1, D_MODEL), weight(D_MODEL, 2 * D_MODEL),
            weight(NSA_HEADS * DH, D_MODEL), weight(GLA_HEADS * GLA_DV, D_MODEL),
            weight(D_MODEL, D_MODEL),
            weight(1, D_MODEL), weight(D_MODEL, D_FF), weight(D_MODEL, D_FF), weight(D_FF, D_MODEL),
            weight(1, D_MODEL),
        ],
        out_specs=pl.BlockSpec((tm, D_MODEL), row),
        out_shape=jax.ShapeDtypeStruct((n, D_MODEL), F32),
        compiler_params=pltpu.CompilerParams(
            dimension_semantics=("parallel",), vmem_limit_bytes=VMEM_LIMIT),
        name="tail",
    )(x2d, oa, ob, g0, wm, wn, wb, wo, g1, wg, wu, wd, g2)


def _rope_tables(seq):
    half = ROPE_DIM // 2
    inv_freq = ROPE_THETA ** (-jnp.arange(half, dtype=F32) / half)
    dim = np.arange(LANE) % DH
    freq = jnp.where(dim < ROPE_DIM, inv_freq[dim % half], 0.0)
    sign = np.where(dim < half, -1.0, np.where(dim < ROPE_DIM, 1.0, 0.0)).astype(np.float32)
    ang = jnp.arange(seq).astype(F32)[:, None] * freq[None, :]
    return jnp.cos(ang), jnp.sin(ang) * sign[None, :]


def _overlap_t(n_sel, n_cmp_pad, n_slc, n_cmp):
    c_start = np.arange(n_cmp_pad)[None, :] * CMP_STRIDE
    s_start = np.arange(n_sel)[:, None] * SLC_LEN
    ov = (c_start < s_start + SLC_LEN) & (c_start + CMP_LEN > s_start)
    ov &= (np.arange(n_cmp_pad)[None, :] < n_cmp) & (np.arange(n_sel)[:, None] < n_slc)
    return jnp.asarray(ov, dtype=BF16)


def _pack_in_weights(w_in):
    w_in = w_in.astype(BF16)
    c = 0
    wq = w_in[:, c:c + 512]; c += 512
    wkv = w_in[:, c:c + 768]; c += 768
    wgate = w_in[:, c:c + 24]; c += 24
    wgq = w_in[:, c:c + 256]; c += 256
    wgk = w_in[:, c:c + 256]; c += 256
    wgv = w_in[:, c:c + 512]; c += 512
    wlr = w_in[:, c:c + GLA_RANK]; c += GLA_RANK
    wgr = w_in[:, c:c + 512]; c += 512
    wmg = w_in[:, c:c + 2048]
    d = w_in.shape[0]
    wg3 = wgate.reshape(d, NSA_GROUPS, HPG, 3).transpose(0, 1, 3, 2).reshape(d, NSA_GROUPS, 3 * HPG)
    wg_pad = jnp.pad(wg3, ((0, 0), (0, 0), (0, GATE_LANES - 3 * HPG))).reshape(d, NSA_GROUPS * GATE_LANES)
    misc = jnp.concatenate([
        wg_pad, jnp.zeros((d, LR_LANE - NSA_GROUPS * GATE_LANES), BF16),
        wlr, jnp.zeros((d, LANE - LR_LANE - GLA_RANK), BF16)], axis=1)
    w_all = jnp.concatenate([wq, wkv, misc, wgq, wgk, wgv, wgr], axis=1)
    return w_all, wmg


def kernel(x, norm_mix, w_in, cmp_pe_k, cmp_pe_v, cmp_k_w1, cmp_k_w2, cmp_v_w1, cmp_v_w2,
           gla_gate_w2, gla_gate_b, gla_norm, w_up_nsa, w_up_gla, w_out, norm_ffn,
           w_ffn_gate, w_ffn_up, w_ffn_down, norm_final):
    batch, seq, d = x.shape
    assert d == D_MODEL and w_in.shape[0] == 1
    assert seq % KEY_TILE == 0
    n = batch * seq
    x2d = x.reshape(n, d)

    w_all, w_merge = _pack_in_weights(w_in[0])
    assert w_all.shape[1] == SEG_END
    cos_t, sin_t = _rope_tables(seq)
    w2p = jnp.pad(gla_gate_w2[0], ((LR_LANE, LANE - LR_LANE - GLA_RANK), (0, 0))).astype(BF16)
    b2 = gla_gate_b[0].reshape(1, -1)
    gain_mix = norm_mix[0].reshape(1, d)

    (qraw, qrot, cmp_src, kaug, vslc_t, kwin, vwin_t, gates, gq, gk, gv, ga, gr) = _in_proj(
        x2d, gain_mix, w_all, cos_t, sin_t, w2p, b2, seq=seq)

    n_sub = seq // CMP_STRIDE
    n_cmp = n_sub - CMP_LEN // CMP_STRIDE + 1
    n_slc = seq // SLC_LEN
    cmp_in = cmp_src.reshape(4, batch, n_sub, CMP_STRIDE * DH)
    w1 = jnp.stack([cmp_k_w1[0], cmp_v_w1[0]]).astype(BF16)
    w2 = jnp.stack([cmp_k_w2[0], cmp_v_w2[0]]).astype(BF16)
    pe = jnp.stack([cmp_pe_k[0].reshape(1, -1), cmp_pe_v[0].reshape(1, -1)])
    pe = jnp.broadcast_to(pe, (2, 8, CMP_LEN * DH)).astype(BF16)
    w2t = jnp.stack([cmp_k_w2[0].T, cmp_v_w2[0].T]).astype(BF16)
    cmp_kv, cmp_kv_t = _compress(cmp_in, w1, w2, w2t, pe, batch=batch, n_sub=n_sub)

    n_sel = -(-n_slc // SEL_HALF) * SEL_HALF
    ovt = _overlap_t(n_sel, n_sub, n_slc, n_cmp)
    o_a = _nsa(qraw, qrot, cmp_kv, cmp_kv_t, ovt, kaug, vslc_t, kwin, vwin_t, gates,
               batch=batch, seq=seq)

    per_seq = lambda a: a.reshape(batch, seq, a.shape[-1])
    o_b = _gla(per_seq(gq), per_seq(gk), per_seq(gv), per_seq(ga), per_seq(gr),
               gla_norm[0].reshape(1, -1), batch=batch, seq=seq).reshape(n, -1)

    out = _tail(x2d, o_a, o_b, gain_mix, w_merge, w_up_nsa[0].astype(BF16), w_up_gla[0].astype(BF16),
                w_out[0].astype(BF16), norm_ffn[0].reshape(1, d), w_ffn_gate[0].astype(BF16),
                w_ffn_up[0].astype(BF16), w_ffn_down[0].astype(BF16), norm_final.reshape(1, d))
    return out.reshape(batch, seq, d)
```

```python
import functools

import numpy as np
import jax
import jax.numpy as jnp
from jax import lax
from jax.experimental import pallas as pl
from jax.experimental.pallas import tpu as pltpu

F32 = jnp.float32
BF16 = jnp.bfloat16

D_MODEL = 1024
NSA_HEADS = 8
NSA_GROUPS = 2
HPG = NSA_HEADS // NSA_GROUPS
DH = 64
CMP_LEN = 32
CMP_STRIDE = 16
CMP_HIDDEN = 128
SLC_LEN = 64
SLC_TOPK = 16
WINDOW = 512
FORCE_SCORE = 1.0e4
GLA_HEADS = 4
GLA_DK = 64
GLA_DV = 128
GLA_RANK = 16
GLA_TAU = 16.0
GLA_CHUNK = 64
GLA_SUB = 16
ROPE_THETA = 500000.0
ROPE_DIM = DH // 4
D_FF = 2816
EPS = 1e-6
NEG = -1e30
OVERFLOW_GUARD = 1e37
LOG2E = 1.4426950408889634
KEY_TILE = 256

LANE = 128
VMEM_LIMIT = 56 * 1024 * 1024
SEL_HALF = 64
V_ROWS = DH + 16
MXU_LEAD = 8
SEL_GROUP = 8

SEG_Q = 0
SEG_KV = SEG_Q + 512
SEG_MISC = SEG_KV + 768
SEG_GLA = SEG_MISC + 128
SEG_END = SEG_GLA + 1536
GATE_LANES = 16
LR_LANE = 64


def _dot(a, b):
    return jnp.dot(a, b, preferred_element_type=F32)


def _dot_nt(a, b):
    return lax.dot_general(a, b, (((1,), (1,)), ((), ())), preferred_element_type=F32)


def _sigmoid(x):
    return 1.0 / (1.0 + jnp.exp(-x))


def _in_proj_kernel(x_ref, gain_ref, w_ref, cos_ref, sin_ref, w2_ref, b2_ref,
                    qraw_ref, qrot_ref, cmp_ref, kaug_ref, vslc_ref, kwin_ref, vwin_ref,
                    gate_ref, gq_ref, gk_ref, gv_ref, ga_ref, gr_ref, flat_ref, *, tm, seq):
    i = pl.program_id(0)
    x = x_ref[...]
    ms = jnp.mean(x * x, axis=-1, keepdims=True)
    h = (x * lax.rsqrt(ms + EPS) * gain_ref[...]).astype(BF16)

    def proj(a, b):
        return _dot(h, w_ref[:, a:b])

    cos = cos_ref[...]
    sin = sin_ref[...]

    def rope(v):
        width = v.shape[1]
        half = ROPE_DIM // 2
        first = lax.broadcasted_iota(jnp.int32, v.shape, 1) % DH < half
        partner = jnp.where(first, pltpu.roll(v, width - half, axis=1), pltpu.roll(v, half, axis=1))
        reps = width // LANE
        c = cos if reps == 1 else jnp.concatenate([cos] * reps, axis=1)
        s = sin if reps == 1 else jnp.concatenate([sin] * reps, axis=1)
        return v * c + partner * s

    scale = DH ** -0.5
    q = proj(SEG_Q, SEG_Q + 512)
    qraw_t = (q * (scale * LOG2E)).T
    qrot_t = (rope(q) * (scale * LOG2E)).T
    gw = HPG * DH
    for g in range(NSA_GROUPS):
        qraw_ref[g, 0] = qraw_t[g * gw:(g + 1) * gw, :].astype(BF16)
        qrot_ref[g, 0] = qrot_t[g * gw:(g + 1) * gw, :].astype(BF16)

    kv = proj(SEG_KV, SEG_KV + 768)
    n_rows = tm // CMP_STRIDE
    for j in range(2):
        for g in range(NSA_GROUPS):
            flat_ref[2 * j + g] = kv[:, j * 128 + g * DH:j * 128 + (g + 1) * DH]
    for jg in range(2 * NSA_GROUPS):
        for l in range(0, CMP_STRIDE, 2):
            pair = jnp.concatenate(
                [flat_ref[jg, pl.ds(l, n_rows, stride=CMP_STRIDE), :],
                 flat_ref[jg, pl.ds(l + 1, n_rows, stride=CMP_STRIDE), :]], axis=1)
            cmp_ref[jg, :, l * DH:(l + 2) * DH] = pair.astype(BF16)
    kslc = rope(kv[:, 256:384])
    kwin = rope(kv[:, 512:640])
    vslc_t = kv[:, 384:512].T
    vwin_t = kv[:, 640:768].T
    pos = (i * tm) % seq + lax.broadcasted_iota(jnp.int32, (tm, SEL_HALF), 0)
    blk = (pos // SLC_LEN) % SEL_HALF
    onehot = jnp.where(lax.broadcasted_iota(jnp.int32, (tm, SEL_HALF), 1) == blk, 1.0, 0.0)
    ones_rows = jnp.ones((V_ROWS - DH, tm), F32)
    for g in range(NSA_GROUPS):
        sl = slice(g * DH, (g + 1) * DH)
        kaug_ref[g] =jnp.concatenate([kslc[:, sl], onehot], axis=1).astype(BF16)
        vslc_ref[g, 0] = jnp.concatenate([vslc_t[sl, :], ones_rows], axis=0).astype(BF16)
        kwin_ref[g] = kwin[:, sl].astype(BF16)
        vwin_ref[g, 0] = jnp.concatenate([vwin_t[sl, :], ones_rows], axis=0).astype(BF16)

    misc = proj(SEG_MISC, SEG_MISC + 128)
    gate_ref[...] = _sigmoid(misc)

    gla = proj(SEG_GLA, SEG_GLA + 1536)
    gq_ref[...] = (gla[:, 0:256] * (GLA_DK ** -0.5)).astype(BF16)
    gk_ref[...] = gla[:, 256:512].astype(BF16)
    gv_ref[...] = gla[:, 512:1024].astype(BF16)
    r = gla[:, 1024:1536]
    gr_ref[...] = (r * _sigmoid(r)).astype(BF16)

    z = _dot(misc.astype(BF16), w2_ref[...]) + b2_ref[...]
    log_sig = jnp.minimum(z, 0.0) - jnp.log1p(jnp.exp(-jnp.abs(z)))
    ga_ref[...] = log_sig * (1.0 / GLA_TAU)


def _in_proj(x2d, gain, w_all, cos_t, sin_t, w2p, b2, *, seq, tm=KEY_TILE):
    n = x2d.shape[0]
    nt = seq // tm
    row = lambda i: (i, 0)
    grp = lambda i: (0, i, 0)
    const = lambda i: (0, 0)
    tab = lambda i: (i % nt, 0)
    out_shape = (
        jax.ShapeDtypeStruct((NSA_GROUPS, n // tm, HPG * DH, tm), BF16),
        jax.ShapeDtypeStruct((NSA_GROUPS, n // tm, HPG * DH, tm), BF16),
        jax.ShapeDtypeStruct((4, n // CMP_STRIDE, CMP_STRIDE * DH), BF16),
        jax.ShapeDtypeStruct((NSA_GROUPS, n, 128), BF16),
        jax.ShapeDtypeStruct((NSA_GROUPS, n // tm, V_ROWS, tm), BF16),
        jax.ShapeDtypeStruct((NSA_GROUPS, n, DH), BF16),
        jax.ShapeDtypeStruct((NSA_GROUPS, n // tm, V_ROWS, tm), BF16),
        jax.ShapeDtypeStruct((n, LANE), F32),
        jax.ShapeDtypeStruct((n, 256), BF16),
        jax.ShapeDtypeStruct((n, 256), BF16),
        jax.ShapeDtypeStruct((n, 512), BF16),
        jax.ShapeDtypeStruct((n, 256), F32),
        jax.ShapeDtypeStruct((n, 512), BF16),
    )
    out_specs = (
        pl.BlockSpec((NSA_GROUPS, 1, HPG * DH, tm), lambda i: (0, i, 0, 0)),
        pl.BlockSpec((NSA_GROUPS, 1, HPG * DH, tm), lambda i: (0, i, 0, 0)),
        pl.BlockSpec((4, tm // CMP_STRIDE, CMP_STRIDE * DH), grp),
        pl.BlockSpec((NSA_GROUPS, tm, 128), grp),
        pl.BlockSpec((NSA_GROUPS, 1, V_ROWS, tm), lambda i: (0, i, 0, 0)),
        pl.BlockSpec((NSA_GROUPS, tm, DH), grp),
        pl.BlockSpec((NSA_GROUPS, 1, V_ROWS, tm), lambda i: (0, i, 0, 0)),
        pl.BlockSpec((tm, LANE), row),
        pl.BlockSpec((tm, 256), row), pl.BlockSpec((tm, 256), row), pl.BlockSpec((tm, 512), row),
        pl.BlockSpec((tm, 256), row), pl.BlockSpec((tm, 512), row),
    )
    in_specs = [
        pl.BlockSpec((tm, D_MODEL), row),
        pl.BlockSpec((1, D_MODEL), const),
        pl.BlockSpec((D_MODEL, SEG_END), const, pipeline_mode=pl.Buffered(1)),
        pl.BlockSpec((tm, LANE), tab),
        pl.BlockSpec((tm, LANE), tab),
        pl.BlockSpec((LANE, 256), const),
        pl.BlockSpec((1, 256), const),
    ]
    return pl.pallas_call(
        functools.partial(_in_proj_kernel, tm=tm, seq=seq),
        grid=(n // tm,),
        in_specs=in_specs, out_specs=out_specs, out_shape=out_shape,
        scratch_shapes=[pltpu.VMEM((2 * NSA_GROUPS, tm, DH), F32)],
        compiler_params=pltpu.CompilerParams(
            dimension_semantics=("parallel",), vmem_limit_bytes=VMEM_LIMIT),
        name="in_proj",
    )(x2d, gain, w_all, cos_t, sin_t, w2p, b2)


def _compress_kernel(x_ref, w1_ref, w2_ref, w2t_ref, pe_ref, o_ref, ot_ref, *, n_sub):
    half = CMP_STRIDE * DH
    x = x_ref[...]
    u = _dot(x, w1_ref[0:half, :])
    v = _dot(x, w1_ref[half:2 * half, :])
    c = _dot(pe_ref[...], w1_ref[...])[0:1, :]
    hid = u + pltpu.roll(v, shift=n_sub - 1, axis=0) + c
    hid = (hid * _sigmoid(hid)).astype(BF16)
    out = _dot(hid, w2_ref[...])
    rowi = lax.broadcasted_iota(jnp.int32, out.shape, 0)
    o_ref[...] = jnp.where(rowi < n_sub - 1, out, 0.0).astype(BF16)
    out_t = _dot_nt(w2t_ref[...], hid)
    coli = lax.broadcasted_iota(jnp.int32, out_t.shape, 1)
    ot_ref[0:DH, :] = jnp.where(coli < n_sub - 1, out_t, 0.0).astype(BF16)
    ot_ref[DH:V_ROWS, :] = jnp.ones((V_ROWS - DH, n_sub), BF16)


def _compress(src, w1, w2, w2t, pe, *, batch, n_sub):
    return pl.pallas_call(
        functools.partial(_compress_kernel, n_sub=n_sub),
        grid=(4, batch),
        in_specs=[
            pl.BlockSpec((None, None, n_sub, CMP_STRIDE * DH), lambda j, b: (j, b, 0, 0)),
            pl.BlockSpec((None, CMP_LEN * DH, CMP_HIDDEN), lambda j, b: (j // 2, 0, 0)),
            pl.BlockSpec((None, CMP_HIDDEN, DH), lambda j, b: (j // 2, 0, 0)),
            pl.BlockSpec((None, DH, CMP_HIDDEN), lambda j, b: (j // 2, 0, 0)),
            pl.BlockSpec((None, 8, CMP_LEN * DH), lambda j, b: (j // 2, 0, 0)),
        ],
        out_specs=(pl.BlockSpec((None, None, n_sub, DH), lambda j, b: (j, b, 0, 0)),
                   pl.BlockSpec((None, None, V_ROWS, n_sub), lambda j, b: (j, b, 0, 0))),
        out_shape=(jax.ShapeDtypeStruct((4, batch, n_sub, DH), BF16),
                   jax.ShapeDtypeStruct((4, batch, V_ROWS, n_sub), BF16)),
        compiler_params=pltpu.CompilerParams(
            dimension_semantics=("parallel", "parallel"), vmem_limit_bytes=VMEM_LIMIT),
        name="compress",
    )(src, w1, w2, w2t, pe)


def _softmax_tiles(score_fns, vts, carry, lead=MXU_LEAD):
    n = len(score_fns)
    scores = [fn() for fn in score_fns[:lead]] + [None] * max(n - lead, 0)
    out = []
    for j in range(n):
        m, acc = carry[j]
        m_new = jnp.maximum(m, jnp.max(scores[j], axis=0, keepdims=True))
        p = jnp.exp2((scores[j] - m_new).astype(BF16))
        out.append((m_new, jnp.exp2(m - m_new) * acc + _dot(vts[j], p)))
        if j + lead < n:
            scores[j + lead] = score_fns[j + lead]()
    return tuple(out)


def _nsa_kernel(qraw_ref, qrot_ref, kc_ref, vct_ref, ovt_ref, kaug_ref, vst_ref, kw_ref, vwt_ref,
                gate_ref, o_ref, qa_ref, gt_ref, bias_ref, *, tq, topk):
    i = pl.program_id(2)
    s0 = i * tq
    n_cmp = kc_ref.shape[0]
    n_sel = ovt_ref.shape[0]

    def init_carry():
        lane = lax.broadcasted_iota(jnp.int32, (1, tq), 1)
        one = (jnp.where(lane >= 0, NEG, 0.0), jnp.zeros((V_ROWS, tq), F32))
        return tuple(one for _ in range(HPG))

    def normalised(acc):
        return acc[0:DH] / acc[DH:DH + 1]

    def compressed(n_keys):
        kc = kc_ref[0:n_keys, :]
        cend = lax.broadcasted_iota(jnp.int32, (n_keys, tq), 0) * CMP_STRIDE + (CMP_LEN - 1)
        cmask = cend <= s0 + lax.broadcasted_iota(jnp.int32, (n_keys, tq), 1)
        scores = [_dot(kc, qraw_ref[h * DH:(h + 1) * DH, :]) for h in range(HPG)]
        expd, seen = [], []
        for s in scores:
            s = jnp.where(cmask, s, NEG)
            m = jnp.max(s, axis=0, keepdims=True)
            expd.append(jnp.exp2((s - m).astype(BF16)))
            seen.append(m > 0.5 * NEG)
        vct = vct_ref[:, 0:n_keys]
        ovt = ovt_ref[:, 0:n_keys]
        outs, imp = [], 0.0
        for e, ok in zip(expd, seen):
            acc = _dot(vct, e)
            r = jnp.where(ok, 1.0 / acc[DH:DH + 1], 0.0)
            outs.append(acc[0:DH] * r)
            imp = imp + _dot(ovt, e) * r
        return outs, imp

    oc_t, imp_t = compressed(n_cmp)

    key_i = lax.broadcasted_iota(jnp.int32, (tq, tq), 0)
    qry_i = lax.broadcasted_iota(jnp.int32, (tq, tq), 1)
    causal = key_i <= qry_i
    n_back = WINDOW // tq
    w_tiles = [jnp.maximum(i - n_back + u, 0) for u in range(n_back)] + [i]
    w_keys = [kw_ref[pl.ds(pl.multiple_of(t * tq, tq), tq), :] for t in w_tiles]
    w_bands = [(i >= n_back - u) & ((qry_i < key_i) if u == 0 else True) for u in range(n_back)]
    w_bands.append(causal)
    w_scores = [lambda h=h: jnp.concatenate(
        [jnp.where(band, _dot(kt, qrot_ref[h * DH:(h + 1) * DH, :]), NEG)
         for kt, band in zip(w_keys, w_bands)], axis=0) for h in range(HPG)]
    w_vt = jnp.concatenate([vwt_ref[t] for t in w_tiles], axis=1)
    ow_t = [normalised(acc) for (_, acc) in _softmax_tiles(w_scores, [w_vt] * HPG, init_carry())]

    kt_diag = kaug_ref[pl.ds(pl.multiple_of(s0, tq), tq), 0:DH]
    diag_raw = [_dot(kt_diag, qrot_ref[h * DH:(h + 1) * DH, :]) for h in range(HPG)]
    near = causal & (key_i // SLC_LEN >= qry_i // SLC_LEN - 1)
    shifts = [jnp.max(jnp.where(near, s, NEG), axis=0, keepdims=True).astype(BF16).astype(F32)
              for s in diag_raw]

    jrow = lax.broadcasted_iota(jnp.int32, (n_sel, tq), 0)
    tlane = s0 + lax.broadcasted_iota(jnp.int32, (n_sel, tq), 1)
    cur = tlane // SLC_LEN
    valid = jrow * SLC_LEN <= tlane
    forced = (jrow == 0) | (jrow == cur) | (jrow == cur - 1)
    score = jnp.where(valid, jnp.where(forced, FORCE_SCORE, imp_t), -jnp.inf)

    n_forced = 3
    jrow_f = jrow.astype(F32)
    start = jnp.where(forced, -jnp.inf, score)
    work = start
    for _ in range(max(topk - n_forced, 0)):
        top = jnp.max(work, axis=0, keepdims=True)
        first = jnp.min(jnp.where(work == top, jrow_f, float(n_sel)), axis=0, keepdims=True)
        work = jnp.where(jrow_f == first, -jnp.inf, work)
    chosen = (work != start) | (forced & valid)
    bias_t = jnp.where(chosen, 0.0, NEG).astype(BF16)

    n_half = n_sel // SEL_HALF
    tiles_per_half = SEL_HALF * SLC_LEN // tq
    def write_weights(shifts):
        for a in range(n_half):
            rows = bias_t[a * SEL_HALF:(a + 1) * SEL_HALF, :]
            for h in range(HPG):
                qa_ref[a, h, 0:DH, :] = qrot_ref[h * DH:(h + 1) * DH, :]
                qa_ref[a, h, DH:DH + SEL_HALF, :] = (
                    rows if shifts is None else (rows.astype(F32) - shifts[h]).astype(BF16))

    def key_tiles(first, count):
        return kaug_ref[pl.ds(pl.multiple_of(first * tq, tq), count * tq), :]

    def value_tiles(first, count):
        return vst_ref[first] if count == 1 else jnp.concatenate(
            [vst_ref[first + u] for u in range(count)], axis=1)

    bias_ref[...] = bias_t.astype(F32)
    per_tile = tq // SLC_LEN
    diag_bias = jnp.concatenate(
        [jnp.broadcast_to(bias_ref[pl.ds(i * per_tile + r, 1), :], (SLC_LEN, tq))
         for r in range(per_tile)], axis=0)
    vt_diag = vst_ref[i]
    accs = tuple(_dot(vt_diag, jnp.exp2((jnp.where(causal, s + diag_bias, NEG) - c).astype(BF16)))
                 for s, c in zip(diag_raw, shifts))
    write_weights(shifts)

    def fast_step(first, count, accs):
        kt = key_tiles(first, count)
        vt = value_tiles(first, count)
        half = first // tiles_per_half
        scores = [_dot(kt, qa_ref[half, h]) for h in range(HPG)]
        return tuple(acc + _dot(vt, jnp.exp2(s.astype(BF16))) for s, acc in zip(scores, accs))

    half_group = SEL_GROUP // 2
    n_groups = i // SEL_GROUP
    accs = lax.fori_loop(0, n_groups, lambda t, a: fast_step(t * SEL_GROUP, SEL_GROUP, a), accs)
    done = n_groups * SEL_GROUP
    n_half_groups = (i - done) // half_group
    accs = lax.fori_loop(0, n_half_groups, lambda t, a: fast_step(done, half_group, a), accs)
    done = done + n_half_groups * half_group
    leftovers = [lambda a: a] + [
        functools.partial(lambda a, k: fast_step(done, k, a), k=k) for k in range(1, half_group)]
    accs = lax.switch(i - done, leftovers, accs)

    def rescaled_sweep():
        write_weights(None)

        def tile_scores(t, diagonal):
            kt = key_tiles(t, 1)
            half = t // tiles_per_half
            return [functools.partial(
                lambda h: jnp.where(causal, _dot(kt, qa_ref[half, h]), NEG) if diagonal
                else _dot(kt, qa_ref[half, h]), h) for h in range(HPG)]

        carry = lax.fori_loop(
            0, i, lambda t, c: _softmax_tiles(tile_scores(t, False), [vst_ref[t]] * HPG, c),
            init_carry())
        carry = _softmax_tiles(tile_scores(i, True), [vt_diag] * HPG, carry)
        return tuple(normalised(acc) for (_, acc) in carry)

    denominators = jnp.concatenate([acc[DH:DH + 1] for acc in accs], axis=0)
    no_overflow = jnp.max(denominators) < OVERFLOW_GUARD
    os_t = lax.cond(no_overflow, lambda: tuple(normalised(acc) for acc in accs), rescaled_sweep)

    gt_ref[...] = gate_ref[...].T
    gt_t = gt_ref[pl.ds(pl.multiple_of(pl.program_id(1) * GATE_LANES, GATE_LANES), GATE_LANES), :]
    outs = [gt_t[h:h + 1, :] * oc_t[h] + gt_t[HPG + h:HPG + h + 1, :] * os_t[h]
            + gt_t[2 * HPG + h:2 * HPG + h + 1, :] * ow_t[h] for h in range(HPG)]
    o_ref[...] = jnp.concatenate(outs, axis=0).T.astype(BF16)


def _nsa(qraw_t, qrot_t, cmp_kv, cmp_kv_t, ovt, kaug, vslc_t, kwin, vwin_t, gates, *, batch, seq,
         tq=KEY_TILE):
    n = batch * seq
    nq = seq // tq
    nk = nq
    tk = tq
    n_cmp = cmp_kv.shape[2]
    n_sel = ovt.shape[0]
    topk = min(SLC_TOPK, seq // SLC_LEN)
    qmap = lambda b, g, i: (b * nq + i, g)
    qtmap = lambda b, g, i: (g, b * nq + i, 0, 0)
    kvmap = lambda b, g, i: (g, b, 0)
    vtmap = lambda b, g, i: (g, b, 0, 0)
    return pl.pallas_call(
        functools.partial(_nsa_kernel, tq=tq, topk=topk),
        grid=(batch, NSA_GROUPS, nq),
        in_specs=[
            pl.BlockSpec((None, None, HPG * DH, tq), qtmap),
            pl.BlockSpec((None, None, HPG * DH, tq), qtmap),
            pl.BlockSpec((None, None, n_cmp, DH), lambda b, g, i: (g, b, 0, 0)),
            pl.BlockSpec((None, None, V_ROWS, n_cmp), lambda b, g, i: (NSA_GROUPS + g, b, 0, 0)),
            pl.BlockSpec((n_sel, n_cmp), lambda b, g, i: (0, 0)),
            pl.BlockSpec((None, seq, 128), kvmap),
            pl.BlockSpec((None, nk, V_ROWS, tk), vtmap),
            pl.BlockSpec((None, seq, DH), kvmap),
            pl.BlockSpec((None, nk, V_ROWS, tk), vtmap),
            pl.BlockSpec((tq, LANE), lambda b, g, i: (b * nq + i, 0)),
        ],
        out_specs=pl.BlockSpec((tq, HPG * DH), qmap),
        out_shape=jax.ShapeDtypeStruct((n, NSA_HEADS * DH), BF16),
        scratch_shapes=[
            pltpu.VMEM((n_sel // SEL_HALF, HPG, DH + SEL_HALF, tq), BF16),
            pltpu.VMEM((LANE, tq), F32),
            pltpu.VMEM((n_sel, tq), F32),
        ],
        compiler_params=pltpu.CompilerParams(
            dimension_semantics=("parallel", "parallel", "arbitrary"),
            vmem_limit_bytes=VMEM_LIMIT),
        name="nsa",
    )(qraw_t, qrot_t, cmp_kv, cmp_kv_t, ovt, kaug, vslc_t, kwin, vwin_t, gates)


def _gla_kernel(q_ref, k_ref, v_ref, g_ref, r_ref, gn_ref, o_ref, st_ref, *, tm, chunk):
    nb = q_ref.shape[0]
    n_chunk = tm // chunk
    n_sub = chunk // GLA_SUB
    dk_all = GLA_HEADS * GLA_DK
    dv_all = GLA_HEADS * GLA_DV

    @pl.when(pl.program_id(0) == 0)
    def _():
        st_ref[...] = jnp.zeros(st_ref.shape, F32)

    def iota(shape, axis):
        return lax.broadcasted_iota(jnp.int32, shape, axis)

    tril = jnp.where((iota((tm, tm), 0) >= iota((tm, tm), 1))
                     & (iota((tm, tm), 0) // chunk == iota((tm, tm), 1) // chunk), 1.0, 0.0).astype(BF16)
    k_head = iota((chunk, dk_all), 1) // GLA_DK
    v_head = iota((chunk, dv_all), 1) // GLA_DV
    a_rows = iota((n_sub * chunk, GLA_HEADS * chunk), 0)
    a_key = iota((n_sub * chunk, GLA_HEADS * chunk), 1) % chunk
    a_keep = (a_key // GLA_SUB == a_rows // chunk) & (a_key <= a_rows % chunk)
    st_keep = iota((dv_all, dk_all), 0) // GLA_DV == iota((dv_all, dk_all), 1) // GLA_DK
    gn = jnp.concatenate([gn_ref[...]] * GLA_HEADS, axis=1)

    work = []
    for b in range(nb):
        g = g_ref[b]
        g_hi = g.astype(BF16)
        g_lo = (g - g_hi.astype(F32)).astype(BF16)
        cum_all = _dot(tril, g_hi) + _dot(tril, g_lo)
        for c in range(n_chunk):
            rows = slice(c * chunk, (c + 1) * chunk)
            cum = cum_all[rows]
            q = q_ref[b, rows, :].astype(F32)
            k = k_ref[b, rows, :].astype(F32)
            v = v_ref[b, rows, :]
            last = cum[chunk - 1:chunk, :]
            refs = [cum[(j + 1) * GLA_SUB - 1:(j + 1) * GLA_SUB, :] for j in range(n_sub)]
            k_hat = jnp.concatenate(
                [k[j * GLA_SUB:(j + 1) * GLA_SUB] * jnp.exp(refs[j] - cum[j * GLA_SUB:(j + 1) * GLA_SUB])
                 for j in range(n_sub)], axis=0)
            q_hat = jnp.concatenate(
                [q * jnp.exp(jnp.minimum(cum - refs[j], 80.0)) for j in range(n_sub)], axis=0)
            k_stack = jnp.concatenate(
                [jnp.where(k_head == h, k_hat, 0.0) for h in range(GLA_HEADS)], axis=0)
            a_all = _dot_nt(q_hat.astype(BF16), k_stack.astype(BF16))
            k_dec = (k * jnp.exp(last - cum)).astype(BF16)
            v_t = v.astype(F32).T.astype(BF16)
            st_inc = _dot(v_t, k_dec)
            work.append(dict(b=b, rows=rows, a_all=a_all, st_inc=st_inc, v=v,
                             q_in=(q * jnp.exp(cum)).astype(BF16), decay=jnp.exp(last)))

    states = [st_ref[b] for b in range(nb)]
    for w in work:
        st = states[w["b"]]
        w["o_inter"] = _dot_nt(w["q_in"], st.astype(BF16))
        states[w["b"]] = st * w["decay"] + jnp.where(st_keep, w["st_inc"], 0.0)
    for b in range(nb):
        st_ref[b] = states[b]

    for w in work:
        a_m = jnp.where(a_keep, w["a_all"], 0.0)
        a = a_m[0:chunk]
        for j in range(1, n_sub):
            a = a + a_m[j * chunk:(j + 1) * chunk]
        v_blocks = jnp.concatenate(
            [jnp.where(v_head == h, w["v"], jnp.zeros_like(w["v"])) for h in range(GLA_HEADS)], axis=0)
        o = w["o_inter"] + _dot(a.astype(BF16), v_blocks)
        scale = jnp.concatenate(
            [jnp.broadcast_to(lax.rsqrt(jnp.mean(
                o[:, h * GLA_DV:(h + 1) * GLA_DV] ** 2, axis=-1, keepdims=True) + EPS), (chunk, GLA_DV))
             for h in range(GLA_HEADS)], axis=1)
        y = o * scale * gn * r_ref[w["b"], w["rows"], :].astype(F32)
        o_ref[w["b"], w["rows"], :] = y.astype(BF16)


def _gla(gq, gk, gv, ga, gr, gnorm, *, batch, seq, tm=KEY_TILE, chunk=GLA_CHUNK):
    tmap = lambda i: (0, i, 0)
    dk_all = GLA_HEADS * GLA_DK
    dv_all = GLA_HEADS * GLA_DV
    return pl.pallas_call(
        functools.partial(_gla_kernel, tm=tm, chunk=chunk),
        grid=(seq // tm,),
        in_specs=[
            pl.BlockSpec((batch, tm, dk_all), tmap),
            pl.BlockSpec((batch, tm, dk_all), tmap),
            pl.BlockSpec((batch, tm, dv_all), tmap),
            pl.BlockSpec((batch, tm, dk_all), tmap),
            pl.BlockSpec((batch, tm, dv_all), tmap),
            pl.BlockSpec((1, GLA_DV), lambda i: (0, 0)),
        ],
        out_specs=pl.BlockSpec((batch, tm, dv_all), tmap),
        out_shape=jax.ShapeDtypeStruct((batch, seq, dv_all), BF16),
        scratch_shapes=[pltpu.VMEM((batch, dv_all, dk_all), F32)],
        compiler_params=pltpu.CompilerParams(
            dimension_semantics=("arbitrary",), vmem_limit_bytes=VMEM_LIMIT),
        name="gla",
    )(gq, gk, gv, ga, gr, gnorm)


def _tail_kernel(x_ref, oa_ref, ob_ref, g0_ref, wm_ref, wn_ref, wb_ref, wo_ref,
                 g1_ref, wg_ref, wu_ref, wd_ref, g2_ref, o_ref):
    x = x_ref[...]

    def rms(v, gain_ref):
        return v * lax.rsqrt(jnp.mean(v * v, axis=-1, keepdims=True) + EPS) * gain_ref[...]

    gates = _sigmoid(_dot(rms(x, g0_ref).astype(BF16), wm_ref[...]))
    merged = (gates[:, 0:D_MODEL] * _dot(oa_ref[...], wn_ref[...])
              + gates[:, D_MODEL:2 * D_MODEL] * _dot(ob_ref[...], wb_ref[...]))
    x1 = x + _dot(merged.astype(BF16), wo_ref[...])
    h = rms(x1, g1_ref).astype(BF16)
    a = _dot(h, wg_ref[...])
    u = _dot(h, wu_ref[...])
    y = x1 + _dot((a * _sigmoid(a) * u).astype(BF16), wd_ref[...])
    o_ref[...] = rms(y, g2_ref)


def _tail(x2d, oa, ob, g0, wm, wn, wb, wo, g1, wg, wu, wd, g2, *, tm=256):
    n = x2d.shape[0]
    row = lambda i: (i, 0)
    const = lambda i: (0, 0)
    weight = lambda r, c: pl.BlockSpec((r, c), const, pipeline_mode=pl.Buffered(1))
    return pl.pallas_call(
        _tail_kernel,
        grid=(n // tm,),
        in_specs=[
            pl.BlockSpec((tm, D_MODEL), row),
            pl.BlockSpec((tm, NSA_HEADS * DH), row),
            pl.BlockSpec((tm, GLA_HEADS * GLA_DV), row),
            weight(1, D_MODEL), weight(D_MODEL, 2 * D_MODEL),
            weight(NSA_HEADS * DH, D_MODEL), weight(GLA_HEADS * GLA_DV, D_MODEL),
            weight(D_MODEL, D_MODEL),
            weight(1, D_MODEL), weight(D_MODEL, D_FF), weight(D_MODEL, D_FF), weight(D_FF, D_MODEL),
            weight(1, D_MODEL),
        ],
        out_specs=pl.BlockSpec((tm, D_MODEL), row),
        out_shape=jax.ShapeDtypeStruct((n, D_MODEL), F32),
        compiler_params=pltpu.CompilerParams(
            dimension_semantics=("parallel",), vmem_limit_bytes=VMEM_LIMIT),
        name="tail",
    )(x2d, oa, ob, g0, wm, wn, wb, wo, g1, wg, wu, wd, g2)


def _rope_tables(seq):
    half = ROPE_DIM // 2
    inv_freq = ROPE_THETA ** (-jnp.arange(half, dtype=F32) / half)
    dim = np.arange(LANE) % DH
    freq = jnp.where(dim < ROPE_DIM, inv_freq[dim % half], 0.0)
    sign = np.where(dim < half, -1.0, np.where(dim < ROPE_DIM, 1.0, 0.0)).astype(np.float32)
    ang = jnp.arange(seq).astype(F32)[:, None] * freq[None, :]
    return jnp.cos(ang), jnp.sin(ang) * sign[None, :]


def _overlap_t(n_sel, n_cmp_pad, n_slc, n_cmp):
    c_start = np.arange(n_cmp_pad)[None, :] * CMP_STRIDE
    s_start = np.arange(n_sel)[:, None] * SLC_LEN
    ov = (c_start < s_start + SLC_LEN) & (c_start + CMP_LEN > s_start)
    ov &= (np.arange(n_cmp_pad)[None, :] < n_cmp) & (np.arange(n_sel)[:, None] < n_slc)
    return jnp.asarray(ov, dtype=BF16)


def _pack_in_weights(w_in):
    w_in = w_in.astype(BF16)
    c = 0
    wq = w_in[:, c:c + 512]; c += 512
    wkv = w_in[:, c:c + 768]; c += 768
    wgate = w_in[:, c:c + 24]; c += 24
    wgq = w_in[:, c:c + 256]; c += 256
    wgk = w_in[:, c:c + 256]; c += 256
    wgv = w_in[:, c:c + 512]; c += 512
    wlr = w_in[:, c:c + GLA_RANK]; c += GLA_RANK
    wgr = w_in[:, c:c + 512]; c += 512
    wmg = w_in[:, c:c + 2048]
    d = w_in.shape[0]
    wg3 = wgate.reshape(d, NSA_GROUPS, HPG, 3).transpose(0, 1, 3, 2).reshape(d, NSA_GROUPS, 3 * HPG)
    wg_pad = jnp.pad(wg3, ((0, 0), (0, 0), (0, GATE_LANES - 3 * HPG))).reshape(d, NSA_GROUPS * GATE_LANES)
    misc = jnp.concatenate([
        wg_pad, jnp.zeros((d, LR_LANE - NSA_GROUPS * GATE_LANES), BF16),
        wlr, jnp.zeros((d, LANE - LR_LANE - GLA_RANK), BF16)], axis=1)
    w_all = jnp.concatenate([wq, wkv, misc, wgq, wgk, wgv, wgr], axis=1)
    return w_all, wmg


def kernel(x, norm_mix, w_in, cmp_pe_k, cmp_pe_v, cmp_k_w1, cmp_k_w2, cmp_v_w1, cmp_v_w2,
           gla_gate_w2, gla_gate_b, gla_norm, w_up_nsa, w_up_gla, w_out, norm_ffn,
           w_ffn_gate, w_ffn_up, w_ffn_down, norm_final):
    batch, seq, d = x.shape
    assert d == D_MODEL and w_in.shape[0] == 1
    assert seq % KEY_TILE == 0
    n = batch * seq
    x2d = x.reshape(n, d)

    w_all, w_merge = _pack_in_weights(w_in[0])
    assert w_all.shape[1] == SEG_END
    cos_t, sin_t = _rope_tables(seq)
    w2p = jnp.pad(gla_gate_w2[0], ((LR_LANE, LANE - LR_LANE - GLA_RANK), (0, 0))).astype(BF16)
    b2 = gla_gate_b[0].reshape(1, -1)
    gain_mix = norm_mix[0].reshape(1, d)

    (qraw, qrot, cmp_src, kaug, vslc_t, kwin, vwin_t, gates, gq, gk, gv, ga, gr) = _in_proj(
        x2d, gain_mix, w_all, cos_t, sin_t, w2p, b2, seq=seq)

    n_sub = seq // CMP_STRIDE
    n_cmp = n_sub - CMP_LEN // CMP_STRIDE + 1
    n_slc = seq // SLC_LEN
    cmp_in = cmp_src.reshape(4, batch, n_sub, CMP_STRIDE * DH)
    w1 = jnp.stack([cmp_k_w1[0], cmp_v_w1[0]]).astype(BF16)
    w2 = jnp.stack([cmp_k_w2[0], cmp_v_w2[0]]).astype(BF16)
    pe = jnp.stack([cmp_pe_k[0].reshape(1, -1), cmp_pe_v[0].reshape(1, -1)])
    pe = jnp.broadcast_to(pe, (2, 8, CMP_LEN * DH)).astype(BF16)
    w2t = jnp.stack([cmp_k_w2[0].T, cmp_v_w2[0].T]).astype(BF16)
    cmp_kv, cmp_kv_t = _compress(cmp_in, w1, w2, w2t, pe, batch=batch, n_sub=n_sub)

    n_sel = -(-n_slc // SEL_HALF) * SEL_HALF
    ovt = _overlap_t(n_sel, n_sub, n_slc, n_cmp)
    o_a = _nsa(qraw, qrot, cmp_kv, cmp_kv_t, ovt, kaug, vslc_t, kwin, vwin_t, gates,
               batch=batch, seq=seq)

    per_seq = lambda a: a.reshape(batch, seq, a.shape[-1])
    o_b = _gla(per_seq(gq), per_seq(gk), per_seq(gv), per_seq(ga), per_seq(gr),
               gla_norm[0].reshape(1, -1), batch=batch, seq=seq).reshape(n, -1)

    out = _tail(x2d, o_a, o_b, gain_mix, w_merge, w_up_nsa[0].astype(BF16), w_up_gla[0].astype(BF16),
                w_out[0].astype(BF16), norm_ffn[0].reshape(1, d), w_ffn_gate[0].astype(BF16),
                w_ffn_up[0].astype(BF16), w_ffn_down[0].astype(BF16), norm_final.reshape(1, d))
    return out.reshape(batch, seq, d)
```

```python
import functools

import numpy as np
import jax
import jax.numpy as jnp
from jax import lax
from jax.experimental import pallas as pl
from jax.experimental.pallas import tpu as pltpu

F32 = jnp.float32
BF16 = jnp.bfloat16

D_MODEL = 1024
NSA_HEADS = 8
NSA_GROUPS = 2
HPG = NSA_HEADS // NSA_GROUPS
DH = 64
CMP_LEN = 32
CMP_STRIDE = 16
CMP_HIDDEN = 128
SLC_LEN = 64
SLC_TOPK = 16
WINDOW = 512
FORCE_SCORE = 1.0e4
GLA_HEADS = 4
GLA_DK = 64
GLA_DV = 128
GLA_RANK = 16
GLA_TAU = 16.0
GLA_CHUNK = 64
GLA_SUB = 16
ROPE_THETA = 500000.0
ROPE_DIM = DH // 4
D_FF = 2816
EPS = 1e-6
NEG = -1e30
OVERFLOW_GUARD = 1e37
LOG2E = 1.4426950408889634
KEY_TILE = 256
IN_ROWS = 512

LANE = 128
VMEM_LIMIT = 56 * 1024 * 1024
SEL_HALF = 64
V_ROWS = DH + 16
MXU_LEAD = 8
NSA_QUERIES = 256
SEL_GROUP = 8

GATE_LANES = 16
LR_LANE = 64


def _dot(a, b):
    return jnp.dot(a, b, preferred_element_type=F32)


def _dot_nt(a, b):
    return lax.dot_general(a, b, (((1,), (1,)), ((), ())), preferred_element_type=F32)


def _sigmoid(x):
    return 1.0 / (1.0 + jnp.exp(-x))


def _in_proj_kernel(x_ref, gain_ref, wqkv_ref, wmisc_ref, wgla_ref, wr_ref, cos_ref, sin_ref,
                    w2_ref, b2_ref,
                    qraw_ref, qrot_ref, cmp_ref, kaug_ref, vslc_ref, kwin_ref, vwin_ref,
                    gate_ref, gq_ref, gk_ref, gv_ref, ga_ref, gr_ref, flat_ref, *, tm, seq):
    i = pl.program_id(0)
    x = x_ref[...]
    ms = jnp.mean(x * x, axis=-1, keepdims=True)
    h = (x * lax.rsqrt(ms + EPS) * gain_ref[...]).astype(BF16)

    rest = jnp.ones((tm, DH - ROPE_DIM), F32)
    cos = jnp.concatenate([cos_ref[...], rest] * (LANE // DH), axis=1)
    sin = jnp.concatenate([sin_ref[...], 0.0 * rest] * (LANE // DH), axis=1)

    def rope(v):
        width = v.shape[1]
        half = ROPE_DIM // 2
        first = lax.broadcasted_iota(jnp.int32, v.shape, 1) % DH < half
        partner = jnp.where(first, pltpu.roll(v, width - half, axis=1), pltpu.roll(v, half, axis=1))
        reps = width // LANE
        c = cos if reps == 1 else jnp.concatenate([cos] * reps, axis=1)
        s = sin if reps == 1 else jnp.concatenate([sin] * reps, axis=1)
        return v * c + partner * s

    scale = DH ** -0.5
    q = _dot(h, wqkv_ref[:, 0:NSA_HEADS * DH])
    qraw_t = (q * (scale * LOG2E)).T
    qrot_t = (rope(q) * (scale * LOG2E)).T
    gw = HPG * DH
    tiles = [slice(u * KEY_TILE, (u + 1) * KEY_TILE) for u in range(tm // KEY_TILE)]
    for g in range(NSA_GROUPS):
        for u, cols in enumerate(tiles):
            qraw_ref[g, u] = qraw_t[g * gw:(g + 1) * gw, cols].astype(BF16)
            qrot_ref[g, u] = qrot_t[g * gw:(g + 1) * gw, cols].astype(BF16)

    kv = _dot(h, wqkv_ref[:, NSA_HEADS * DH:])
    n_rows = tm // CMP_STRIDE
    for j in range(2):
        for g in range(NSA_GROUPS):
            flat_ref[2 * j + g] = kv[:, j * 128 + g * DH:j * 128 + (g + 1) * DH]
    for jg in range(2 * NSA_GROUPS):
        for l in range(0, CMP_STRIDE, 2):
            pair = jnp.concatenate(
                [flat_ref[jg, pl.ds(l, n_rows, stride=CMP_STRIDE), :],
                 flat_ref[jg, pl.ds(l + 1, n_rows, stride=CMP_STRIDE), :]], axis=1)
            cmp_ref[jg, :, l * DH:(l + 2) * DH] = pair.astype(BF16)
    kslc = rope(kv[:, 256:384])
    kwin = rope(kv[:, 512:640])
    vslc_t = kv[:, 384:512].T
    vwin_t = kv[:, 640:768].T
    pos = (i * tm) % seq + lax.broadcasted_iota(jnp.int32, (tm, SEL_HALF), 0)
    blk = (pos // SLC_LEN) % SEL_HALF
    onehot = jnp.where(lax.broadcasted_iota(jnp.int32, (tm, SEL_HALF), 1) == blk, 1.0, 0.0)
    ones_rows = jnp.ones((V_ROWS - DH, KEY_TILE), F32)
    for g in range(NSA_GROUPS):
        sl = slice(g * DH, (g + 1) * DH)
        kaug_ref[g] = jnp.concatenate([kslc[:, sl], onehot], axis=1).astype(BF16)
        kwin_ref[g] = kwin[:, sl].astype(BF16)
        for u, cols in enumerate(tiles):
            vslc_ref[g, u] = jnp.concatenate([vslc_t[sl, cols], ones_rows], axis=0).astype(BF16)
            vwin_ref[g, u] = jnp.concatenate([vwin_t[sl, cols], ones_rows], axis=0).astype(BF16)

    misc = _dot(h, wmisc_ref[...])
    gate_ref[...] = _sigmoid(misc)

    gla = _dot(h, wgla_ref[...])
    gq_ref[...] = (gla[:, 0:256] * (GLA_DK ** -0.5)).astype(BF16)
    gk_ref[...] = gla[:, 256:512].astype(BF16)
    gv_ref[...] = gla[:, 512:1024].astype(BF16)
    r = _dot(h, wr_ref[...])
    gr_ref[...] = (r * _sigmoid(r)).astype(BF16)

    z = _dot(misc.astype(BF16), w2_ref[...]) + b2_ref[...]
    log_sig = jnp.minimum(z, 0.0) - jnp.log1p(jnp.exp(-jnp.abs(z)))
    ga_ref[...] = log_sig * (1.0 / GLA_TAU)


def _in_proj(x2d, gain, weights, cos_t, sin_t, w2p, b2, *, seq, tm=IN_ROWS):
    n = x2d.shape[0]
    nt = seq // tm
    per = tm // KEY_TILE
    tile = lambda i: (0, i, 0, 0)
    row = lambda i: (i, 0)
    grp = lambda i: (0, i, 0)
    const = lambda i: (0, 0)
    tab = lambda i: (i % nt, 0)
    out_shape = (
        jax.ShapeDtypeStruct((NSA_GROUPS, n // KEY_TILE, HPG * DH, KEY_TILE), BF16),
        jax.ShapeDtypeStruct((NSA_GROUPS, n // KEY_TILE, HPG * DH, KEY_TILE), BF16),
        jax.ShapeDtypeStruct((4, n // CMP_STRIDE, CMP_STRIDE * DH), BF16),
        jax.ShapeDtypeStruct((NSA_GROUPS, n, 128), BF16),
        jax.ShapeDtypeStruct((NSA_GROUPS, n // KEY_TILE, V_ROWS, KEY_TILE), BF16),
        jax.ShapeDtypeStruct((NSA_GROUPS, n, DH), BF16),
        jax.ShapeDtypeStruct((NSA_GROUPS, n // KEY_TILE, V_ROWS, KEY_TILE), BF16),
        jax.ShapeDtypeStruct((n, LANE), F32),
        jax.ShapeDtypeStruct((n, 256), BF16),
        jax.ShapeDtypeStruct((n, 256), BF16),
        jax.ShapeDtypeStruct((n, 512), BF16),
        jax.ShapeDtypeStruct((n, 256), F32),
        jax.ShapeDtypeStruct((n, 512), BF16),
    )
    out_specs = (
        pl.BlockSpec((NSA_GROUPS, per, HPG * DH, KEY_TILE), tile),
        pl.BlockSpec((NSA_GROUPS, per, HPG * DH, KEY_TILE), tile),
        pl.BlockSpec((4, tm // CMP_STRIDE, CMP_STRIDE * DH), grp),
        pl.BlockSpec((NSA_GROUPS, tm, 128), grp),
        pl.BlockSpec((NSA_GROUPS, per, V_ROWS, KEY_TILE), tile),
        pl.BlockSpec((NSA_GROUPS, tm, DH), grp),
        pl.BlockSpec((NSA_GROUPS, per, V_ROWS, KEY_TILE), tile),
        pl.BlockSpec((tm, LANE), row),
        pl.BlockSpec((tm, 256), row), pl.BlockSpec((tm, 256), row), pl.BlockSpec((tm, 512), row),
        pl.BlockSpec((tm, 256), row), pl.BlockSpec((tm, 512), row),
    )
    in_specs = [
        pl.BlockSpec((tm, D_MODEL), row),
        pl.BlockSpec((1, D_MODEL), const),
        *[pl.BlockSpec(w.shape, const, pipeline_mode=pl.Buffered(1)) for w in weights],
        pl.BlockSpec((tm, ROPE_DIM), tab),
        pl.BlockSpec((tm, ROPE_DIM), tab),
        pl.BlockSpec((LANE, 256), const),
        pl.BlockSpec((1, 256), const),
    ]
    return pl.pallas_call(
        functools.partial(_in_proj_kernel, tm=tm, seq=seq),
        grid=(n // tm,),
        in_specs=in_specs, out_specs=out_specs, out_shape=out_shape,
        scratch_shapes=[pltpu.VMEM((2 * NSA_GROUPS, tm, DH), F32)],
        compiler_params=pltpu.CompilerParams(
            dimension_semantics=("parallel",), vmem_limit_bytes=VMEM_LIMIT),
        name="in_proj",
    )(x2d, gain, *weights, cos_t, sin_t, w2p, b2)


def _compress_kernel(x_ref, w1_ref, w2_ref, w2t_ref, pe_ref, o_ref, ot_ref, *, n_sub):
    half = CMP_STRIDE * DH
    x = x_ref[...]
    u = _dot(x, w1_ref[0:half, :])
    v = _dot(x, w1_ref[half:2 * half, :])
    c = _dot(pe_ref[...], w1_ref[...])[0:1, :]
    hid = u + pltpu.roll(v, shift=n_sub - 1, axis=0) + c
    hid = (hid * _sigmoid(hid)).astype(BF16)
    out = _dot(hid, w2_ref[...])
    rowi = lax.broadcasted_iota(jnp.int32, out.shape, 0)
    o_ref[...] = jnp.where(rowi < n_sub - 1, out, 0.0).astype(BF16)
    out_t = _dot_nt(w2t_ref[...], hid)
    coli = lax.broadcasted_iota(jnp.int32, out_t.shape, 1)
    ot_ref[0:DH, :] = jnp.where(coli < n_sub - 1, out_t, 0.0).astype(BF16)
    ot_ref[DH:V_ROWS, :] = jnp.ones((V_ROWS - DH, n_sub), BF16)


def _compress(src, w1, w2, w2t, pe, *, batch, n_sub):
    return pl.pallas_call(
        functools.partial(_compress_kernel, n_sub=n_sub),
        grid=(4, batch),
        in_specs=[
            pl.BlockSpec((None, None, n_sub, CMP_STRIDE * DH), lambda j, b: (j, b, 0, 0)),
            pl.BlockSpec((None, CMP_LEN * DH, CMP_HIDDEN), lambda j, b: (j // 2, 0, 0)),
            pl.BlockSpec((None, CMP_HIDDEN, DH), lambda j, b: (j // 2, 0, 0)),
            pl.BlockSpec((None, DH, CMP_HIDDEN), lambda j, b: (j // 2, 0, 0)),
            pl.BlockSpec((None, 8, CMP_LEN * DH), lambda j, b: (j // 2, 0, 0)),
        ],
        out_specs=(pl.BlockSpec((None, None, n_sub, DH), lambda j, b: (j, b, 0, 0)),
                   pl.BlockSpec((None, None, V_ROWS, n_sub), lambda j, b: (j, b, 0, 0))),
        out_shape=(jax.ShapeDtypeStruct((4, batch, n_sub, DH), BF16),
                   jax.ShapeDtypeStruct((4, batch, V_ROWS, n_sub), BF16)),
        compiler_params=pltpu.CompilerParams(
            dimension_semantics=("parallel", "parallel"), vmem_limit_bytes=VMEM_LIMIT),
        name="compress",
    )(src, w1, w2, w2t, pe)


def _softmax_tiles(score_fns, vts, carry, lead=MXU_LEAD):
    n = len(score_fns)
    scores = [fn() for fn in score_fns[:lead]] + [None] * max(n - lead, 0)
    out = []
    for j in range(n):
        m, acc = carry[j]
        m_new = jnp.maximum(m, jnp.max(scores[j], axis=0, keepdims=True))
        p = jnp.exp2((scores[j] - m_new).astype(BF16))
        out.append((m_new, jnp.exp2(m - m_new) * acc + _dot(vts[j], p)))
        if j + lead < n:
            scores[j + lead] = score_fns[j + lead]()
    return tuple(out)


def _nsa_kernel(qraw_ref, qrot_ref, kc_ref, vct_ref, ovt_ref, kaug_ref, vst_ref, kw_ref, vwt_ref,
                gate_ref, o_ref, qa_ref, gt_ref, bias_ref, *, tq, tk, topk):
    i = pl.program_id(2)
    s0 = i * tq
    per_q = tq // tk
    base = i * per_q
    n_cmp = kc_ref.shape[0]
    n_sel = ovt_ref.shape[0]

    def heads_of(ref):
        return [jnp.concatenate([ref[u, h * DH:(h + 1) * DH, :] for u in range(per_q)], axis=1)
                for h in range(HPG)]

    q_raw = heads_of(qraw_ref)
    q_rot = heads_of(qrot_ref)

    def init_carry():
        lane = lax.broadcasted_iota(jnp.int32, (1, tq), 1)
        one = (jnp.where(lane >= 0, NEG, 0.0), jnp.zeros((V_ROWS, tq), F32))
        return tuple(one for _ in range(HPG))

    def normalised(acc):
        return acc[0:DH] / acc[DH:DH + 1]

    kc = kc_ref[...]
    cend = lax.broadcasted_iota(jnp.int32, (n_cmp, tq), 0) * CMP_STRIDE + (CMP_LEN - 1)
    cmask = cend <= s0 + lax.broadcasted_iota(jnp.int32, (n_cmp, tq), 1)
    scores = [_dot(kc, q) for q in q_raw]
    expd, seen = [], []
    for s in scores:
        s = jnp.where(cmask, s, NEG)
        m = jnp.max(s, axis=0, keepdims=True)
        expd.append(jnp.exp2((s - m).astype(BF16)))
        seen.append(m > 0.5 * NEG)
    vct = vct_ref[...]
    ovt = ovt_ref[...]
    oc_t, imp_t = [], 0.0
    for e, ok in zip(expd, seen):
        acc = _dot(vct, e)
        r = jnp.where(ok, 1.0 / acc[DH:DH + 1], 0.0)
        oc_t.append(acc[0:DH] * r)
        imp_t = imp_t + _dot(ovt, e) * r

    diff = (lax.broadcasted_iota(jnp.int32, (tk, tq), 1)
            - lax.broadcasted_iota(jnp.int32, (tk, tq), 0))
    key_blk = lax.broadcasted_iota(jnp.int32, (tk, tq), 0) // SLC_LEN
    qry_blk = lax.broadcasted_iota(jnp.int32, (tk, tq), 1) // SLC_LEN
    causal = [diff >= r * tk for r in range(per_q)]

    n_back = WINDOW // tk
    w_tiles = [jnp.maximum(base - n_back + u, 0) for u in range(n_back)] + [
        base + r for r in range(per_q)]
    w_keys = [kw_ref[pl.ds(pl.multiple_of(t * tk, tk), tk), :] for t in w_tiles]
    w_bands = []
    for u in range(n_back):
        d0 = (n_back - u) * tk
        band = base - n_back + u >= 0
        if d0 + tq - 1 >= WINDOW:
            band = band & (diff < WINDOW - d0)
        w_bands.append(band)
    w_bands += causal
    assert (per_q - 1) * tk + tk - 1 < WINDOW
    w_scores = [lambda q=q: jnp.concatenate(
        [jnp.where(band, _dot(kt, q), NEG) for kt, band in zip(w_keys, w_bands)], axis=0)
        for q in q_rot]
    w_vt = jnp.concatenate([vwt_ref[t] for t in w_tiles], axis=1)
    ow_t = [normalised(acc) for (_, acc) in _softmax_tiles(w_scores, [w_vt] * HPG, init_carry())]

    kt_diag = [kaug_ref[pl.ds(pl.multiple_of((base + r) * tk, tk), tk), 0:DH] for r in range(per_q)]
    diag_raw = [[_dot(kt, q) for kt in kt_diag] for q in q_rot]
    blocks_per_tile = tk // SLC_LEN
    near = [causal[r] & (r * blocks_per_tile + key_blk >= qry_blk - 1) for r in range(per_q)]
    shifts = []
    for per_head in diag_raw:
        top = jnp.max(jnp.where(near[0], per_head[0], NEG), axis=0, keepdims=True)
        for r in range(1, per_q):
            top = jnp.maximum(top, jnp.max(jnp.where(near[r], per_head[r], NEG), axis=0, keepdims=True))
        shifts.append(top.astype(BF16).astype(F32))

    jrow = lax.broadcasted_iota(jnp.int32, (n_sel, tq), 0)
    tlane = s0 + lax.broadcasted_iota(jnp.int32, (n_sel, tq), 1)
    cur = tlane // SLC_LEN
    valid = jrow * SLC_LEN <= tlane
    forced = (jrow == 0) | (jrow == cur) | (jrow == cur - 1)
    score = jnp.where(valid, jnp.where(forced, FORCE_SCORE, imp_t), -jnp.inf)

    n_forced = 3
    jrow_f = jrow.astype(F32)
    start = jnp.where(forced, -jnp.inf, score)
    work = start
    for _ in range(max(topk - n_forced, 0)):
        top = jnp.max(work, axis=0, keepdims=True)
        first = jnp.min(jnp.where(work == top, jrow_f, float(n_sel)), axis=0, keepdims=True)
        work = jnp.where(jrow_f == first, -jnp.inf, work)
    chosen = (work != start) | (forced & valid)
    bias_t = jnp.where(chosen, 0.0, NEG).astype(BF16)

    n_half = n_sel // SEL_HALF
    tiles_per_half = SEL_HALF * SLC_LEN // tk

    def write_weights(shifts):
        for a in range(n_half):
            rows = bias_t[a * SEL_HALF:(a + 1) * SEL_HALF, :]
            for h in range(HPG):
                qa_ref[a, h, 0:DH, :] = q_rot[h]
                qa_ref[a, h, DH:DH + SEL_HALF, :] = (
                    rows if shifts is None else (rows.astype(F32) - shifts[h]).astype(BF16))

    def key_tiles(first, count):
        return kaug_ref[pl.ds(pl.multiple_of(first * tk, tk), count * tk), :]

    def value_tiles(first, count):
        return vst_ref[first] if count == 1 else jnp.concatenate(
            [vst_ref[first + u] for u in range(count)], axis=1)

    bias_ref[...] = bias_t.astype(F32)
    diag_bias = [jnp.concatenate(
        [jnp.broadcast_to(bias_ref[pl.ds((base + r) * blocks_per_tile + b, 1), :], (SLC_LEN, tq))
         for b in range(blocks_per_tile)], axis=0) for r in range(per_q)]
    vt_diag = value_tiles(base, per_q)
    accs = tuple(
        _dot(vt_diag, jnp.concatenate(
            [jnp.exp2((jnp.where(causal[r], per_head[r] + diag_bias[r], NEG) - c).astype(BF16))
             for r in range(per_q)], axis=0))
        for per_head, c in zip(diag_raw, shifts))
    write_weights(shifts)

    def fast_step(first, count, accs):
        kt = key_tiles(first, count)
        vt = value_tiles(first, count)
        half = first // tiles_per_half
        scores = [_dot(kt, qa_ref[half, h]) for h in range(HPG)]
        return tuple(acc + _dot(vt, jnp.exp2(s.astype(BF16))) for s, acc in zip(scores, accs))

    half_group = SEL_GROUP // 2
    n_groups = base // SEL_GROUP
    accs = lax.fori_loop(0, n_groups, lambda t, a: fast_step(t * SEL_GROUP, SEL_GROUP, a), accs)
    done = n_groups * SEL_GROUP
    n_half_groups = (base - done) // half_group
    accs = lax.fori_loop(0, n_half_groups, lambda t, a: fast_step(done, half_group, a), accs)
    done = done + n_half_groups * half_group
    leftovers = [lambda a: a] + [
        functools.partial(lambda a, k: fast_step(done, k, a), k=k) for k in range(1, half_group)]
    accs = lax.switch(base - done, leftovers, accs)

    def rescaled_sweep():
        write_weights(None)

        def tile_scores(t, mask):
            kt = key_tiles(t, 1)
            half = t // tiles_per_half
            return [functools.partial(
                lambda h: _dot(kt, qa_ref[half, h]) if mask is None
                else jnp.where(mask, _dot(kt, qa_ref[half, h]), NEG), h) for h in range(HPG)]

        carry = lax.fori_loop(
            0, base, lambda t, c: _softmax_tiles(tile_scores(t, None), [vst_ref[t]] * HPG, c),
            init_carry())
        for r in range(per_q):
            carry = _softmax_tiles(tile_scores(base + r, causal[r]), [vst_ref[base + r]] * HPG, carry)
        return tuple(normalised(acc) for (_, acc) in carry)

    denominators = jnp.concatenate([acc[DH:DH + 1] for acc in accs], axis=0)
    no_overflow = jnp.max(denominators) < OVERFLOW_GUARD
    os_t = lax.cond(no_overflow, lambda: tuple(normalised(acc) for acc in accs), rescaled_sweep)

    gt_ref[...] = gate_ref[...].T
    gt_t = gt_ref[pl.ds(pl.multiple_of(pl.program_id(1) * GATE_LANES, GATE_LANES), GATE_LANES), :]
    outs = [gt_t[h:h + 1, :] * oc_t[h] + gt_t[HPG + h:HPG + h + 1, :] * os_t[h]
            + gt_t[2 * HPG + h:2 * HPG + h + 1, :] * ow_t[h] for h in range(HPG)]
    o_ref[...] = jnp.concatenate(outs, axis=0).T.astype(BF16)


def _nsa(qraw_t, qrot_t, cmp_kv, cmp_kv_t, ovt, kaug, vslc_t, kwin, vwin_t, gates, *, batch, seq,
         tq=NSA_QUERIES, tk=KEY_TILE):
    n = batch * seq
    nq = seq // tq
    nk = seq // tk
    per_q = tq // tk
    n_cmp = cmp_kv.shape[2]
    n_sel = ovt.shape[0]
    topk = min(SLC_TOPK, seq // SLC_LEN)
    qmap = lambda b, g, i: (b * nq + i, g)
    qtmap = lambda b, g, i: (g, b * nq + i, 0, 0)
    kvmap = lambda b, g, i: (g, b, 0)
    vtmap = lambda b, g, i: (g, b, 0, 0)
    return pl.pallas_call(
        functools.partial(_nsa_kernel, tq=tq, tk=tk, topk=topk),
        grid=(batch, NSA_GROUPS, nq),
        in_specs=[
            pl.BlockSpec((None, per_q, HPG * DH, tk), qtmap),
            pl.BlockSpec((None, per_q, HPG * DH, tk), qtmap),
            pl.BlockSpec((None, None, n_cmp, DH), lambda b, g, i: (g, b, 0, 0)),
            pl.BlockSpec((None, None, V_ROWS, n_cmp), lambda b, g, i: (NSA_GROUPS + g, b, 0, 0)),
            pl.BlockSpec((n_sel, n_cmp), lambda b, g, i: (0, 0)),
            pl.BlockSpec((None, seq, 128), kvmap),
            pl.BlockSpec((None, nk, V_ROWS, tk), vtmap),
            pl.BlockSpec((None, seq, DH), kvmap),
            pl.BlockSpec((None, nk, V_ROWS, tk), vtmap),
            pl.BlockSpec((tq, LANE), lambda b, g, i: (b * nq + i, 0)),
        ],
        out_specs=pl.BlockSpec((tq, HPG * DH), qmap),
        out_shape=jax.ShapeDtypeStruct((n, NSA_HEADS * DH), BF16),
        scratch_shapes=[
            pltpu.VMEM((n_sel // SEL_HALF, HPG, DH + SEL_HALF, tq), BF16),
            pltpu.VMEM((LANE, tq), F32),
            pltpu.VMEM((n_sel, tq), F32),
        ],
        compiler_params=pltpu.CompilerParams(
            dimension_semantics=("parallel", "parallel", "arbitrary"),
            vmem_limit_bytes=VMEM_LIMIT),
        name="nsa",
    )(qraw_t, qrot_t, cmp_kv, cmp_kv_t, ovt, kaug, vslc_t, kwin, vwin_t, gates)


def _gla_kernel(q_ref, k_ref, v_ref, g_ref, r_ref, gn_ref, o_ref, st_ref, *, tm, chunk):
    nb = q_ref.shape[0]
    n_chunk = tm // chunk
    n_sub = chunk // GLA_SUB
    dk_all = GLA_HEADS * GLA_DK
    dv_all = GLA_HEADS * GLA_DV

    @pl.when(pl.program_id(0) == 0)
    def _():
        st_ref[...] = jnp.zeros(st_ref.shape, F32)

    def iota(shape, axis):
        return lax.broadcasted_iota(jnp.int32, shape, axis)

    tril = jnp.where((iota((tm, tm), 0) >= iota((tm, tm), 1))
                     & (iota((tm, tm), 0) // chunk == iota((tm, tm), 1) // chunk), 1.0, 0.0).astype(BF16)
    k_head = iota((chunk, dk_all), 1) // GLA_DK
    v_head = iota((chunk, dv_all), 1) // GLA_DV
    a_rows = iota((n_sub * chunk, GLA_HEADS * chunk), 0)
    a_key = iota((n_sub * chunk, GLA_HEADS * chunk), 1) % chunk
    a_keep = (a_key // GLA_SUB == a_rows // chunk) & (a_key <= a_rows % chunk)
    st_keep = iota((dv_all, dk_all), 0) // GLA_DV == iota((dv_all, dk_all), 1) // GLA_DK
    gn = jnp.concatenate([gn_ref[...]] * GLA_HEADS, axis=1)

    work = []
    for b in range(nb):
        g = g_ref[b]
        g_hi = g.astype(BF16)
        g_lo = (g - g_hi.astype(F32)).astype(BF16)
        cum_all = _dot(tril, g_hi) + _dot(tril, g_lo)
        for c in range(n_chunk):
            rows = slice(c * chunk, (c + 1) * chunk)
            cum = cum_all[rows]
            q = q_ref[b, rows, :].astype(F32)
            k = k_ref[b, rows, :].astype(F32)
            v = v_ref[b, rows, :]
            last = cum[chunk - 1:chunk, :]
            refs = [cum[(j + 1) * GLA_SUB - 1:(j + 1) * GLA_SUB, :] for j in range(n_sub)]
            k_hat = jnp.concatenate(
                [k[j * GLA_SUB:(j + 1) * GLA_SUB] * jnp.exp(refs[j] - cum[j * GLA_SUB:(j + 1) * GLA_SUB])
                 for j in range(n_sub)], axis=0)
            q_hat = jnp.concatenate(
                [q * jnp.exp(jnp.minimum(cum - refs[j], 80.0)) for j in range(n_sub)], axis=0)
            k_stack = jnp.concatenate(
                [jnp.where(k_head == h, k_hat, 0.0) for h in range(GLA_HEADS)], axis=0)
            a_all = _dot_nt(q_hat.astype(BF16), k_stack.astype(BF16))
            k_dec = (k * jnp.exp(last - cum)).astype(BF16)
            v_t = v.astype(F32).T.astype(BF16)
            st_inc = _dot(v_t, k_dec)
            work.append(dict(b=b, rows=rows, a_all=a_all, st_inc=st_inc, v=v,
                             q_in=(q * jnp.exp(cum)).astype(BF16), decay=jnp.exp(last)))

    states = [st_ref[b] for b in range(nb)]
    for w in work:
        st = states[w["b"]]
        w["o_inter"] = _dot_nt(w["q_in"], st.astype(BF16))
        states[w["b"]] = st * w["decay"] + jnp.where(st_keep, w["st_inc"], 0.0)
    for b in range(nb):
        st_ref[b] = states[b]

    for w in work:
        a_m = jnp.where(a_keep, w["a_all"], 0.0)
        a = a_m[0:chunk]
        for j in range(1, n_sub):
            a = a + a_m[j * chunk:(j + 1) * chunk]
        v_blocks = jnp.concatenate(
            [jnp.where(v_head == h, w["v"], jnp.zeros_like(w["v"])) for h in range(GLA_HEADS)], axis=0)
        o = w["o_inter"] + _dot(a.astype(BF16), v_blocks)
        scale = jnp.concatenate(
            [jnp.broadcast_to(lax.rsqrt(jnp.mean(
                o[:, h * GLA_DV:(h + 1) * GLA_DV] ** 2, axis=-1, keepdims=True) + EPS), (chunk, GLA_DV))
             for h in range(GLA_HEADS)], axis=1)
        y = o * scale * gn * r_ref[w["b"], w["rows"], :].astype(F32)
        o_ref[w["b"], w["rows"], :] = y.astype(BF16)


def _gla(gq, gk, gv, ga, gr, gnorm, *, batch, seq, tm=KEY_TILE, chunk=GLA_CHUNK):
    tmap = lambda i: (0, i, 0)
    dk_all = GLA_HEADS * GLA_DK
    dv_all = GLA_HEADS * GLA_DV
    return pl.pallas_call(
        functools.partial(_gla_kernel, tm=tm, chunk=chunk),
        grid=(seq // tm,),
        in_specs=[
            pl.BlockSpec((batch, tm, dk_all), tmap),
            pl.BlockSpec((batch, tm, dk_all), tmap),
            pl.BlockSpec((batch, tm, dv_all), tmap),
            pl.BlockSpec((batch, tm, dk_all), tmap),
            pl.BlockSpec((batch, tm, dv_all), tmap),
            pl.BlockSpec((1, GLA_DV), lambda i: (0, 0)),
        ],
        out_specs=pl.BlockSpec((batch, tm, dv_all), tmap),
        out_shape=jax.ShapeDtypeStruct((batch, seq, dv_all), BF16),
        scratch_shapes=[pltpu.VMEM((batch, dv_all, dk_all), F32)],
        compiler_params=pltpu.CompilerParams(
            dimension_semantics=("arbitrary",), vmem_limit_bytes=VMEM_LIMIT),
        name="gla",
    )(gq, gk, gv, ga, gr, gnorm)


def _tail_kernel(x_ref, oa_ref, ob_ref, g0_ref, wm_ref, wn_ref, wb_ref, wo_ref,
                 g1_ref, wg_ref, wu_ref, wd_ref, g2_ref, o_ref):
    x = x_ref[...]

    def rms(v, gain_ref):
        return v * lax.rsqrt(jnp.mean(v * v, axis=-1, keepdims=True) + EPS) * gain_ref[...]

    gates = _sigmoid(_dot(rms(x, g0_ref).astype(BF16), wm_ref[...]))
    merged = (gates[:, 0:D_MODEL] * _dot(oa_ref[...], wn_ref[...])
              + gates[:, D_MODEL:2 * D_MODEL] * _dot(ob_ref[...], wb_ref[...]))
    x1 = x + _dot(merged.astype(BF16), wo_ref[...])
    h = rms(x1, g1_ref).astype(BF16)
    a = _dot(h, wg_ref[...])
    u = _dot(h, wu_ref[...])
    y = x1 + _dot((a * _sigmoid(a) * u).astype(BF16), wd_ref[...])
    o_ref[...] = rms(y, g2_ref)


def _tail(x2d, oa, ob, g0, wm, wn, wb, wo, g1, wg, wu, wd, g2, *, tm=256):
    n = x2d.shape[0]
    row = lambda i: (i, 0)
    const = lambda i: (0, 0)
    weight = lambda r, c: pl.BlockSpec((r, c), const, pipeline_mode=pl.Buffered(1))
    return pl.pallas_call(
        _tail_kernel,
        grid=(n // tm,),
        in_specs=[
            pl.BlockSpec((tm, D_MODEL), row),
            pl.BlockSpec((tm, NSA_HEADS * DH), row),
            pl.BlockSpec((tm, GLA_HEADS * GLA_DV), row),
            weight(1, D_MODEL), weight(D_MODEL, 2 * D_MODEL),
            weight(NSA_HEADS * DH, D_MODEL), weight(GLA_HEADS * GLA_DV, D_MODEL),
            weight(D_MODEL, D_MODEL),
            weight(1, D_MODEL), weight(D_MODEL, D_FF), weight(D_MODEL, D_FF), weight(D_FF, D_MODEL),
            weight(1, D_MODEL),
        ],
        out_specs=pl.BlockSpec((tm, D_MODEL), row),
        out_shape=jax.ShapeDtypeStruct((n, D_MODEL), F32),
        compiler_params=pltpu.CompilerParams(
            dimension_semantics=("parallel",), vmem_limit_bytes=VMEM_LIMIT),
        name="tail",
    )(x2d, oa, ob, g0, wm, wn, wb, wo, g1, wg, wu, wd, g2)


def _rope_tables(seq):
    half = ROPE_DIM // 2
    inv_freq = ROPE_THETA ** (-jnp.arange(half, dtype=F32) / half)
    ang = jnp.arange(seq).astype(F32)[:, None] * inv_freq[None, :]
    cos = jnp.cos(ang)
    sin = jnp.sin(ang)
    return jnp.concatenate([cos, cos], axis=1), jnp.concatenate([-sin, sin], axis=1)


def _overlap_t(n_sel, n_cmp_pad, n_slc, n_cmp):
    c_start = np.arange(n_cmp_pad)[None, :] * CMP_STRIDE
    s_start = np.arange(n_sel)[:, None] * SLC_LEN
    ov = (c_start < s_start + SLC_LEN) & (c_start + CMP_LEN > s_start)
    ov &= (np.arange(n_cmp_pad)[None, :] < n_cmp) & (np.arange(n_sel)[:, None] < n_slc)
    return jnp.asarray(ov, dtype=BF16)


def _split_in_weights(w_in):
    c = 0
    w_qkv = w_in[:, c:c + 1280]; c += 1280
    wgate = w_in[:, c:c + 24]; c += 24
    w_gla = w_in[:, c:c + 1024]; c += 1024
    wlr = w_in[:, c:c + GLA_RANK]; c += GLA_RANK
    w_r = w_in[:, c:c + 512]; c += 512
    w_merge = w_in[:, c:c + 2048]
    d = w_in.shape[0]
    wg3 = wgate.reshape(d, NSA_GROUPS, HPG, 3).transpose(0, 1, 3, 2).reshape(d, NSA_GROUPS, 3 * HPG)
    wg_pad = jnp.pad(wg3, ((0, 0), (0, 0), (0, GATE_LANES - 3 * HPG))).reshape(d, NSA_GROUPS * GATE_LANES)
    w_misc = jnp.concatenate([
        wg_pad, jnp.zeros((d, LR_LANE - NSA_GROUPS * GATE_LANES), F32),
        wlr, jnp.zeros((d, LANE - LR_LANE - GLA_RANK), F32)], axis=1)
    return tuple(w.astype(BF16) for w in (w_qkv, w_misc, w_gla, w_r, w_merge))


def kernel(x, norm_mix, w_in, cmp_pe_k, cmp_pe_v, cmp_k_w1, cmp_k_w2, cmp_v_w1, cmp_v_w2,
           gla_gate_w2, gla_gate_b, gla_norm, w_up_nsa, w_up_gla, w_out, norm_ffn,
           w_ffn_gate, w_ffn_up, w_ffn_down, norm_final):
    batch, seq, d = x.shape
    assert d == D_MODEL and w_in.shape[0] == 1
    assert seq % IN_ROWS == 0 and seq % NSA_QUERIES == 0
    n = batch * seq
    x2d = x.reshape(n, d)

    *in_weights, w_merge = _split_in_weights(w_in[0])
    cos_t, sin_t = _rope_tables(seq)
    w2p = jnp.pad(gla_gate_w2[0], ((LR_LANE, LANE - LR_LANE - GLA_RANK), (0, 0))).astype(BF16)
    b2 = gla_gate_b[0].reshape(1, -1)
    gain_mix = norm_mix[0].reshape(1, d)

    (qraw, qrot, cmp_src, kaug, vslc_t, kwin, vwin_t, gates, gq, gk, gv, ga, gr) = _in_proj(
        x2d, gain_mix, in_weights, cos_t, sin_t, w2p, b2, seq=seq)

    n_sub = seq // CMP_STRIDE
    n_cmp = n_sub - CMP_LEN // CMP_STRIDE + 1
    n_slc = seq // SLC_LEN
    cmp_in = cmp_src.reshape(4, batch, n_sub, CMP_STRIDE * DH)
    w1 = jnp.stack([cmp_k_w1[0], cmp_v_w1[0]]).astype(BF16)
    w2 = jnp.stack([cmp_k_w2[0], cmp_v_w2[0]]).astype(BF16)
    pe = jnp.stack([cmp_pe_k[0].reshape(1, -1), cmp_pe_v[0].reshape(1, -1)])
    pe = jnp.broadcast_to(pe, (2, 8, CMP_LEN * DH)).astype(BF16)
    w2t = jnp.stack([cmp_k_w2[0].T, cmp_v_w2[0].T]).astype(BF16)
    cmp_kv, cmp_kv_t = _compress(cmp_in, w1, w2, w2t, pe, batch=batch, n_sub=n_sub)

    n_sel = -(-n_slc // SEL_HALF) * SEL_HALF
    ovt = _overlap_t(n_sel, n_sub, n_slc, n_cmp)
    o_a = _nsa(qraw, qrot, cmp_kv, cmp_kv_t, ovt, kaug, vslc_t, kwin, vwin_t, gates,
               batch=batch, seq=seq)

    per_seq = lambda a: a.reshape(batch, seq, a.shape[-1])
    o_b = _gla(per_seq(gq), per_seq(gk), per_seq(gv), per_seq(ga), per_seq(gr),
               gla_norm[0].reshape(1, -1), batch=batch, seq=seq).reshape(n, -1)

    out = _tail(x2d, o_a, o_b, gain_mix, w_merge, w_up_nsa[0].astype(BF16), w_up_gla[0].astype(BF16),
                w_out[0].astype(BF16), norm_ffn[0].reshape(1, d), w_ffn_gate[0].astype(BF16),
                w_ffn_up[0].astype(BF16), w_ffn_down[0].astype(BF16), norm_final.reshape(1, d))
    return out.reshape(batch, seq, d)
```

```python
import functools

import numpy as np
import jax
import jax.numpy as jnp
from jax import lax
from jax.experimental import pallas as pl
from jax.experimental.pallas import tpu as pltpu

F32 = jnp.float32
BF16 = jnp.bfloat16

D_MODEL = 1024
NSA_HEADS = 8
NSA_GROUPS = 2
HPG = NSA_HEADS // NSA_GROUPS
DH = 64
CMP_LEN = 32
CMP_STRIDE = 16
CMP_HIDDEN = 128
SLC_LEN = 64
SLC_TOPK = 16
WINDOW = 512
FORCE_SCORE = 1.0e4
GLA_HEADS = 4
GLA_DK = 64
GLA_DV = 128
GLA_RANK = 16
GLA_TAU = 16.0
GLA_CHUNK = 64
GLA_SUB = 16
ROPE_THETA = 500000.0
ROPE_DIM = DH // 4
D_FF = 2816
EPS = 1e-6
NEG = -1e30
OVERFLOW_GUARD = 1e37
LOG2E = 1.4426950408889634
KEY_TILE = 256
IN_ROWS = 512

LANE = 128
VMEM_LIMIT = 56 * 1024 * 1024
SEL_HALF = 64
V_ROWS = DH + 16
NSA_QUERIES = 256
SEL_GROUP = 8

GATE_LANES = 16
LR_LANE = 64


def _dot(a, b):
    return jnp.dot(a, b, preferred_element_type=F32)


def _dot_nt(a, b):
    return lax.dot_general(a, b, (((1,), (1,)), ((), ())), preferred_element_type=F32)


def _sigmoid(x):
    return 1.0 / (1.0 + jnp.exp(-x))


def _in_proj_kernel(x_ref, gain_ref, wqkv_ref, wmisc_ref, wgla_ref, wr_ref, cos_ref, sin_ref,
                    w2_ref, b2_ref,
                    qraw_ref, qrot_ref, cmp_ref, kaug_ref, vslc_ref, kwin_ref, vwin_ref,
                    gate_ref, gq_ref, gk_ref, gv_ref, ga_ref, gr_ref, flat_ref, *, tm, seq):
    i = pl.program_id(0)
    x = x_ref[...]
    ms = jnp.mean(x * x, axis=-1, keepdims=True)
    h = (x * lax.rsqrt(ms + EPS) * gain_ref[...]).astype(BF16)

    rest = jnp.ones((tm, DH - ROPE_DIM), F32)
    cos = jnp.concatenate([cos_ref[...], rest] * (LANE // DH), axis=1)
    sin = jnp.concatenate([sin_ref[...], 0.0 * rest] * (LANE // DH), axis=1)

    def rope(v):
        width = v.shape[1]
        half = ROPE_DIM // 2
        first = lax.broadcasted_iota(jnp.int32, v.shape, 1) % DH < half
        partner = jnp.where(first, pltpu.roll(v, width - half, axis=1), pltpu.roll(v, half, axis=1))
        reps = width // LANE
        c = cos if reps == 1 else jnp.concatenate([cos] * reps, axis=1)
        s = sin if reps == 1 else jnp.concatenate([sin] * reps, axis=1)
        return v * c + partner * s

    scale = DH ** -0.5
    q = _dot(h, wqkv_ref[:, 0:NSA_HEADS * DH])
    qraw_t = (q * (scale * LOG2E)).T
    qrot_t = (rope(q) * (scale * LOG2E)).T
    gw = HPG * DH
    tiles = [slice(u * KEY_TILE, (u + 1) * KEY_TILE) for u in range(tm // KEY_TILE)]
    for g in range(NSA_GROUPS):
        for u, cols in enumerate(tiles):
            qraw_ref[g, u] = qraw_t[g * gw:(g + 1) * gw, cols].astype(BF16)
            qrot_ref[g, u] = qrot_t[g * gw:(g + 1) * gw, cols].astype(BF16)

    kv = _dot(h, wqkv_ref[:, NSA_HEADS * DH:])
    n_rows = tm // CMP_STRIDE
    for j in range(2):
        for g in range(NSA_GROUPS):
            flat_ref[2 * j + g] = kv[:, j * 128 + g * DH:j * 128 + (g + 1) * DH]
    for jg in range(2 * NSA_GROUPS):
        for l in range(0, CMP_STRIDE, 2):
            pair = jnp.concatenate(
                [flat_ref[jg, pl.ds(l, n_rows, stride=CMP_STRIDE), :],
                 flat_ref[jg, pl.ds(l + 1, n_rows, stride=CMP_STRIDE), :]], axis=1)
            cmp_ref[jg, :, l * DH:(l + 2) * DH] = pair.astype(BF16)
    kslc = rope(kv[:, 256:384])
    kwin = rope(kv[:, 512:640])
    vslc_t = kv[:, 384:512].T
    vwin_t = kv[:, 640:768].T
    pos = (i * tm) % seq + lax.broadcasted_iota(jnp.int32, (tm, SEL_HALF), 0)
    blk = (pos // SLC_LEN) % SEL_HALF
    onehot = jnp.where(lax.broadcasted_iota(jnp.int32, (tm, SEL_HALF), 1) == blk, 1.0, 0.0)
    ones_rows = jnp.ones((V_ROWS - DH, KEY_TILE), F32)
    for g in range(NSA_GROUPS):
        sl = slice(g * DH, (g + 1) * DH)
        kaug_ref[g] = jnp.concatenate([kslc[:, sl], onehot], axis=1).astype(BF16)
        kwin_ref[g] = kwin[:, sl].astype(BF16)
        for u, cols in enumerate(tiles):
            vslc_ref[g, u] = jnp.concatenate([vslc_t[sl, cols], ones_rows], axis=0).astype(BF16)
            vwin_ref[g, u] = jnp.concatenate([vwin_t[sl, cols], ones_rows], axis=0).astype(BF16)

    misc = _dot(h, wmisc_ref[...])
    gate_ref[...] = _sigmoid(misc)

    gla = _dot(h, wgla_ref[...])
    gq_ref[...] = (gla[:, 0:256] * (GLA_DK ** -0.5)).astype(BF16)
    gk_ref[...] = gla[:, 256:512].astype(BF16)
    gv_ref[...] = gla[:, 512:1024].astype(BF16)
    r = _dot(h, wr_ref[...])
    gr_ref[...] = (r * _sigmoid(r)).astype(BF16)

    z = _dot(misc.astype(BF16), w2_ref[...]) + b2_ref[...]
    log_sig = jnp.minimum(z, 0.0) - jnp.log1p(jnp.exp(-jnp.abs(z)))
    ga_ref[...] = log_sig * (1.0 / GLA_TAU)


def _in_proj(x2d, gain, weights, cos_t, sin_t, w2p, b2, *, seq, tm=IN_ROWS):
    n = x2d.shape[0]
    nt = seq // tm
    per = tm // KEY_TILE
    tile = lambda i: (0, i, 0, 0)
    row = lambda i: (i, 0)
    grp = lambda i: (0, i, 0)
    const = lambda i: (0, 0)
    tab = lambda i: (i % nt, 0)
    out_shape = (
        jax.ShapeDtypeStruct((NSA_GROUPS, n // KEY_TILE, HPG * DH, KEY_TILE), BF16),
        jax.ShapeDtypeStruct((NSA_GROUPS, n // KEY_TILE, HPG * DH, KEY_TILE), BF16),
        jax.ShapeDtypeStruct((4, n // CMP_STRIDE, CMP_STRIDE * DH), BF16),
        jax.ShapeDtypeStruct((NSA_GROUPS, n, 128), BF16),
        jax.ShapeDtypeStruct((NSA_GROUPS, n // KEY_TILE, V_ROWS, KEY_TILE), BF16),
        jax.ShapeDtypeStruct((NSA_GROUPS, n, DH), BF16),
        jax.ShapeDtypeStruct((NSA_GROUPS, n // KEY_TILE, V_ROWS, KEY_TILE), BF16),
        jax.ShapeDtypeStruct((n, LANE), F32),
        jax.ShapeDtypeStruct((n, 256), BF16),
        jax.ShapeDtypeStruct((n, 256), BF16),
        jax.ShapeDtypeStruct((n, 512), BF16),
        jax.ShapeDtypeStruct((n, 256), F32),
        jax.ShapeDtypeStruct((n, 512), BF16),
    )
    out_specs = (
        pl.BlockSpec((NSA_GROUPS, per, HPG * DH, KEY_TILE), tile),
        pl.BlockSpec((NSA_GROUPS, per, HPG * DH, KEY_TILE), tile),
        pl.BlockSpec((4, tm // CMP_STRIDE, CMP_STRIDE * DH), grp),
        pl.BlockSpec((NSA_GROUPS, tm, 128), grp),
        pl.BlockSpec((NSA_GROUPS, per, V_ROWS, KEY_TILE), tile),
        pl.BlockSpec((NSA_GROUPS, tm, DH), grp),
        pl.BlockSpec((NSA_GROUPS, per, V_ROWS, KEY_TILE), tile),
        pl.BlockSpec((tm, LANE), row),
        pl.BlockSpec((tm, 256), row), pl.BlockSpec((tm, 256), row), pl.BlockSpec((tm, 512), row),
        pl.BlockSpec((tm, 256), row), pl.BlockSpec((tm, 512), row),
    )
    in_specs = [
        pl.BlockSpec((tm, D_MODEL), row),
        pl.BlockSpec((1, D_MODEL), const),
        *[pl.BlockSpec(w.shape, const, pipeline_mode=pl.Buffered(1)) for w in weights],
        pl.BlockSpec((tm, ROPE_DIM), tab),
        pl.BlockSpec((tm, ROPE_DIM), tab),
        pl.BlockSpec((LANE, 256), const),
        pl.BlockSpec((1, 256), const),
    ]
    return pl.pallas_call(
        functools.partial(_in_proj_kernel, tm=tm, seq=seq),
        grid=(n // tm,),
        in_specs=in_specs, out_specs=out_specs, out_shape=out_shape,
        scratch_shapes=[pltpu.VMEM((2 * NSA_GROUPS, tm, DH), F32)],
        compiler_params=pltpu.CompilerParams(
            dimension_semantics=("parallel",), vmem_limit_bytes=VMEM_LIMIT),
        name="in_proj",
    )(x2d, gain, *weights, cos_t, sin_t, w2p, b2)


def _compress_kernel(x_ref, w1_ref, w2_ref, w2t_ref, pe_ref, o_ref, ot_ref, *, n_sub):
    half = CMP_STRIDE * DH
    x = x_ref[...]
    u = _dot(x, w1_ref[0:half, :])
    v = _dot(x, w1_ref[half:2 * half, :])
    c = _dot(pe_ref[...], w1_ref[...])[0:1, :]
    hid = u + pltpu.roll(v, shift=n_sub - 1, axis=0) + c
    hid = (hid * _sigmoid(hid)).astype(BF16)
    out = _dot(hid, w2_ref[...])
    rowi = lax.broadcasted_iota(jnp.int32, out.shape, 0)
    o_ref[...] = jnp.where(rowi < n_sub - 1, out, 0.0).astype(BF16)
    out_t = _dot_nt(w2t_ref[...], hid)
    coli = lax.broadcasted_iota(jnp.int32, out_t.shape, 1)
    ot_ref[0:DH, :] = jnp.where(coli < n_sub - 1, out_t, 0.0).astype(BF16)
    ot_ref[DH:V_ROWS, :] = jnp.ones((V_ROWS - DH, n_sub), BF16)


def _compress(src, w1, w2, w2t, pe, *, batch, n_sub):
    return pl.pallas_call(
        functools.partial(_compress_kernel, n_sub=n_sub),
        grid=(4, batch),
        in_specs=[
            pl.BlockSpec((None, None, n_sub, CMP_STRIDE * DH), lambda j, b: (j, b, 0, 0)),
            pl.BlockSpec((None, CMP_LEN * DH, CMP_HIDDEN), lambda j, b: (j // 2, 0, 0)),
            pl.BlockSpec((None, CMP_HIDDEN, DH), lambda j, b: (j // 2, 0, 0)),
            pl.BlockSpec((None, DH, CMP_HIDDEN), lambda j, b: (j // 2, 0, 0)),
            pl.BlockSpec((None, 8, CMP_LEN * DH), lambda j, b: (j // 2, 0, 0)),
        ],
        out_specs=(pl.BlockSpec((None, None, n_sub, DH), lambda j, b: (j, b, 0, 0)),
                   pl.BlockSpec((None, None, V_ROWS, n_sub), lambda j, b: (j, b, 0, 0))),
        out_shape=(jax.ShapeDtypeStruct((4, batch, n_sub, DH), BF16),
                   jax.ShapeDtypeStruct((4, batch, V_ROWS, n_sub), BF16)),
        compiler_params=pltpu.CompilerParams(
            dimension_semantics=("parallel", "parallel"), vmem_limit_bytes=VMEM_LIMIT),
        name="compress",
    )(src, w1, w2, w2t, pe)


def _softmax_tiles(score_fns, vts, carry):
    scores = [fn() for fn in score_fns]
    out = []
    for s, vt, (m, acc) in zip(scores, vts, carry):
        m_new = jnp.maximum(m, jnp.max(s, axis=0, keepdims=True))
        p = jnp.exp2((s - m_new).astype(BF16))
        out.append((m_new, jnp.exp2(m - m_new) * acc + _dot(vt, p)))
    return tuple(out)


def _nsa_kernel(qraw_ref, qrot_ref, kc_ref, vct_ref, ovt_ref, kaug_ref, vst_ref, kw_ref, vwt_ref,
                gate_ref, o_ref, qa_ref, gt_ref, bias_ref, *, tq, tk, topk):
    i = pl.program_id(2)
    s0 = i * tq
    per_q = tq // tk
    base = i * per_q
    n_cmp = kc_ref.shape[0]
    n_sel = ovt_ref.shape[0]

    def heads_of(ref):
        return [jnp.concatenate([ref[u, h * DH:(h + 1) * DH, :] for u in range(per_q)], axis=1)
                for h in range(HPG)]

    q_raw = heads_of(qraw_ref)
    q_rot = heads_of(qrot_ref)

    def init_carry():
        lane = lax.broadcasted_iota(jnp.int32, (1, tq), 1)
        one = (jnp.where(lane >= 0, NEG, 0.0), jnp.zeros((V_ROWS, tq), F32))
        return tuple(one for _ in range(HPG))

    def normalised(acc):
        return acc[0:DH] / acc[DH:DH + 1]

    kc = kc_ref[...]
    cend = lax.broadcasted_iota(jnp.int32, (n_cmp, tq), 0) * CMP_STRIDE + (CMP_LEN - 1)
    cmask = cend <= s0 + lax.broadcasted_iota(jnp.int32, (n_cmp, tq), 1)
    scores = [_dot(kc, q) for q in q_raw]
    expd, seen = [], []
    for s in scores:
        s = jnp.where(cmask, s, NEG)
        m = jnp.max(s, axis=0, keepdims=True)
        expd.append(jnp.exp2((s - m).astype(BF16)))
        seen.append(m > 0.5 * NEG)
    vct = vct_ref[...]
    ovt = ovt_ref[...]
    oc_t, imp_t = [], 0.0
    for e, ok in zip(expd, seen):
        acc = _dot(vct, e)
        r = jnp.where(ok, 1.0 / acc[DH:DH + 1], 0.0)
        oc_t.append(acc[0:DH] * r)
        imp_t = imp_t + _dot(ovt, e) * r

    diff = (lax.broadcasted_iota(jnp.int32, (tk, tq), 1)
            - lax.broadcasted_iota(jnp.int32, (tk, tq), 0))
    key_blk = lax.broadcasted_iota(jnp.int32, (tk, tq), 0) // SLC_LEN
    qry_blk = lax.broadcasted_iota(jnp.int32, (tk, tq), 1) // SLC_LEN
    causal = [diff >= r * tk for r in range(per_q)]

    n_back = WINDOW // tk
    w_tiles = [jnp.maximum(base - n_back + u, 0) for u in range(n_back)] + [
        base + r for r in range(per_q)]
    w_keys = [kw_ref[pl.ds(pl.multiple_of(t * tk, tk), tk), :] for t in w_tiles]
    w_bands = []
    for u in range(n_back):
        d0 = (n_back - u) * tk
        band = base - n_back + u >= 0
        if d0 + tq - 1 >= WINDOW:
            band = band & (diff < WINDOW - d0)
        w_bands.append(band)
    w_bands += causal
    assert (per_q - 1) * tk + tk - 1 < WINDOW
    w_scores = [lambda q=q: jnp.concatenate(
        [jnp.where(band, _dot(kt, q), NEG) for kt, band in zip(w_keys, w_bands)], axis=0)
        for q in q_rot]
    w_vt = jnp.concatenate([vwt_ref[t] for t in w_tiles], axis=1)
    ow_t = [normalised(acc) for (_, acc) in _softmax_tiles(w_scores, [w_vt] * HPG, init_carry())]

    kt_diag = [kaug_ref[pl.ds(pl.multiple_of((base + r) * tk, tk), tk), 0:DH] for r in range(per_q)]
    diag_raw = [[_dot(kt, q) for kt in kt_diag] for q in q_rot]
    blocks_per_tile = tk // SLC_LEN
    near = [causal[r] & (r * blocks_per_tile + key_blk >= qry_blk - 1) for r in range(per_q)]
    shifts = []
    for per_head in diag_raw:
        top = jnp.max(jnp.where(near[0], per_head[0], NEG), axis=0, keepdims=True)
        for r in range(1, per_q):
            top = jnp.maximum(top, jnp.max(jnp.where(near[r], per_head[r], NEG), axis=0, keepdims=True))
        shifts.append(top.astype(BF16).astype(F32))

    jrow = lax.broadcasted_iota(jnp.int32, (n_sel, tq), 0)
    tlane = s0 + lax.broadcasted_iota(jnp.int32, (n_sel, tq), 1)
    cur = tlane // SLC_LEN
    valid = jrow * SLC_LEN <= tlane
    forced = (jrow == 0) | (jrow == cur) | (jrow == cur - 1)
    score = jnp.where(valid, jnp.where(forced, FORCE_SCORE, imp_t), -jnp.inf)

    n_forced = 3
    jrow_f = jrow.astype(F32)
    start = jnp.where(forced, -jnp.inf, score)
    work = start
    for _ in range(max(topk - n_forced, 0)):
        top = jnp.max(work, axis=0, keepdims=True)
        first = jnp.min(jnp.where(work == top, jrow_f, float(n_sel)), axis=0, keepdims=True)
        work = jnp.where(jrow_f == first, -jnp.inf, work)
    chosen = (work != start) | (forced & valid)
    bias_t = jnp.where(chosen, 0.0, NEG).astype(BF16)

    n_half = n_sel // SEL_HALF
    tiles_per_half = SEL_HALF * SLC_LEN // tk

    def write_weights(shifts):
        for a in range(n_half):
            rows = bias_t[a * SEL_HALF:(a + 1) * SEL_HALF, :]
            for h in range(HPG):
                qa_ref[a, h, 0:DH, :] = q_rot[h]
                qa_ref[a, h, DH:DH + SEL_HALF, :] = (
                    rows if shifts is None else (rows.astype(F32) - shifts[h]).astype(BF16))

    def key_tiles(first, count):
        return kaug_ref[pl.ds(pl.multiple_of(first * tk, tk), count * tk), :]

    def value_tiles(first, count):
        return vst_ref[first] if count == 1 else jnp.concatenate(
            [vst_ref[first + u] for u in range(count)], axis=1)

    bias_ref[...] = bias_t.astype(F32)
    diag_bias = [jnp.concatenate(
        [jnp.broadcast_to(bias_ref[pl.ds((base + r) * blocks_per_tile + b, 1), :], (SLC_LEN, tq))
         for b in range(blocks_per_tile)], axis=0) for r in range(per_q)]
    vt_diag = value_tiles(base, per_q)
    accs = tuple(
        _dot(vt_diag, jnp.concatenate(
            [jnp.exp2((jnp.where(causal[r], per_head[r] + diag_bias[r], NEG) - c).astype(BF16))
             for r in range(per_q)], axis=0))
        for per_head, c in zip(diag_raw, shifts))
    write_weights(shifts)

    def fast_step(first, count, accs):
        kt = key_tiles(first, count)
        vt = value_tiles(first, count)
        half = first // tiles_per_half
        scores = [_dot(kt, qa_ref[half, h]) for h in range(HPG)]
        return tuple(acc + _dot(vt, jnp.exp2(s.astype(BF16))) for s, acc in zip(scores, accs))

    half_group = SEL_GROUP // 2
    n_groups = base // SEL_GROUP
    accs = lax.fori_loop(0, n_groups, lambda t, a: fast_step(t * SEL_GROUP, SEL_GROUP, a), accs)
    done = n_groups * SEL_GROUP
    n_half_groups = (base - done) // half_group
    accs = lax.fori_loop(0, n_half_groups, lambda t, a: fast_step(done, half_group, a), accs)
    done = done + n_half_groups * half_group
    leftovers = [lambda a: a] + [
        functools.partial(lambda a, k: fast_step(done, k, a), k=k) for k in range(1, half_group)]
    accs = lax.switch(base - done, leftovers, accs)

    def rescaled_sweep():
        write_weights(None)

        def tile_scores(t, mask):
            kt = key_tiles(t, 1)
            half = t // tiles_per_half
            return [functools.partial(
                lambda h: _dot(kt, qa_ref[half, h]) if mask is None
                else jnp.where(mask, _dot(kt, qa_ref[half, h]), NEG), h) for h in range(HPG)]

        carry = lax.fori_loop(
            0, base, lambda t, c: _softmax_tiles(tile_scores(t, None), [vst_ref[t]] * HPG, c),
            init_carry())
        for r in range(per_q):
            carry = _softmax_tiles(tile_scores(base + r, causal[r]), [vst_ref[base + r]] * HPG, carry)
        return tuple(normalised(acc) for (_, acc) in carry)

    denominators = jnp.concatenate([acc[DH:DH + 1] for acc in accs], axis=0)
    no_overflow = jnp.max(denominators) < OVERFLOW_GUARD
    os_t = lax.cond(no_overflow, lambda: tuple(normalised(acc) for acc in accs), rescaled_sweep)

    gt_ref[...] = gate_ref[...].T
    gt_t = gt_ref[pl.ds(pl.multiple_of(pl.program_id(1) * GATE_LANES, GATE_LANES), GATE_LANES), :]
    outs = [gt_t[h:h + 1, :] * oc_t[h] + gt_t[HPG + h:HPG + h + 1, :] * os_t[h]
            + gt_t[2 * HPG + h:2 * HPG + h + 1, :] * ow_t[h] for h in range(HPG)]
    o_ref[...] = jnp.concatenate(outs, axis=0).T.astype(BF16)


def _nsa(qraw_t, qrot_t, cmp_kv, cmp_kv_t, ovt, kaug, vslc_t, kwin, vwin_t, gates, *, batch, seq,
         tq=NSA_QUERIES, tk=KEY_TILE):
    n = batch * seq
    nq = seq // tq
    nk = seq // tk
    per_q = tq // tk
    n_cmp = cmp_kv.shape[2]
    n_sel = ovt.shape[0]
    topk = min(SLC_TOPK, seq // SLC_LEN)
    qmap = lambda b, g, i: (b * nq + i, g)
    qtmap = lambda b, g, i: (g, b * nq + i, 0, 0)
    kvmap = lambda b, g, i: (g, b, 0)
    vtmap = lambda b, g, i: (g, b, 0, 0)
    return pl.pallas_call(
        functools.partial(_nsa_kernel, tq=tq, tk=tk, topk=topk),
        grid=(batch, NSA_GROUPS, nq),
        in_specs=[
            pl.BlockSpec((None, per_q, HPG * DH, tk), qtmap),
            pl.BlockSpec((None, per_q, HPG * DH, tk), qtmap),
            pl.BlockSpec((None, None, n_cmp, DH), lambda b, g, i: (g, b, 0, 0)),
            pl.BlockSpec((None, None, V_ROWS, n_cmp), lambda b, g, i: (NSA_GROUPS + g, b, 0, 0)),
            pl.BlockSpec((n_sel, n_cmp), lambda b, g, i: (0, 0)),
            pl.BlockSpec((None, seq, 128), kvmap),
            pl.BlockSpec((None, nk, V_ROWS, tk), vtmap),
            pl.BlockSpec((None, seq, DH), kvmap),
            pl.BlockSpec((None, nk, V_ROWS, tk), vtmap),
            pl.BlockSpec((tq, LANE), lambda b, g, i: (b * nq + i, 0)),
        ],
        out_specs=pl.BlockSpec((tq, HPG * DH), qmap),
        out_shape=jax.ShapeDtypeStruct((n, NSA_HEADS * DH), BF16),
        scratch_shapes=[
            pltpu.VMEM((n_sel // SEL_HALF, HPG, DH + SEL_HALF, tq), BF16),
            pltpu.VMEM((LANE, tq), F32),
            pltpu.VMEM((n_sel, tq), F32),
        ],
        compiler_params=pltpu.CompilerParams(
            dimension_semantics=("parallel", "parallel", "arbitrary"),
            vmem_limit_bytes=VMEM_LIMIT),
        name="nsa",
    )(qraw_t, qrot_t, cmp_kv, cmp_kv_t, ovt, kaug, vslc_t, kwin, vwin_t, gates)


def _gla_kernel(q_ref, k_ref, v_ref, g_ref, r_ref, gn_ref, o_ref, st_ref, *, tm, chunk):
    nb = q_ref.shape[0]
    n_chunk = tm // chunk
    n_sub = chunk // GLA_SUB
    dk_all = GLA_HEADS * GLA_DK
    dv_all = GLA_HEADS * GLA_DV

    @pl.when(pl.program_id(0) == 0)
    def _():
        st_ref[...] = jnp.zeros(st_ref.shape, F32)

    def iota(shape, axis):
        return lax.broadcasted_iota(jnp.int32, shape, axis)

    tril = jnp.where((iota((tm, tm), 0) >= iota((tm, tm), 1))
                     & (iota((tm, tm), 0) // chunk == iota((tm, tm), 1) // chunk), 1.0, 0.0).astype(BF16)
    k_head = iota((chunk, dk_all), 1) // GLA_DK
    v_head = iota((chunk, dv_all), 1) // GLA_DV
    a_rows = iota((n_sub * chunk, GLA_HEADS * chunk), 0)
    a_key = iota((n_sub * chunk, GLA_HEADS * chunk), 1) % chunk
    a_keep = (a_key // GLA_SUB == a_rows // chunk) & (a_key <= a_rows % chunk)
    st_keep = iota((dv_all, dk_all), 0) // GLA_DV == iota((dv_all, dk_all), 1) // GLA_DK
    gn = jnp.concatenate([gn_ref[...]] * GLA_HEADS, axis=1)

    work = []
    for b in range(nb):
        g = g_ref[b]
        g_hi = g.astype(BF16)
        g_lo = (g - g_hi.astype(F32)).astype(BF16)
        cum_all = _dot(tril, g_hi) + _dot(tril, g_lo)
        for c in range(n_chunk):
            rows = slice(c * chunk, (c + 1) * chunk)
            cum = cum_all[rows]
            q = q_ref[b, rows, :].astype(F32)
            k = k_ref[b, rows, :].astype(F32)
            v = v_ref[b, rows, :]
            last = cum[chunk - 1:chunk, :]
            refs = [cum[(j + 1) * GLA_SUB - 1:(j + 1) * GLA_SUB, :] for j in range(n_sub)]
            k_hat = jnp.concatenate(
                [k[j * GLA_SUB:(j + 1) * GLA_SUB] * jnp.exp(refs[j] - cum[j * GLA_SUB:(j + 1) * GLA_SUB])
                 for j in range(n_sub)], axis=0)
            q_hat = jnp.concatenate(
                [q * jnp.exp(jnp.minimum(cum - refs[j], 80.0)) for j in range(n_sub)], axis=0)
            k_stack = jnp.concatenate(
                [jnp.where(k_head == h, k_hat, 0.0) for h in range(GLA_HEADS)], axis=0)
            a_all = _dot_nt(q_hat.astype(BF16), k_stack.astype(BF16))
            k_dec = (k * jnp.exp(last - cum)).astype(BF16)
            v_t = v.astype(F32).T.astype(BF16)
            st_inc = _dot(v_t, k_dec)
            work.append(dict(b=b, rows=rows, a_all=a_all, st_inc=st_inc, v=v,
                             q_in=(q * jnp.exp(cum)).astype(BF16), decay=jnp.exp(last)))

    states = [st_ref[b] for b in range(nb)]
    for w in work:
        st = states[w["b"]]
        w["o_inter"] = _dot_nt(w["q_in"], st.astype(BF16))
        states[w["b"]] = st * w["decay"] + jnp.where(st_keep, w["st_inc"], 0.0)
    for b in range(nb):
        st_ref[b] = states[b]

    for w in work:
        a_m = jnp.where(a_keep, w["a_all"], 0.0)
        a = a_m[0:chunk]
        for j in range(1, n_sub):
            a = a + a_m[j * chunk:(j + 1) * chunk]
        v_blocks = jnp.concatenate(
            [jnp.where(v_head == h, w["v"], jnp.zeros_like(w["v"])) for h in range(GLA_HEADS)], axis=0)
        o = w["o_inter"] + _dot(a.astype(BF16), v_blocks)
        scale = jnp.concatenate(
            [jnp.broadcast_to(lax.rsqrt(jnp.mean(
                o[:, h * GLA_DV:(h + 1) * GLA_DV] ** 2, axis=-1, keepdims=True) + EPS), (chunk, GLA_DV))
             for h in range(GLA_HEADS)], axis=1)
        y = o * scale * gn * r_ref[w["b"], w["rows"], :].astype(F32)
        o_ref[w["b"], w["rows"], :] = y.astype(BF16)


def _gla(gq, gk, gv, ga, gr, gnorm, *, batch, seq, tm=KEY_TILE, chunk=GLA_CHUNK):
    tmap = lambda i: (0, i, 0)
    dk_all = GLA_HEADS * GLA_DK
    dv_all = GLA_HEADS * GLA_DV
    return pl.pallas_call(
        functools.partial(_gla_kernel, tm=tm, chunk=chunk),
        grid=(seq // tm,),
        in_specs=[
            pl.BlockSpec((batch, tm, dk_all), tmap),
            pl.BlockSpec((batch, tm, dk_all), tmap),
            pl.BlockSpec((batch, tm, dv_all), tmap),
            pl.BlockSpec((batch, tm, dk_all), tmap),
            pl.BlockSpec((batch, tm, dv_all), tmap),
            pl.BlockSpec((1, GLA_DV), lambda i: (0, 0)),
        ],
        out_specs=pl.BlockSpec((batch, tm, dv_all), tmap),
        out_shape=jax.ShapeDtypeStruct((batch, seq, dv_all), BF16),
        scratch_shapes=[pltpu.VMEM((batch, dv_all, dk_all), F32)],
        compiler_params=pltpu.CompilerParams(
            dimension_semantics=("arbitrary",), vmem_limit_bytes=VMEM_LIMIT),
        name="gla",
    )(gq, gk, gv, ga, gr, gnorm)


def _tail_kernel(x_ref, oa_ref, ob_ref, g0_ref, wm_ref, wn_ref, wb_ref, wo_ref,
                 g1_ref, wg_ref, wu_ref, wd_ref, g2_ref, o_ref):
    x = x_ref[...]

    def rms(v, gain_ref):
        return v * lax.rsqrt(jnp.mean(v * v, axis=-1, keepdims=True) + EPS) * gain_ref[...]

    gates = _sigmoid(_dot(rms(x, g0_ref).astype(BF16), wm_ref[...]))
    merged = (gates[:, 0:D_MODEL] * _dot(oa_ref[...], wn_ref[...])
              + gates[:, D_MODEL:2 * D_MODEL] * _dot(ob_ref[...], wb_ref[...]))
    x1 = x + _dot(merged.astype(BF16), wo_ref[...])
    h = rms(x1, g1_ref).astype(BF16)
    a = _dot(h, wg_ref[...])
    u = _dot(h, wu_ref[...])
    y = x1 + _dot((a * _sigmoid(a) * u).astype(BF16), wd_ref[...])
    o_ref[...] = rms(y, g2_ref)


def _tail(x2d, oa, ob, g0, wm, wn, wb, wo, g1, wg, wu, wd, g2, *, tm=256):
    n = x2d.shape[0]
    row = lambda i: (i, 0)
    const = lambda i: (0, 0)
    weight = lambda r, c: pl.BlockSpec((r, c), const, pipeline_mode=pl.Buffered(1))
    return pl.pallas_call(
        _tail_kernel,
        grid=(n // tm,),
        in_specs=[
            pl.BlockSpec((tm, D_MODEL), row),
            pl.BlockSpec((tm, NSA_HEADS * DH), row),
            pl.BlockSpec((tm, GLA_HEADS * GLA_DV), row),
            weight(1, D_MODEL), weight(D_MODEL, 2 * D_MODEL),
            weight(NSA_HEADS * DH, D_MODEL), weight(GLA_HEADS * GLA_DV, D_MODEL),
            weight(D_MODEL, D_MODEL),
            weight(1, D_MODEL), weight(D_MODEL, D_FF), weight(D_MODEL, D_FF), weight(D_FF, D_MODEL),
            weight(1, D_MODEL),
        ],
        out_specs=pl.BlockSpec((tm, D_MODEL), row),
        out_shape=jax.ShapeDtypeStruct((n, D_MODEL), F32),
        compiler_params=pltpu.CompilerParams(
            dimension_semantics=("parallel",), vmem_limit_bytes=VMEM_LIMIT),
        name="tail",
    )(x2d, oa, ob, g0, wm, wn, wb, wo, g1, wg, wu, wd, g2)


def _rope_tables(seq):
    half = ROPE_DIM // 2
    inv_freq = ROPE_THETA ** (-jnp.arange(half, dtype=F32) / half)
    ang = jnp.arange(seq).astype(F32)[:, None] * inv_freq[None, :]
    cos = jnp.cos(ang)
    sin = jnp.sin(ang)
    return jnp.concatenate([cos, cos], axis=1), jnp.concatenate([-sin, sin], axis=1)


def _overlap_t(n_sel, n_cmp_pad, n_slc, n_cmp):
    c_start = np.arange(n_cmp_pad)[None, :] * CMP_STRIDE
    s_start = np.arange(n_sel)[:, None] * SLC_LEN
    ov = (c_start < s_start + SLC_LEN) & (c_start + CMP_LEN > s_start)
    ov &= (np.arange(n_cmp_pad)[None, :] < n_cmp) & (np.arange(n_sel)[:, None] < n_slc)
    return jnp.asarray(ov, dtype=BF16)


def _split_in_weights(w_in):
    c = 0
    w_qkv = w_in[:, c:c + 1280]; c += 1280
    wgate = w_in[:, c:c + 24]; c += 24
    w_gla = w_in[:, c:c + 1024]; c += 1024
    wlr = w_in[:, c:c + GLA_RANK]; c += GLA_RANK
    w_r = w_in[:, c:c + 512]; c += 512
    w_merge = w_in[:, c:c + 2048]
    d = w_in.shape[0]
    wg3 = wgate.reshape(d, NSA_GROUPS, HPG, 3).transpose(0, 1, 3, 2).reshape(d, NSA_GROUPS, 3 * HPG)
    wg_pad = jnp.pad(wg3, ((0, 0), (0, 0), (0, GATE_LANES - 3 * HPG))).reshape(d, NSA_GROUPS * GATE_LANES)
    w_misc = jnp.concatenate([
        wg_pad, jnp.zeros((d, LR_LANE - NSA_GROUPS * GATE_LANES), F32),
        wlr, jnp.zeros((d, LANE - LR_LANE - GLA_RANK), F32)], axis=1)
    return tuple(w.astype(BF16) for w in (w_qkv, w_misc, w_gla, w_r, w_merge))


def kernel(x, norm_mix, w_in, cmp_pe_k, cmp_pe_v, cmp_k_w1, cmp_k_w2, cmp_v_w1, cmp_v_w2,
           gla_gate_w2, gla_gate_b, gla_norm, w_up_nsa, w_up_gla, w_out, norm_ffn,
           w_ffn_gate, w_ffn_up, w_ffn_down, norm_final):
    batch, seq, d = x.shape
    assert d == D_MODEL and w_in.shape[0] == 1
    assert seq % IN_ROWS == 0 and seq % NSA_QUERIES == 0
    n = batch * seq
    x2d = x.reshape(n, d)

    *in_weights, w_merge = _split_in_weights(w_in[0])
    cos_t, sin_t = _rope_tables(seq)
    w2p = jnp.pad(gla_gate_w2[0], ((LR_LANE, LANE - LR_LANE - GLA_RANK), (0, 0))).astype(BF16)
    b2 = gla_gate_b[0].reshape(1, -1)
    gain_mix = norm_mix[0].reshape(1, d)

    (qraw, qrot, cmp_src, kaug, vslc_t, kwin, vwin_t, gates, gq, gk, gv, ga, gr) = _in_proj(
        x2d, gain_mix, in_weights, cos_t, sin_t, w2p, b2, seq=seq)

    n_sub = seq // CMP_STRIDE
    n_cmp = n_sub - CMP_LEN // CMP_STRIDE + 1
    n_slc = seq // SLC_LEN
    cmp_in = cmp_src.reshape(4, batch, n_sub, CMP_STRIDE * DH)
    w1 = jnp.stack([cmp_k_w1[0], cmp_v_w1[0]]).astype(BF16)
    w2 = jnp.stack([cmp_k_w2[0], cmp_v_w2[0]]).astype(BF16)
    pe = jnp.stack([cmp_pe_k[0].reshape(1, -1), cmp_pe_v[0].reshape(1, -1)])
    pe = jnp.broadcast_to(pe, (2, 8, CMP_LEN * DH)).astype(BF16)
    w2t = jnp.stack([cmp_k_w2[0].T, cmp_v_w2[0].T]).astype(BF16)
    cmp_kv, cmp_kv_t = _compress(cmp_in, w1, w2, w2t, pe, batch=batch, n_sub=n_sub)

    n_sel = -(-n_slc // SEL_HALF) * SEL_HALF
    ovt = _overlap_t(n_sel, n_sub, n_slc, n_cmp)
    o_a = _nsa(qraw, qrot, cmp_kv, cmp_kv_t, ovt, kaug, vslc_t, kwin, vwin_t, gates,
               batch=batch, seq=seq)

    per_seq = lambda a: a.reshape(batch, seq, a.shape[-1])
    o_b = _gla(per_seq(gq), per_seq(gk), per_seq(gv), per_seq(ga), per_seq(gr),
               gla_norm[0].reshape(1, -1), batch=batch, seq=seq).reshape(n, -1)

    out = _tail(x2d, o_a, o_b, gain_mix, w_merge, w_up_nsa[0].astype(BF16), w_up_gla[0].astype(BF16),
                w_out[0].astype(BF16), norm_ffn[0].reshape(1, d), w_ffn_gate[0].astype(BF16),
                w_ffn_up[0].astype(BF16), w_ffn_down[0].astype(BF16), norm_final.reshape(1, d))
    return out.reshape(batch, seq, d)
```
